```python
import math
import jax, jax.numpy as jnp
from jax import lax
import numpy as np

D_MODEL = 2048
BATCH = 4
SEQ = 2048
DEPTH = 2
DEC_BATCH = 32
DEC_SEQ = 4
PAST_LEN = 16384
PAGE_SIZE = 128

N_AB_LAYERS = (DEPTH + 1) // 2
N_C_LAYERS = DEPTH // 2
EPS = 1e-6
POOL_WINDOWS = (2, 4, 8, 16)
N_POOL_GROUPS = len(POOL_WINDOWS)
D_POOL = D_MODEL // 4
POOL_GROUP = D_POOL // N_POOL_GROUPS
POOL_STATE = max(POOL_WINDOWS) - 1
HEAD_DIM = 64
N_Q_HEADS = (D_MODEL - D_POOL) // HEAD_DIM
N_KV_HEADS = 4
Q_PER_KV = N_Q_HEADS // N_KV_HEADS
WINDOW = 128
D_ATTN = N_Q_HEADS * HEAD_DIM
D_KV = N_KV_HEADS * HEAD_DIM
D_IN_AB = D_POOL + D_ATTN + 2 * D_KV
D_MIX_AB = D_POOL + D_ATTN
C_DK = 128
C_DV = 128
C_HEADS = D_MODEL // C_DK
D_C = C_HEADS * C_DK
C_CHUNK = 64
N_EXPERTS = 16
N_GROUPS = 4
EXP_PER_GROUP = N_EXPERTS // N_GROUPS
TOP_K = 2
D_FF = 512
NEG = -1e30

kernel_name = "hybrid_pool_swa_hgrn2_moe_decode_step"


def rms_norm(x, g):
    xf = x.astype(jnp.float32)
    y = xf * lax.rsqrt(jnp.mean(xf * xf, axis=-1, keepdims=True) + EPS)
    return (y * g.astype(jnp.float32)).astype(x.dtype)


def alibi_slopes(n):
    def pow2_slopes(m):
        start = 2.0 ** (-8.0 / m)
        return [start ** (i + 1) for i in range(m)]
    if n & (n - 1) == 0:
        s = pow2_slopes(n)
    else:
        c = 2 ** int(math.floor(math.log2(n)))
        s = pow2_slopes(c) + pow2_slopes(2 * c)[0::2][: n - c]
    return np.asarray(s, dtype=np.float32)


def multiscale_pool(u_ext, n_prev, pool_w, pool_scale):
    bsz, length, _ = u_ext.shape
    uf = u_ext.astype(jnp.float32)
    cs = jnp.concatenate([jnp.zeros((bsz, 1, D_POOL), jnp.float32), jnp.cumsum(uf, axis=1)], axis=1)
    hi = np.arange(n_prev + 1, length + 1)
    u_new = uf[:, n_prev:]
    diffs = []
    for g, w in enumerate(POOL_WINDOWS):
        ch = slice(g * POOL_GROUP, (g + 1) * POOL_GROUP)
        lo = np.maximum(hi - w, 0)
        cnt = (hi - lo).astype(np.float32)[None, :, None]
        diffs.append((cs[:, hi, ch] - cs[:, lo, ch]) / cnt - u_new[:, :, ch])
    d = jnp.stack(diffs, axis=2)
    y = jnp.einsum('btgc,gce->btge', d, pool_w.astype(jnp.float32))
    y = y.reshape(bsz, length - n_prev, D_POOL) * pool_scale.astype(jnp.float32)
    return y.astype(u_ext.dtype)


def banded_blocks(a, nb):
    bsz, t = a.shape[0], a.shape[1]
    prev = jnp.pad(a, ((0, 0), (WINDOW, 0), (0, 0), (0, 0)))[:, :t]
    shp = (bsz, nb, WINDOW) + a.shape[2:]
    return jnp.concatenate([prev.reshape(shp), a.reshape(shp)], axis=2)


def sink_attention(q, k, v, qpos, kpos, sinks):
    s = jnp.einsum('bnqkgd,bnskd->bnkgqs', q.astype(jnp.float32), k.astype(jnp.float32)) * (HEAD_DIM ** -0.5)
    dist = qpos[:, :, None] - kpos[:, None, :]
    valid = (kpos[:, None, :] >= 0) & (dist >= 0) & (dist < WINDOW)
    slopes = jnp.asarray(alibi_slopes(N_Q_HEADS)).reshape(1, 1, N_KV_HEADS, Q_PER_KV, 1, 1)
    s = s - slopes * jnp.asarray(dist, jnp.float32)[None, :, None, None]
    s = jnp.where(valid[None, :, None, None], s, NEG)
    sink = jnp.broadcast_to(sinks.astype(jnp.float32).reshape(1, 1, N_KV_HEADS, Q_PER_KV, 1, 1), s.shape[:-1] + (1,))
    p = jax.nn.softmax(jnp.concatenate([s, sink], axis=-1), axis=-1)[..., :-1]
    return jnp.einsum('bnkgqs,bnskd->bnqkgd', p, v.astype(jnp.float32))


def mixer_ab(h, start, w_in, pool_w, pool_scale, q_norm, k_norm, sinks, w_out, pool_buf, k_buf, v_buf):
    bsz, t, _ = h.shape
    u = jnp.einsum('btd,de->bte', h, w_in)
    u_a, q, k, v = jnp.split(u, [D_POOL, D_POOL + D_ATTN, D_POOL + D_ATTN + D_KV], axis=-1)
    q = rms_norm(q.reshape(bsz, t, N_KV_HEADS, Q_PER_KV, HEAD_DIM), q_norm)
    k = rms_norm(k.reshape(bsz, t, N_KV_HEADS, HEAD_DIM), k_norm)
    v = v.reshape(bsz, t, N_KV_HEADS, HEAD_DIM)
    if pool_buf is None:
        pool_ext, n_prev = u_a, 0
        nb = t // WINDOW
        qb = q.reshape(bsz, nb, WINDOW, N_KV_HEADS, Q_PER_KV, HEAD_DIM)
        kb, vb = banded_blocks(k, nb), banded_blocks(v, nb)
        qpos = start + np.arange(t).reshape(nb, WINDOW)
        kpos = np.concatenate([qpos - WINDOW, qpos], axis=1)
        k_ext, v_ext = k, v
    else:
        pool_ext, n_prev = jnp.concatenate([pool_buf, u_a], axis=1), POOL_STATE
        k_ext = jnp.concatenate([k_buf, k], axis=1)
        v_ext = jnp.concatenate([v_buf, v], axis=1)
        qb, kb, vb = q[:, None], k_ext[:, None], v_ext[:, None]
        qpos = (start + np.arange(t))[None]
        kpos = (start - WINDOW + np.arange(WINDOW + t))[None]
    pooled = multiscale_pool(pool_ext, n_prev, pool_w, pool_scale)
    attn = sink_attention(qb, kb, vb, qpos, kpos, sinks).reshape(bsz, t, D_ATTN).astype(h.dtype)
    y = jnp.einsum('bte,ed->btd', jnp.concatenate([pooled, attn], axis=-1), w_out)
    return y, pool_ext[:, -POOL_STATE:], k_ext[:, -WINDOW:], v_ext[:, -WINDOW:]


def hgrn2_chunked(q, k, v, logf, s0, chunk):
    bsz, t, nh, dk = q.shape
    dv = v.shape[-1]
    nc = t // chunk

    def to_chunks(a):
        return a.reshape(bsz, nc, chunk, nh, a.shape[-1]).transpose(1, 0, 3, 2, 4)

    causal = np.tril(np.ones((chunk, chunk), dtype=bool))[:, :, None]

    def step(S, inp):
        qc, kc, vc, gc = inp
        b = jnp.cumsum(gc, axis=2)
        diff = b[:, :, :, None, :] - b[:, :, None, :, :]
        decay = jnp.where(causal, jnp.exp(jnp.minimum(diff, 0.0)), 0.0)
        scores = jnp.einsum('bhtk,bhtsk,bhsk->bhts', qc, decay, kc)
        o = jnp.einsum('bhts,bhsv->bhtv', scores, vc) + jnp.einsum('bhtk,bhkv->bhtv', qc * jnp.exp(b), S)
        b_last = b[:, :, -1:, :]
        S_new = jnp.exp(b_last[:, :, 0, :, None]) * S + jnp.einsum('bhsk,bhsv->bhkv', kc * jnp.exp(b_last - b), vc)
        return S_new, o

    S, o = lax.scan(step, s0, (to_chunks(q), to_chunks(k), to_chunks(v), to_chunks(logf)))
    o = o.transpose(1, 0, 3, 2, 4).reshape(bsz, t, nh, dv)
    return o, S


def mixer_c(h, lb, w_in, o_norm, w_out, s0):
    bsz, t, _ = h.shape
    u = jnp.einsum('btd,de->bte', h, w_in).astype(jnp.float32)
    q, f_pre, i_in, g = jnp.split(u, 4, axis=-1)
    shp = (bsz, t, C_HEADS, C_DK)
    q = jax.nn.silu(q).reshape(shp)
    logf = jnp.logaddexp(jnp.log(lb), jnp.log1p(-lb) + jax.nn.log_sigmoid(f_pre)).reshape(shp)
    k = ((1.0 - lb) * jax.nn.sigmoid(-f_pre)).reshape(shp)
    v = i_in.reshape(bsz, t, C_HEADS, C_DV)
    if s0 is None:
        s0 = jnp.zeros((bsz, C_HEADS, C_DK, C_DV), jnp.float32)
    chunk = C_CHUNK if t % C_CHUNK == 0 else t
    o, S = hgrn2_chunked(q, k, v, logf, s0.astype(jnp.float32), chunk)
    o = rms_norm(o, o_norm) * jax.nn.silu(g.reshape(bsz, t, C_HEADS, C_DV))
    y = jnp.einsum('bte,ed->btd', o.reshape(bsz, t, D_C).astype(h.dtype), w_out)
    return y, S.astype(h.dtype)


def grouped_moe(h, router_w, router_b, w1, w3, w2):
    bsz, t, d = h.shape
    hf = h.reshape(bsz * t, d)
    scores = jax.nn.sigmoid(jnp.dot(hf.astype(jnp.float32), router_w.astype(jnp.float32)))
    sel = scores + router_b.astype(jnp.float32)
    group_score = lax.top_k(sel.reshape(-1, N_GROUPS, EXP_PER_GROUP), 2)[0].sum(-1)
    best_group = jnp.argmax(group_score, axis=-1)
    in_group = (jnp.arange(N_EXPERTS) // EXP_PER_GROUP)[None, :] == best_group[:, None]
    _, idx = lax.top_k(jnp.where(in_group, sel, NEG), TOP_K)
    w = jnp.take_along_axis(scores, idx, axis=-1)
    w = w / jnp.sum(w, axis=-1, keepdims=True)
    combine = jnp.sum(jax.nn.one_hot(idx, N_EXPERTS, dtype=jnp.float32) * w[..., None], axis=1)
    hid = jax.nn.silu(jnp.einsum('nd,edf->enf', hf, w1)) * jnp.einsum('nd,edf->enf', hf, w3)
    hid = hid * combine.T[:, :, None].astype(hid.dtype)
    out = jnp.einsum('enf,efd->nd', hid, w2)
    return out.reshape(bsz, t, d)


def run_trunk(x, c, start, p, pool_bufs, k_bufs, v_bufs, hgrn_states):
    new_pool, new_k, new_v, new_s = [], [], [], []
    lb_soft = jax.nn.softmax(p['lb_param'].astype(jnp.float32), axis=0)
    lower_bounds = jnp.cumsum(lb_soft, axis=0) - lb_soft[0:1]
    for layer in range(DEPTH):
        mod = jnp.dot(jax.nn.silu(c), p['ada_w'][layer]) + p['ada_b'][layer]
        sh_m, sc_m, g_m, sh_f, sc_f, g_f = jnp.split(mod, 6, axis=-1)
        h = rms_norm(x, p['norm_g'][layer, 0]) * (1.0 + sc_m[:, None]) + sh_m[:, None]
        j = layer // 2
        if layer % 2 == 0:
            if pool_bufs is None:
                pb, kb, vb = None, None, None
            else:
                pb, kb, vb = pool_bufs[j], k_bufs[j], v_bufs[j]
            y, npb, nkb, nvb = mixer_ab(h, start, p['w_in_ab'][j], p['pool_w'][j], p['pool_scale'][j],
                                        p['q_norm'][j], p['k_norm'][j], p['attn_sinks'][j], p['w_out_ab'][j],
                                        pb, kb, vb)
            new_pool.append(npb)
            new_k.append(nkb)
            new_v.append(nvb)
        else:
            s0 = None if hgrn_states is None else hgrn_states[j]
            y, ns = mixer_c(h, lower_bounds[layer], p['w_in_c'][j], p['o_norm'][j], p['w_out_c'][j], s0)
            new_s.append(ns)
        x = x + g_m[:, None] * y
        h = rms_norm(x, p['norm_g'][layer, 1]) * (1.0 + sc_f[:, None]) + sh_f[:, None]
        x = x + g_f[:, None] * grouped_moe(h, p['router_w'], p['router_b'],
                                           p['moe_w1'][layer], p['moe_w3'][layer], p['moe_w2'][layer])
    return x, jnp.stack(new_pool), jnp.stack(new_k), jnp.stack(new_v), jnp.stack(new_s)


def setup_inputs(seed: int = 0) -> dict:
    key = jax.random.key(seed)
    ks = jax.random.split(key, 32)
    f32 = jnp.float32

    def nrm(k, shape, scale):
        return jax.random.normal(k, shape, f32) * scale

    return {
        "x_prompt": nrm(ks[0], (BATCH, SEQ, D_MODEL), 1.0),
        "x_sample": nrm(ks[1], (DEC_BATCH, DEC_SEQ, D_MODEL), 1.0),
        "c_prompt": nrm(ks[2], (BATCH, D_MODEL), 1.0),
        "c_sample": nrm(ks[3], (DEC_BATCH, D_MODEL), 1.0),
        "state_pool": nrm(ks[4], (N_AB_LAYERS, DEC_BATCH, POOL_STATE, D_POOL), 1.0),
        "cache_k": nrm(ks[5], (N_AB_LAYERS, DEC_BATCH, WINDOW, N_KV_HEADS, HEAD_DIM), 1.0),
        "cache_v": nrm(ks[6], (N_AB_LAYERS, DEC_BATCH, WINDOW, N_KV_HEADS, HEAD_DIM), 1.0),
        "state_hgrn": nrm(ks[7], (N_C_LAYERS, DEC_BATCH, C_HEADS, C_DK, C_DV), 0.5),
        "norm_g": 1.0 + nrm(ks[8], (DEPTH, 2, D_MODEL), 0.1),
        "ada_w": nrm(ks[9], (DEPTH, D_MODEL, 6 * D_MODEL), 0.5 * D_MODEL ** -0.5),
        "ada_b": nrm(ks[10], (DEPTH, 6 * D_MODEL), 0.02),
        "w_in_ab": nrm(ks[11], (N_AB_LAYERS, D_MODEL, D_IN_AB), D_MODEL ** -0.5),
        "pool_w": nrm(ks[12], (N_AB_LAYERS, N_POOL_GROUPS, POOL_GROUP, POOL_GROUP), POOL_GROUP ** -0.5),
        "pool_scale": 1.0 + nrm(ks[13], (N_AB_LAYERS, D_POOL), 0.1),
        "q_norm": 1.0 + nrm(ks[14], (N_AB_LAYERS, HEAD_DIM), 0.1),
        "k_norm": 1.0 + nrm(ks[15], (N_AB_LAYERS, HEAD_DIM), 0.1),
        "attn_sinks": nrm(ks[16], (N_AB_LAYERS, N_Q_HEADS), 0.5),
        "w_out_ab": nrm(ks[17], (N_AB_LAYERS, D_MIX_AB, D_MODEL), D_MIX_AB ** -0.5),
        "w_in_c": nrm(ks[18], (N_C_LAYERS, D_MODEL, 4 * D_C), D_MODEL ** -0.5),
        "lb_param": nrm(ks[19], (DEPTH, D_C), 0.5),
        "o_norm": 1.0 + nrm(ks[20], (N_C_LAYERS, C_DV), 0.1),
        "w_out_c": nrm(ks[21], (N_C_LAYERS, D_C, D_MODEL), D_C ** -0.5),
        "router_w": nrm(ks[22], (D_MODEL, N_EXPERTS), D_MODEL ** -0.5),
        "router_b": nrm(ks[23], (N_EXPERTS,), 0.01),
        "moe_w1": nrm(ks[24], (DEPTH, N_EXPERTS, D_MODEL, D_FF), D_MODEL ** -0.5),
        "moe_w3": nrm(ks[25], (DEPTH, N_EXPERTS, D_MODEL, D_FF), D_MODEL ** -0.5),
        "moe_w2": nrm(ks[26], (DEPTH, N_EXPERTS, D_FF, D_MODEL), D_FF ** -0.5),
    }


def reference(x_prompt, x_sample, c_prompt, c_sample, state_pool, cache_k, cache_v, state_hgrn,
              norm_g, ada_w, ada_b, w_in_ab, pool_w, pool_scale, q_norm, k_norm, attn_sinks, w_out_ab,
              w_in_c, lb_param, o_norm, w_out_c, router_w, router_b, moe_w1, moe_w3, moe_w2):
    p = dict(norm_g=norm_g, ada_w=ada_w, ada_b=ada_b, w_in_ab=w_in_ab, pool_w=pool_w, pool_scale=pool_scale,
             q_norm=q_norm, k_norm=k_norm, attn_sinks=attn_sinks, w_out_ab=w_out_ab, w_in_c=w_in_c,
             lb_param=lb_param, o_norm=o_norm, w_out_c=w_out_c, router_w=router_w, router_b=router_b,
             moe_w1=moe_w1, moe_w3=moe_w3, moe_w2=moe_w2)
    y_prompt, pool_p, k_p, v_p, hgrn_p = run_trunk(x_prompt, c_prompt, 0, p, None, None, None, None)
    y_sample, pool_s, k_s, v_s, hgrn_s = run_trunk(x_sample, c_sample, PAST_LEN, p,
                                                   state_pool, cache_k, cache_v, state_hgrn)
    return (y_prompt, y_sample, pool_p, pool_s, k_p, k_s, v_p, v_s, hgrn_p, hgrn_s)
```

```python
import functools
import math

import numpy as np
import jax
import jax.numpy as jnp
from jax import lax
from jax.experimental import pallas as pl
from jax.experimental.pallas import tpu as pltpu

f32 = jnp.float32
bf16 = jnp.bfloat16
i32 = jnp.int32

D_MODEL = 2048
BATCH = 4
SEQ = 2048
DEPTH = 2
DEC_BATCH = 32
DEC_SEQ = 4
EPS = 1e-6
POOL_WINDOWS = (2, 4, 8, 16)
D_POOL = 512
POOL_GROUP = 128
POOL_STATE = 15
HEAD_DIM = 64
N_Q_HEADS = 24
N_KV_HEADS = 4
Q_PER_KV = 6
WINDOW = 128
D_ATTN = 1536
D_KV = 256
D_IN_AB = 2560
C_DK = 128
C_HEADS = 16
D_C = 2048
N_EXPERTS = 16
EXP_PER_GROUP = 4
D_FF = 512
NEG = -1e30

LANES = 128
VMEM_LIMIT = 56 * 1024 * 1024

NP = BATCH * SEQ
NS = DEC_BATCH * DEC_SEQ
N_REAL = NP + NS
TM = 512
M_ALL = ((N_REAL + TM - 1) // TM) * TM
N_TILES = M_ALL // TM
NPT = NP // TM
TPB = SEQ // TM
SP = 16
NSP = DEC_BATCH * SP
DCH = D_MODEL // LANES

TME = 256
R_MAX = ((2 * N_REAL + N_EXPERTS * (TME - 1)) // TME + 1) * TME
NT_E = R_MAX // TME
TMC = 128
DMA_WINDOW = 64


def _cparams(sem, vmem=VMEM_LIMIT):
    return pltpu.CompilerParams(dimension_semantics=sem, vmem_limit_bytes=vmem)


def _silu(x):
    return x * jax.nn.sigmoid(x)


def _nt_dot(a, b, **kw):
    return lax.dot_general(a, b, (((1,), (1,)), ((), ())), preferred_element_type=f32, **kw)


def _tn_dot(a, b):
    return lax.dot_general(a, b, (((0,), (0,)), ((), ())), preferred_element_type=f32)


def _ada_body(c_ref, w_ref, b_ref, o_ref):
    s = _silu(c_ref[...]).astype(bf16)
    o_ref[...] = jnp.dot(s, w_ref[...].astype(bf16), preferred_element_type=f32) + b_ref[...]


def _ada_mod(c_all, ada_w, ada_b):
    depth, d, e = ada_w.shape
    cr = c_all.shape[0]
    tn = 1024
    return pl.pallas_call(
        _ada_body,
        grid=(depth, e // tn),
        in_specs=[
            pl.BlockSpec((cr, d), lambda l, j: (0, 0)),
            pl.BlockSpec((None, d, tn), lambda l, j: (l, 0, j)),
            pl.BlockSpec((None, 1, tn), lambda l, j: (l, 0, j)),
        ],
        out_specs=pl.BlockSpec((None, cr, tn), lambda l, j: (l, 0, j)),
        out_shape=jax.ShapeDtypeStruct((depth, cr, e), f32),
        compiler_params=_cparams(("arbitrary", "arbitrary")),
        name="ada_mod",
    )(c_all, ada_w, ada_b.reshape(depth, 1, e))


def _mod_specs(layer, k):
    p = pl.BlockSpec((None, None, None, 1, D_MODEL),
                     lambda i: (layer, k, jnp.minimum(i // TPB, BATCH - 1), 0, 0))
    s = pl.BlockSpec((None, None, NS, D_MODEL), lambda i: (layer, k, 0, 0))
    return p, s


def _norm_rows(x, g):
    return x * lax.rsqrt(jnp.mean(x * x, axis=-1, keepdims=True) + EPS) * g


def _normmod_body(x_ref, g_ref, scp_ref, shp_ref, scs_ref, shs_ref, h_ref):
    i = pl.program_id(0)
    y = _norm_rows(x_ref[...], g_ref[...])

    @pl.when(i < NPT)
    def _():
        h_ref[...] = (y * (1.0 + scp_ref[...]) + shp_ref[...]).astype(h_ref.dtype)

    @pl.when(i >= NPT)
    def _():
        h_ref[:NS] = (y[:NS] * (1.0 + scs_ref[...]) + shs_ref[...]).astype(h_ref.dtype)
        h_ref[NS:] = jnp.zeros((TM - NS, D_MODEL), h_ref.dtype)


def _normmod(x_all, g, modp, mods, layer, k_shift, k_scale):
    scp, scs = _mod_specs(layer, k_scale)
    shp, shs = _mod_specs(layer, k_shift)
    return pl.pallas_call(
        _normmod_body,
        grid=(N_TILES,),
        in_specs=[pl.BlockSpec((TM, D_MODEL), lambda i: (i, 0)),
                  pl.BlockSpec((1, D_MODEL), lambda i: (0, 0)),
                  scp, shp, scs, shs],
        out_specs=pl.BlockSpec((TM, D_MODEL), lambda i: (i, 0)),
        out_shape=jax.ShapeDtypeStruct((M_ALL, D_MODEL), bf16),
        compiler_params=_cparams(("arbitrary",)),
        name="normmod",
    )(x_all, g.reshape(1, D_MODEL), modp, modp, mods, mods)


def _proj_body(h_ref, w_ref, o_ref, wb_ref):
    @pl.when(pl.program_id(1) == 0)
    def _():
        wb_ref[...] = w_ref[...].astype(bf16)

    o_ref[...] = jnp.dot(h_ref[...], wb_ref[...], preferred_element_type=f32)


def _proj(h_all, w, tn):
    k, e = w.shape
    return pl.pallas_call(
        _proj_body,
        grid=(e // tn, N_TILES),
        in_specs=[pl.BlockSpec((TM, k), lambda j, i: (i, 0)),
                  pl.BlockSpec((k, tn), lambda j, i: (0, j))],
        out_specs=pl.BlockSpec((TM, tn), lambda j, i: (i, j)),
        out_shape=jax.ShapeDtypeStruct((M_ALL, e), f32),
        scratch_shapes=[pltpu.VMEM((k, tn), bf16)],
        compiler_params=_cparams(("arbitrary", "arbitrary")),
        name="proj",
    )(h_all, w)


def _oproj_body(*refs, n_lhs):
    ap_refs = refs[:n_lhs]
    as_refs = refs[n_lhs:2 * n_lhs]
    w_ref, x_ref, gp_ref, gs_ref, o_ref, wb_ref = refs[2 * n_lhs:]
    i = pl.program_id(1)

    @pl.when(i == 0)
    def _():
        wb_ref[...] = w_ref[...].astype(bf16)

    def matmul(a_refs, rows):
        off, acc = 0, None
        for a in a_refs:
            kk = a.shape[1]
            part = jnp.dot(a[:rows], wb_ref[off:off + kk, :], preferred_element_type=f32)
            acc = part if acc is None else acc + part
            off += kk
        return acc

    @pl.when(i < NPT)
    def _():
        o_ref[...] = x_ref[...] + gp_ref[...] * matmul(ap_refs, TM)

    @pl.when(i >= NPT)
    def _():
        o_ref[:NS] = x_ref[:NS] + gs_ref[...] * matmul(as_refs, NS)
        o_ref[NS:] = jnp.zeros((TM - NS, o_ref.shape[1]), f32)


def _oproj(a_prompt, a_sample, w, x_all, modp, mods, layer, k_gate):
    n_lhs = len(a_prompt)
    kdim, e = w.shape
    tn = 512
    in_specs = []
    for a in a_prompt:
        in_specs.append(pl.BlockSpec((TM, a.shape[1]), lambda j, i: (jnp.minimum(i, NPT - 1), 0)))
    for a in a_sample:
        in_specs.append(pl.BlockSpec((NS, a.shape[1]), lambda j, i: (0, 0)))
    in_specs += [
        pl.BlockSpec((kdim, tn), lambda j, i: (0, j)),
        pl.BlockSpec((TM, tn), lambda j, i: (i, j)),
        pl.BlockSpec((None, None, None, 1, tn),
                     lambda j, i: (layer, k_gate, jnp.minimum(i // TPB, BATCH - 1), 0, j)),
        pl.BlockSpec((None, None, NS, tn), lambda j, i: (layer, k_gate, 0, j)),
    ]
    return pl.pallas_call(
        functools.partial(_oproj_body, n_lhs=n_lhs),
        grid=(e // tn, N_TILES),
        in_specs=in_specs,
        out_specs=pl.BlockSpec((TM, tn), lambda j, i: (i, j)),
        out_shape=jax.ShapeDtypeStruct((M_ALL, e), f32),
        scratch_shapes=[pltpu.VMEM((kdim, tn), bf16)],
        compiler_params=_cparams(("arbitrary", "arbitrary")),
        name="oproj",
    )(*a_prompt, *a_sample, w, x_all, modp, mods)


def _head_norm(x, w_row, scale):
    lane = lax.broadcasted_iota(i32, x.shape, 1)
    lo = lane < HEAD_DIM
    ss = x * x
    m_lo = jnp.sum(jnp.where(lo, ss, 0.0), axis=-1, keepdims=True)
    m_hi = jnp.sum(jnp.where(lo, 0.0, ss), axis=-1, keepdims=True)
    r = jnp.where(lo, lax.rsqrt(m_lo * (1.0 / HEAD_DIM) + EPS), lax.rsqrt(m_hi * (1.0 / HEAD_DIM) + EPS))
    return x * r * (w_row * scale)


def _mix0_pre_body(cur_ref, halo_ref, q0_ref, q1_ref, q2_ref, k_ref, pw_ref, ps_ref, qw_ref, kw_ref,
                   pool_ref, qn_ref, kn_ref, ext_ref, *, tp, tiles_per_seq, full_count):
    i = pl.program_id(0)
    halo = halo_ref[...]
    if not full_count:
        halo = jnp.where(i % tiles_per_seq == 0, 0.0, halo)
    ext_ref[0:16, :] = halo
    ext_ref[16:16 + tp, :] = cur_ref[...]
    t_pos = (i % tiles_per_seq) * tp + lax.broadcasted_iota(i32, (tp, 1), 0)
    for g, w in enumerate(POOL_WINDOWS):
        ch = slice(g * POOL_GROUP, (g + 1) * POOL_GROUP)
        acc = ext_ref[16:16 + tp, ch]
        for dlt in range(1, w):
            acc = acc + ext_ref[16 - dlt:16 - dlt + tp, ch]
        if full_count:
            mean = acc * (1.0 / w)
        else:
            cnt = jnp.minimum(t_pos + 1, w).astype(f32)
            mean = acc / cnt
        d = mean - ext_ref[16:16 + tp, ch]
        y = jnp.dot(d.astype(bf16), pw_ref[g].astype(bf16), preferred_element_type=f32)
        pool_ref[:, ch] = (y * ps_ref[:, ch]).astype(pool_ref.dtype)
    for c3, q_ref in enumerate((q0_ref, q1_ref, q2_ref)):
        for c in range(4):
            sl = slice(c * LANES, (c + 1) * LANES)
            dst = slice((c3 * 4 + c) * LANES, (c3 * 4 + c + 1) * LANES)
            qn_ref[:, dst] = _head_norm(q_ref[:, sl], qw_ref[...], HEAD_DIM ** -0.5).astype(qn_ref.dtype)
    for c in range(2):
        sl = slice(c * LANES, (c + 1) * LANES)
        kn_ref[:, sl] = _head_norm(k_ref[:, sl], kw_ref[...], 1.0)


def _mix0_pre(u, halo_src, halo_map, rows, tp, tiles_per_seq, full_count, pool_w, pool_scale, q_norm, k_norm):
    qw = jnp.tile(q_norm.reshape(1, HEAD_DIM), (1, 2))
    kw = jnp.tile(k_norm.reshape(1, HEAD_DIM), (1, 2))
    body = functools.partial(_mix0_pre_body, tp=tp, tiles_per_seq=tiles_per_seq, full_count=full_count)
    return pl.pallas_call(
        body,
        grid=(rows // tp,),
        in_specs=[
            pl.BlockSpec((tp, D_POOL), lambda i: (i, 0)),
            pl.BlockSpec((16, D_POOL), halo_map),
            pl.BlockSpec((tp, 512), lambda i: (i, 1)),
            pl.BlockSpec((tp, 512), lambda i: (i, 2)),
            pl.BlockSpec((tp, 512), lambda i: (i, 3)),
            pl.BlockSpec((tp, D_KV), lambda i: (i, (D_POOL + D_ATTN) // D_KV)),
            pl.BlockSpec((4, POOL_GROUP, POOL_GROUP), lambda i: (0, 0, 0)),
            pl.BlockSpec((1, D_POOL), lambda i: (0, 0)),
            pl.BlockSpec((1, LANES), lambda i: (0, 0)),
            pl.BlockSpec((1, LANES), lambda i: (0, 0)),
        ],
        out_specs=[
            pl.BlockSpec((tp, D_POOL), lambda i: (i, 0)),
            pl.BlockSpec((tp, D_ATTN), lambda i: (i, 0)),
            pl.BlockSpec((tp, D_KV), lambda i: (i, 0)),
        ],
        out_shape=[
            jax.ShapeDtypeStruct((rows, D_POOL), bf16),
            jax.ShapeDtypeStruct((rows, D_ATTN), bf16),
            jax.ShapeDtypeStruct((rows, D_KV), f32),
        ],
        scratch_shapes=[pltpu.VMEM((16 + tp, D_POOL), f32)],
        compiler_params=_cparams(("arbitrary",)),
        name="mix0_pre",
    )(u, halo_src, u, u, u, u, pool_w, pool_scale.reshape(1, D_POOL), qw, kw)


def _alibi_slopes(n):
    def pow2_slopes(m):
        start = 2.0 ** (-8.0 / m)
        return [start ** (i + 1) for i in range(m)]
    if n & (n - 1) == 0:
        s = pow2_slopes(n)
    else:
        c = 2 ** int(math.floor(math.log2(n)))
        s = pow2_slopes(c) + pow2_slopes(2 * c)[0::2][: n - c]
    return [float(v) for v in np.asarray(s, dtype=np.float32)]


_SLOPES = _alibi_slopes(N_Q_HEADS)


def _attn_body(sink_ref, q_ref, kp_ref, kc_ref, vp_ref, vc_ref, o_ref, *, tq, prompt):
    lane = lax.broadcasted_iota(i32, (WINDOW, LANES), 1)
    lo = lane < HEAD_DIM
    qi = lax.broadcasted_iota(i32, (tq, WINDOW), 0)
    kj = lax.broadcasted_iota(i32, (tq, WINDOW), 1)
    dist_p = (qi + WINDOW - kj).astype(f32)
    dist_c = (qi - kj).astype(f32)
    ok_p = (qi + WINDOW - kj) < WINDOW
    if prompt:
        ok_p = ok_p & (pl.program_id(1) > 0)
    neg_p = jnp.where(ok_p, 0.0, NEG)
    neg_c = jnp.where(qi - kj >= 0, 0.0, NEG)
    qlane = lax.broadcasted_iota(i32, (tq, LANES), 1) < HEAD_DIM

    for j in range(N_KV_HEADS):
        pair = slice((j // 2) * LANES, (j // 2 + 1) * LANES)
        own_lo = (j % 2 == 0)

        def dup(kpair):
            sw = pltpu.roll(kpair, HEAD_DIM, 1)
            return jnp.where(lo, kpair, sw) if own_lo else jnp.where(lo, sw, kpair)

        def halves(vpair):
            own = jnp.where(lo, vpair, 0.0) if own_lo else jnp.where(lo, 0.0, vpair)
            sw = pltpu.roll(own, HEAD_DIM, 1)
            return (own, sw) if own_lo else (sw, own)

        kk_p = dup(kp_ref[:, pair]).astype(bf16)
        kk_c = dup(kc_ref[:, pair]).astype(bf16)
        v_p = [h.astype(bf16) for h in halves(vp_ref[:, pair])]
        v_c = [h.astype(bf16) for h in halves(vc_ref[:, pair])]

        heads = [(hf, c) for hf in (0, 1) for c in range(3)]
        q_rows = []
        for hf, c in heads:
            ch = slice((3 * j + c) * LANES, (3 * j + c + 1) * LANES)
            qc = q_ref[:, ch].astype(f32)
            q_rows.append(jnp.where(qlane, qc, 0.0) if hf == 0 else jnp.where(qlane, 0.0, qc))
        q_st = jnp.concatenate(q_rows, axis=0).astype(bf16)
        s_p = _nt_dot(q_st, kk_p)
        s_c = _nt_dot(q_st, kk_c)
        p_p, p_c, inv = [], [], []
        for r, (hf, c) in enumerate(heads):
            head = 6 * j + 2 * c + hf
            rows = slice(r * tq, (r + 1) * tq)
            sp = s_p[rows] - _SLOPES[head] * dist_p + neg_p
            sc = s_c[rows] - _SLOPES[head] * dist_c + neg_c
            sink = sink_ref[head]
            m = jnp.maximum(jnp.maximum(jnp.max(sp, axis=-1, keepdims=True),
                                        jnp.max(sc, axis=-1, keepdims=True)), sink)
            ep = jnp.exp(sp - m)
            ec = jnp.exp(sc - m)
            den = jnp.sum(ep, axis=-1, keepdims=True) + jnp.sum(ec, axis=-1, keepdims=True) + jnp.exp(sink - m)
            p_p.append(ep)
            p_c.append(ec)
            inv.append(1.0 / den)
        for c in range(3):
            out = None
            for hf in (0, 1):
                r = hf * 3 + c
                o = (jnp.dot(p_p[r].astype(bf16), v_p[hf], preferred_element_type=f32)
                     + jnp.dot(p_c[r].astype(bf16), v_c[hf], preferred_element_type=f32)) * inv[r]
                out = o if out is None else out + o
            ch = slice((3 * j + c) * LANES, (3 * j + c + 1) * LANES)
            o_ref[:, ch] = out.astype(o_ref.dtype)


def _attn_call(sinks, q, k_arr, v_arr, specs, grid, rows, tq, prompt):
    body = functools.partial(_attn_body, tq=tq, prompt=prompt)
    q_spec, kp_spec, kc_spec, vp_spec, vc_spec, o_spec = specs
    return pl.pallas_call(
        body,
        grid=grid,
        in_specs=[pl.BlockSpec(memory_space=pltpu.SMEM), q_spec, kp_spec, kc_spec, vp_spec, vc_spec],
        out_specs=o_spec,
        out_shape=jax.ShapeDtypeStruct((rows, D_ATTN), bf16),
        compiler_params=_cparams(("arbitrary",) * len(grid)),
        name="attn",
    )(sinks, q, k_arr, k_arr, v_arr, v_arr)


def _hgrn_chunk(q, kk, v, logf, st, *, L, C):
    row = lax.broadcasted_iota(i32, (L, L), 0)
    col = lax.broadcasted_iota(i32, (L, L), 1)
    tri = (row >= col).astype(bf16)
    h1 = logf.astype(bf16)
    r1 = logf - h1.astype(f32)
    h2 = r1.astype(bf16)
    h3 = (r1 - h2.astype(f32)).astype(bf16)
    b = (jnp.dot(tri, h1, preferred_element_type=f32) + jnp.dot(tri, h2, preferred_element_type=f32)
         + jnp.dot(tri, h3, preferred_element_type=f32))
    vb = v.astype(bf16)
    o_inter = _nt_dot((q * jnp.exp(b)).astype(bf16), st.astype(bf16))
    rowid = lax.broadcasted_iota(i32, (L, 1), 0)
    tloc = lax.broadcasted_iota(i32, (C, 1), 0)
    outs = []
    for i in range(L // C):
        lo = i * C
        bi, qi, ki, vi = b[lo:lo + C], q[lo:lo + C], kk[lo:lo + C], v[lo:lo + C]
        if i > 0:
            r = b[lo - 1:lo]
            qt = (qi * jnp.exp(bi - r)).astype(bf16)
            kt = jnp.where(rowid < lo, kk * jnp.exp(jnp.minimum(r - b, 0.0)), 0.0).astype(bf16)
            a_off = _nt_dot(qt, kt)
            oi = jnp.dot(a_off.astype(bf16), vb, preferred_element_type=f32)
        else:
            oi = jnp.zeros((C, LANES), f32)
        for s in range(C):
            e = jnp.exp(jnp.minimum(bi - bi[s:s + 1], 0.0))
            a = jnp.sum(qi * ki[s:s + 1] * e, axis=-1, keepdims=True)
            oi = oi + jnp.where(tloc >= s, a, 0.0) * vi[s:s + 1]
        outs.append(oi)
    o = (outs[0] if len(outs) == 1 else jnp.concatenate(outs, axis=0)) + o_inter
    bl = b[L - 1:L]
    kt2 = (kk * jnp.exp(bl - b)).astype(bf16)
    st_new = st * jnp.exp(bl) + _tn_dot(vb, kt2)
    return o, st_new


def _hgrn_body(*refs, L, C, hpb, n_valid, has_state):
    if has_state:
        q_ref, f_ref, i_ref, g_ref, lb_ref, on_ref, s0_ref, og_ref, so_ref, st_ref = refs
    else:
        q_ref, f_ref, i_ref, g_ref, lb_ref, on_ref, og_ref, so_ref, st_ref = refs
    c = pl.program_id(2)
    nc = pl.num_programs(2)

    @pl.when(c == 0)
    def _():
        for hh in range(hpb):
            st_ref[hh] = s0_ref[hh].T if has_state else jnp.zeros((C_DK, C_DK), f32)

    valid = lax.broadcasted_iota(i32, (L, 1), 0) < n_valid
    for hh in range(hpb):
        sl = slice(hh * LANES, (hh + 1) * LANES)
        lb = lb_ref[:, sl]
        fp = f_ref[:, sl]
        q = _silu(q_ref[:, sl])
        log_sig = jnp.minimum(fp, 0.0) - jnp.log1p(jnp.exp(-jnp.abs(fp)))
        a1 = jnp.log(lb)
        a2 = jnp.log1p(-lb) + log_sig
        logf = jnp.maximum(a1, a2) + jnp.log1p(jnp.exp(-jnp.abs(a1 - a2)))
        kk = (1.0 - lb) * jax.nn.sigmoid(-fp)
        if n_valid < L:
            logf = jnp.where(valid, logf, 0.0)
            kk = jnp.where(valid, kk, 0.0)
        o, st_new = _hgrn_chunk(q, kk, i_ref[:, sl], logf, st_ref[hh], L=L, C=C)
        st_ref[hh] = st_new
        o = o * lax.rsqrt(jnp.mean(o * o, axis=-1, keepdims=True) + EPS) * on_ref[...]
        og_ref[:, sl] = (o * _silu(g_ref[:, sl])).astype(og_ref.dtype)

    @pl.when(c == nc - 1)
    def _():
        for hh in range(hpb):
            so_ref[hh] = st_ref[hh].T


def _hgrn(u, lb, o_norm, s0, *, nb, nc, L, C, hpb, n_valid, rows):
    nh = C_HEADS // hpb
    w = hpb * LANES
    has_state = s0 is not None
    body = functools.partial(_hgrn_body, L=L, C=C, hpb=hpb, n_valid=n_valid, has_state=has_state)

    def uspec(part):
        return pl.BlockSpec((L, w), lambda b, h, c: (b * nc + c, part * nh + h))

    in_specs = [uspec(0), uspec(1), uspec(2), uspec(3),
                pl.BlockSpec((1, w), lambda b, h, c: (0, h)),
                pl.BlockSpec((1, LANES), lambda b, h, c: (0, 0))]
    args = [u, u, u, u, lb.reshape(1, D_C), o_norm.reshape(1, LANES)]
    if has_state:
        in_specs.append(pl.BlockSpec((None, hpb, C_DK, C_DK), lambda b, h, c: (b, h, 0, 0)))
        args.append(s0)
    return pl.pallas_call(
        body,
        grid=(nb, nh, nc),
        in_specs=in_specs,
        out_specs=[pl.BlockSpec((L, w), lambda b, h, c: (b * nc + c, h)),
                   pl.BlockSpec((None, hpb, C_DK, C_DK), lambda b, h, c: (b, h, 0, 0))],
        out_shape=[jax.ShapeDtypeStruct((rows, D_C), bf16),
                   jax.ShapeDtypeStruct((nb, C_HEADS, C_DK, C_DK), f32)],
        scratch_shapes=[pltpu.VMEM((hpb, C_DK, C_DK), f32)],
        compiler_params=_cparams(("arbitrary", "arbitrary", "arbitrary")),
        name="hgrn",
    )(*args)


def _moe_pre_body(x_ref, g_ref, scp_ref, shp_ref, scs_ref, shs_ref, wrt_ref, rb_ref,
                  h3_ref, idx_ref, wts_ref, rank_ref, cnt_ref, h_buf, carry):
    i = pl.program_id(0)

    @pl.when(i == 0)
    def _():
        carry[...] = jnp.zeros_like(carry)

    y = _norm_rows(x_ref[...], g_ref[...])

    @pl.when(i < NPT)
    def _():
        h_buf[...] = y * (1.0 + scp_ref[...]) + shp_ref[...]

    @pl.when(i >= NPT)
    def _():
        h_buf[:NS] = y[:NS] * (1.0 + scs_ref[...]) + shs_ref[...]
        h_buf[NS:] = jnp.zeros((TM - NS, D_MODEL), f32)

    h = h_buf[...]
    for c in range(DCH):
        h3_ref[:, c, :] = h[:, c * LANES:(c + 1) * LANES]

    logits = _nt_dot(wrt_ref[...], h, precision=lax.Precision.HIGHEST)
    scores = jax.nn.sigmoid(logits)
    sel = scores + rb_ref[...]
    best = jnp.zeros((1, TM), i32)
    best_v = None
    for g in range(N_EXPERTS // EXP_PER_GROUP):
        r = [sel[EXP_PER_GROUP * g + t:EXP_PER_GROUP * g + t + 1] for t in range(EXP_PER_GROUP)]
        top2 = None
        for a in range(EXP_PER_GROUP):
            for bb in range(a + 1, EXP_PER_GROUP):
                s2 = r[a] + r[bb]
                top2 = s2 if top2 is None else jnp.maximum(top2, s2)
        if g == 0:
            best_v = top2
        else:
            upd = top2 > best_v
            best = jnp.where(upd, g, best)
            best_v = jnp.where(upd, top2, best_v)
    eidx = lax.broadcasted_iota(i32, (N_EXPERTS, TM), 0)
    masked = jnp.where(jnp.right_shift(eidx, 2) == best, sel, NEG)
    m1 = jnp.max(masked, axis=0, keepdims=True)
    i1 = jnp.min(jnp.where(masked == m1, eidx, N_EXPERTS), axis=0, keepdims=True)
    masked2 = jnp.where(eidx == i1, -jnp.inf, masked)
    m2 = jnp.max(masked2, axis=0, keepdims=True)
    i2 = jnp.min(jnp.where(masked2 == m2, eidx, N_EXPERTS), axis=0, keepdims=True)
    w1 = jnp.sum(jnp.where(eidx == i1, scores, 0.0), axis=0, keepdims=True)
    w2 = jnp.sum(jnp.where(eidx == i2, scores, 0.0), axis=0, keepdims=True)
    den = w1 + w2
    idx_ref[0:1, :] = i1
    idx_ref[1:2, :] = i2
    wts_ref[0:1, :] = w1 / den
    wts_ref[1:2, :] = w2 / den

    colg = i * TM + lax.broadcasted_iota(i32, (1, TM), 1)
    oh = jnp.where(((eidx == i1) | (eidx == i2)) & (colg < N_REAL), 1.0, 0.0)
    rr = lax.broadcasted_iota(i32, (TM, TM), 0)
    cc = lax.broadcasted_iota(i32, (TM, TM), 1)
    before = (rr < cc).astype(bf16)
    rank = jnp.dot(oh.astype(bf16), before, preferred_element_type=f32) + carry[:, 0:1]
    rank_ref[0:1, :] = jnp.sum(jnp.where(eidx == i1, rank, 0.0), axis=0, keepdims=True).astype(i32)
    rank_ref[1:2, :] = jnp.sum(jnp.where(eidx == i2, rank, 0.0), axis=0, keepdims=True).astype(i32)
    carry[...] = carry[...] + jnp.sum(oh, axis=1, keepdims=True)
    cnt_ref[...] = carry[...]


def _moe_pre(x_all, g, modp, mods, layer, router_wt, router_b):
    scp, scs = _mod_specs(layer, 4)
    shp, shs = _mod_specs(layer, 3)
    return pl.pallas_call(
        _moe_pre_body,
        grid=(N_TILES,),
        in_specs=[pl.BlockSpec((TM, D_MODEL), lambda i: (i, 0)),
                  pl.BlockSpec((1, D_MODEL), lambda i: (0, 0)),
                  scp, shp, scs, shs,
                  pl.BlockSpec((N_EXPERTS, D_MODEL), lambda i: (0, 0)),
                  pl.BlockSpec((N_EXPERTS, 1), lambda i: (0, 0))],
        out_specs=[pl.BlockSpec((TM, DCH, LANES), lambda i: (i, 0, 0)),
                   pl.BlockSpec((2, TM), lambda i: (0, i)),
                   pl.BlockSpec((2, TM), lambda i: (0, i)),
                   pl.BlockSpec((2, TM), lambda i: (0, i)),
                   pl.BlockSpec((N_EXPERTS, LANES), lambda i: (0, 0))],
        out_shape=[jax.ShapeDtypeStruct((M_ALL, DCH, LANES), f32),
                   jax.ShapeDtypeStruct((2, M_ALL), i32),
                   jax.ShapeDtypeStruct((2, M_ALL), f32),
                   jax.ShapeDtypeStruct((2, M_ALL), i32),
                   jax.ShapeDtypeStruct((N_EXPERTS, LANES), f32)],
        scratch_shapes=[pltpu.VMEM((TM, D_MODEL), f32), pltpu.VMEM((N_EXPERTS, LANES), f32)],
        compiler_params=_cparams(("arbitrary",)),
        name="moe_pre",
    )(x_all, g.reshape(1, D_MODEL), modp, modp, mods, mods, router_wt, router_b.reshape(N_EXPERTS, 1))


def _dispatch_body(pos_ref, fill_lo_ref, fill_hi_ref, nu_ref, h3_hbm, hs_hbm, zero_buf, sem, tile_sem):
    zero_buf[...] = jnp.zeros_like(zero_buf)
    zero_row = zero_buf.at[pl.ds(0, 1)]

    def row_copy(src, dst_row):
        return pltpu.make_async_copy(src, hs_hbm.at[pl.ds(dst_row, 1)], sem)

    def tile_copy(t):
        return pltpu.make_async_copy(zero_buf, hs_hbm.at[pl.ds(t * TME, TME)], tile_sem)

    def wait_one():
        row_copy(zero_row, 0).wait()

    def tail_start(t, carry):
        tile_copy(t).start()
        return carry

    def tail_wait(t, carry):
        tile_copy(t).wait()
        return carry

    lax.fori_loop(nu_ref[0], NT_E, tail_start, 0)

    def issue(n, carry):
        src = h3_hbm.at[pl.ds(n, 1)]
        row_copy(src, pos_ref[n]).start()
        row_copy(src, pos_ref[M_ALL + n]).start()

        @pl.when(n >= DMA_WINDOW)
        def _():
            wait_one()
            wait_one()
        return carry

    lax.fori_loop(0, N_REAL, issue, 0)

    def drain(n, carry):
        wait_one()
        return carry

    lax.fori_loop(0, 2 * DMA_WINDOW, drain, 0)

    def fill_expert(e, carry):
        def fill(r, c2):
            row_copy(zero_row, r).start()
            return c2
        lax.fori_loop(fill_lo_ref[e], fill_hi_ref[e], fill, 0)
        lax.fori_loop(fill_lo_ref[e], fill_hi_ref[e], drain, 0)
        return carry

    lax.fori_loop(0, N_EXPERTS, fill_expert, 0)
    lax.fori_loop(nu_ref[0], NT_E, tail_wait, 0)


def _dispatch(pos_flat, fill_lo, fill_hi, n_used, h3):
    return pl.pallas_call(
        _dispatch_body,
        grid_spec=pltpu.PrefetchScalarGridSpec(
            num_scalar_prefetch=4,
            grid=(1,),
            in_specs=[pl.BlockSpec(memory_space=pl.ANY)],
            out_specs=pl.BlockSpec(memory_space=pl.ANY),
            scratch_shapes=[pltpu.VMEM((TME, DCH, LANES), f32), pltpu.SemaphoreType.DMA,
                            pltpu.SemaphoreType.DMA],
        ),
        out_shape=jax.ShapeDtypeStruct((R_MAX, DCH, LANES), f32),
        compiler_params=_cparams(("arbitrary",)),
        name="moe_dispatch",
    )(pos_flat, fill_lo, fill_hi, n_used, h3)


def _expert_body(te_ref, nu_ref, hs_ref, w1_ref, w3_ref, w2_ref, y_ref, w1b, w3b, w2b):
    t = pl.program_id(0)

    @pl.when(t < nu_ref[0])
    def _():
        changed = (t == 0) | (te_ref[t] != te_ref[jnp.maximum(t - 1, 0)])

        @pl.when(changed)
        def _():
            w1b[...] = w1_ref[...].astype(bf16)
            w3b[...] = w3_ref[...].astype(bf16)
            w2b[...] = w2_ref[...].astype(bf16)

        hb = jnp.concatenate([hs_ref[:, c, :] for c in range(DCH)], axis=1).astype(bf16)
        a = jnp.dot(hb, w1b[...], preferred_element_type=f32)
        b = jnp.dot(hb, w3b[...], preferred_element_type=f32)
        hid = (_silu(a) * b).astype(bf16)
        y = jnp.dot(hid, w2b[...], preferred_element_type=f32)
        for c in range(DCH):
            y_ref[:, c, :] = y[:, c * LANES:(c + 1) * LANES]

    @pl.when(t >= nu_ref[0])
    def _():
        y_ref[...] = jnp.zeros_like(y_ref)


def _experts(tile_expert, n_used, hs, w1, w3, w2, layer):
    def row_map(t, te, nu):
        return (jnp.minimum(t, nu[0] - 1), 0, 0)

    def out_map(t, te, nu):
        return (t, 0, 0)

    def w_map(t, te, nu):
        return (layer, te[t], 0, 0)

    return pl.pallas_call(
        _expert_body,
        grid_spec=pltpu.PrefetchScalarGridSpec(
            num_scalar_prefetch=2,
            grid=(NT_E,),
            in_specs=[pl.BlockSpec((TME, DCH, LANES), row_map),
                      pl.BlockSpec((None, None, D_MODEL, D_FF), w_map),
                      pl.BlockSpec((None, None, D_MODEL, D_FF), w_map),
                      pl.BlockSpec((None, None, D_FF, D_MODEL), w_map)],
            out_specs=pl.BlockSpec((TME, DCH, LANES), out_map),
            scratch_shapes=[pltpu.VMEM((D_MODEL, D_FF), bf16), pltpu.VMEM((D_MODEL, D_FF), bf16),
                            pltpu.VMEM((D_FF, D_MODEL), bf16)],
        ),
        out_shape=jax.ShapeDtypeStruct((R_MAX, DCH, LANES), f32),
        compiler_params=_cparams(("arbitrary",)),
        name="moe_experts",
    )(tile_expert, n_used, hs, w1, w3, w2)


def _combine_body(pos_ref, x_ref, w_ref, gp_ref, gs_ref, y_hbm, o_ref, buf0, buf1, sem):
    i = pl.program_id(0)
    base = i * TMC
    n_real_tiles = N_REAL // TMC

    def copies(r):
        return (pltpu.make_async_copy(y_hbm.at[pl.ds(pos_ref[base + r], 1)], buf0.at[pl.ds(r, 1)], sem),
                pltpu.make_async_copy(y_hbm.at[pl.ds(pos_ref[M_ALL + base + r], 1)], buf1.at[pl.ds(r, 1)], sem))

    def finish(gate):
        def start(r, c):
            for cp in copies(r):
                cp.start()
            return c

        def wait(r, c):
            for cp in copies(r):
                cp.wait()
            return c

        lax.fori_loop(0, TMC, start, 0)
        lax.fori_loop(0, TMC, wait, 0)
        w0 = w_ref[:, 0:1]
        w1 = w_ref[:, 1:2]
        for c in range(DCH):
            sl = slice(c * LANES, (c + 1) * LANES)
            m = w0 * buf0[:, c, :] + w1 * buf1[:, c, :]
            o_ref[:, sl] = x_ref[:, sl] + gate[:, sl] * m

    @pl.when(i < NP // TMC)
    def _():
        finish(gp_ref[...])

    @pl.when((i >= NP // TMC) & (i < n_real_tiles))
    def _():
        finish(gs_ref[...])

    @pl.when(i >= n_real_tiles)
    def _():
        o_ref[...] = jnp.zeros_like(o_ref)


def _combine(pos_flat, x_all, wts_t, modp, mods, layer, y):
    tiles_per_seq = SEQ // TMC
    return pl.pallas_call(
        _combine_body,
        grid_spec=pltpu.PrefetchScalarGridSpec(
            num_scalar_prefetch=1,
            grid=(M_ALL // TMC,),
            in_specs=[pl.BlockSpec((TMC, D_MODEL), lambda i, p: (i, 0)),
                      pl.BlockSpec((TMC, 2), lambda i, p: (i, 0)),
                      pl.BlockSpec((None, None, None, 1, D_MODEL),
                                   lambda i, p: (layer, 5, jnp.minimum(i // tiles_per_seq, BATCH - 1), 0, 0)),
                      pl.BlockSpec((None, None, NS, D_MODEL), lambda i, p: (layer, 5, 0, 0)),
                      pl.BlockSpec(memory_space=pl.ANY)],
            out_specs=pl.BlockSpec((TMC, D_MODEL), lambda i, p: (i, 0)),
            scratch_shapes=[pltpu.VMEM((TMC, DCH, LANES), f32), pltpu.VMEM((TMC, DCH, LANES), f32),
                            pltpu.SemaphoreType.DMA],
        ),
        out_shape=jax.ShapeDtypeStruct((M_ALL, D_MODEL), f32),
        compiler_params=_cparams(("arbitrary",)),
        name="moe_combine",
    )(pos_flat, x_all, wts_t, modp, mods, y)


def _moe(x_all, g, modp, mods, layer, router_wt, router_b, w1, w3, w2):
    h3, idx, wts, rank, cnt = _moe_pre(x_all, g, modp, mods, layer, router_wt, router_b)
    counts = cnt[:, 0].astype(i32)
    padded = ((counts + TME - 1) // TME) * TME
    ends = jnp.cumsum(padded)
    offs = ends - padded
    real = (jnp.arange(M_ALL) < N_REAL)[None, :]
    pos = jnp.where(real, offs[idx] + rank, 0).astype(i32)
    pos_flat = pos.reshape(2 * M_ALL)
    n_used = (ends[-1] // TME).astype(i32).reshape(1)
    tile_start = jnp.minimum(jnp.arange(NT_E, dtype=i32), n_used[0] - 1) * TME
    tile_expert = jnp.sum((tile_start[:, None] >= ends[None, :]).astype(i32), axis=1).astype(i32)
    hs = _dispatch(pos_flat, (offs + counts).astype(i32), ends.astype(i32), n_used, h3)
    y = _experts(tile_expert, n_used, hs, w1, w3, w2, layer)
    return _combine(pos_flat, x_all, wts.T, modp, mods, layer, y)


def _sample_rows(a, width):
    return a.reshape(DEC_BATCH, SP, width)[:, :DEC_SEQ].reshape(NS, width)


def kernel(x_prompt, x_sample, c_prompt, c_sample, state_pool, cache_k, cache_v, state_hgrn, norm_g, ada_w, ada_b, w_in_ab, pool_w, pool_scale, q_norm, k_norm, attn_sinks, w_out_ab, w_in_c, lb_param, o_norm, w_out_c, router_w, router_b, moe_w1, moe_w3, moe_w2):
    n_c = BATCH + DEC_BATCH
    c_rows = ((n_c + 7) // 8) * 8
    c_all = jnp.concatenate([c_prompt, c_sample, jnp.zeros((c_rows - n_c, D_MODEL), f32)], axis=0)
    mod = _ada_mod(c_all, ada_w, ada_b).reshape(DEPTH, c_rows, 6, D_MODEL)
    modp = mod[:, :BATCH].transpose(0, 2, 1, 3).reshape(DEPTH, 6, BATCH, 1, D_MODEL)
    mods = jnp.repeat(mod[:, BATCH:n_c].transpose(0, 2, 1, 3), DEC_SEQ, axis=2)

    x_all = jnp.concatenate([x_prompt.reshape(NP, D_MODEL), x_sample.reshape(NS, D_MODEL),
                             jnp.zeros((M_ALL - N_REAL, D_MODEL), f32)], axis=0)
    router_wt = router_w.T

    h = _normmod(x_all, norm_g[0, 0], modp, mods, 0, 0, 1)
    u = _proj(h, w_in_ab[0], 512)
    tp = 256
    pool_p, qn_p, kn_p = _mix0_pre(
        u, u, lambda i: (jnp.maximum(i * (tp // 16) - 1, 0), 0), NP, tp, SEQ // tp, False,
        pool_w[0], pool_scale[0], q_norm[0], k_norm[0])
    nb = SEQ // WINDOW
    blk = lambda b, n: (b * nb + n, 0)
    prv = lambda b, n: (jnp.maximum(b * nb + n - 1, 0), 0)
    vcol = (D_POOL + D_ATTN + D_KV) // D_KV
    attn_p = _attn_call(
        attn_sinks[0], qn_p, kn_p, u,
        (pl.BlockSpec((WINDOW, D_ATTN), blk), pl.BlockSpec((WINDOW, D_KV), prv), pl.BlockSpec((WINDOW, D_KV), blk),
         pl.BlockSpec((WINDOW, D_KV), lambda b, n: (jnp.maximum(b * nb + n - 1, 0), vcol)),
         pl.BlockSpec((WINDOW, D_KV), lambda b, n: (b * nb + n, vcol)),
         pl.BlockSpec((WINDOW, D_ATTN), blk)),
        (BATCH, nb), NP, WINDOW, True)
    u_s = u[NP:N_REAL].reshape(DEC_BATCH, DEC_SEQ, D_IN_AB)
    u_s16 = jnp.pad(u_s, ((0, 0), (0, SP - DEC_SEQ), (0, 0))).reshape(NSP, D_IN_AB)
    halo_s = jnp.pad(state_pool[0], ((0, 0), (1, 0), (0, 0))).reshape(DEC_BATCH * 16, D_POOL)
    pool_s, qn_s, kn_s = _mix0_pre(
        u_s16, halo_s, lambda i: (i, 0), NSP, SP, 1, True,
        pool_w[0], pool_scale[0], q_norm[0], k_norm[0])
    kn_new = kn_s.reshape(DEC_BATCH, SP, D_KV)[:, :DEC_SEQ]
    v_new = u_s[:, :, D_POOL + D_ATTN + D_KV:]
    zpad = jnp.zeros((DEC_BATCH, WINDOW - DEC_SEQ, D_KV), f32)
    k_ext = jnp.concatenate([cache_k[0].reshape(DEC_BATCH, WINDOW, D_KV), kn_new, zpad], axis=1)
    v_ext = jnp.concatenate([cache_v[0].reshape(DEC_BATCH, WINDOW, D_KV), v_new, zpad], axis=1)
    ext_p = pl.BlockSpec((None, WINDOW, D_KV), lambda b: (b, 0, 0))
    ext_c = pl.BlockSpec((None, WINDOW, D_KV), lambda b: (b, 1, 0))
    attn_s = _attn_call(
        attn_sinks[0], qn_s, k_ext, v_ext,
        (pl.BlockSpec((SP, D_ATTN), lambda b: (b, 0)), ext_p, ext_c, ext_p, ext_c,
         pl.BlockSpec((SP, D_ATTN), lambda b: (b, 0))),
        (DEC_BATCH,), NSP, SP, False)
    x_all = _oproj([pool_p, attn_p], [_sample_rows(pool_s, D_POOL), _sample_rows(attn_s, D_ATTN)],
                   w_out_ab[0], x_all, modp, mods, 0, 2)
    x_all = _moe(x_all, norm_g[0, 1], modp, mods, 0, router_wt, router_b, moe_w1, moe_w3, moe_w2)

    lb_soft = jax.nn.softmax(lb_param.astype(f32), axis=0)
    lower_bounds = jnp.cumsum(lb_soft, axis=0) - lb_soft[0:1]
    h = _normmod(x_all, norm_g[1, 0], modp, mods, 1, 0, 1)
    u1 = _proj(h, w_in_c[0], 1024)
    og_p, hg_p = _hgrn(u1, lower_bounds[1], o_norm[0], None,
                       nb=BATCH, nc=SEQ // 64, L=64, C=16, hpb=2, n_valid=64, rows=NP)
    u1_s = jnp.pad(u1[NP:N_REAL].reshape(DEC_BATCH, DEC_SEQ, 4 * D_C),
                   ((0, 0), (0, SP - DEC_SEQ), (0, 0))).reshape(NSP, 4 * D_C)
    og_s, hg_s = _hgrn(u1_s, lower_bounds[1], o_norm[0], state_hgrn[0],
                       nb=DEC_BATCH, nc=1, L=SP, C=SP, hpb=4, n_valid=DEC_SEQ, rows=NSP)
    x_all = _oproj([og_p], [_sample_rows(og_s, D_C)], w_out_c[0], x_all, modp, mods, 1, 2)
    x_all = _moe(x_all, norm_g[1, 1], modp, mods, 1, router_wt, router_b, moe_w1, moe_w3, moe_w2)

    y_prompt = x_all[:NP].reshape(BATCH, SEQ, D_MODEL)
    y_sample = x_all[NP:N_REAL].reshape(DEC_BATCH, DEC_SEQ, D_MODEL)
    u_p = u[:NP].reshape(BATCH, SEQ, D_IN_AB)
    pool_prompt = u_p[:, SEQ - POOL_STATE:, :D_POOL][None]
    pool_sample = jnp.concatenate([state_pool[0], u_s[:, :, :D_POOL]], axis=1)[:, -POOL_STATE:][None]
    kv_shape = (N_KV_HEADS, HEAD_DIM)
    k_prompt = kn_p.reshape(BATCH, SEQ, *kv_shape)[:, SEQ - WINDOW:][None]
    v_prompt = u_p[:, SEQ - WINDOW:, D_POOL + D_ATTN + D_KV:].reshape(BATCH, WINDOW, *kv_shape)[None]
    k_sample = jnp.concatenate([cache_k[0], kn_new.reshape(DEC_BATCH, DEC_SEQ, *kv_shape)], axis=1)[:, -WINDOW:][None]
    v_sample = jnp.concatenate([cache_v[0], v_new.reshape(DEC_BATCH, DEC_SEQ, *kv_shape)], axis=1)[:, -WINDOW:][None]
    return (y_prompt, y_sample, pool_prompt, pool_sample, k_prompt, k_sample, v_prompt, v_sample,
            hg_p[None], hg_s[None])
```

```python
import functools
import math

import numpy as np
import jax
import jax.numpy as jnp
from jax import lax
from jax.experimental import pallas as pl
from jax.experimental.pallas import tpu as pltpu

f32 = jnp.float32
bf16 = jnp.bfloat16
i32 = jnp.int32

D_MODEL = 2048
BATCH = 4
SEQ = 2048
DEPTH = 2
DEC_BATCH = 32
DEC_SEQ = 4
EPS = 1e-6
POOL_WINDOWS = (2, 4, 8, 16)
D_POOL = 512
POOL_GROUP = 128
POOL_STATE = 15
HEAD_DIM = 64
N_Q_HEADS = 24
N_KV_HEADS = 4
Q_PER_KV = 6
WINDOW = 128
D_ATTN = 1536
D_KV = 256
D_IN_AB = 2560
C_DK = 128
C_HEADS = 16
D_C = 2048
N_EXPERTS = 16
EXP_PER_GROUP = 4
D_FF = 512
NEG = -1e30

LANES = 128
VMEM_LIMIT = 56 * 1024 * 1024

NP = BATCH * SEQ
NS = DEC_BATCH * DEC_SEQ
N_REAL = NP + NS
TM = 512
M_ALL = ((N_REAL + TM - 1) // TM) * TM
N_TILES = M_ALL // TM
NPT = NP // TM
TPB = SEQ // TM
SP = 16
NSP = DEC_BATCH * SP

TME = 256
R_MAX = ((2 * N_REAL + N_EXPERTS * (TME - 1)) // TME + 1) * TME
NT_E = R_MAX // TME
TMC = 128
TMD = 640

FACTORED_BLOCK = 32
DECAY_CLAMP = 60.0


def _cparams(sem, vmem=VMEM_LIMIT):
    return pltpu.CompilerParams(dimension_semantics=sem, vmem_limit_bytes=vmem)


def _silu(x):
    return x * jax.nn.sigmoid(x)


def _nt_dot(a, b, **kw):
    return lax.dot_general(a, b, (((1,), (1,)), ((), ())), preferred_element_type=f32, **kw)


def _tn_dot(a, b):
    return lax.dot_general(a, b, (((0,), (0,)), ((), ())), preferred_element_type=f32)


def _ada_body(c_ref, w_ref, b_ref, o_ref):
    s = _silu(c_ref[...]).astype(bf16)
    o_ref[...] = jnp.dot(s, w_ref[...].astype(bf16), preferred_element_type=f32) + b_ref[...]


def _ada_mod(c_all, ada_w, ada_b):
    depth, d, e = ada_w.shape
    cr = c_all.shape[0]
    tn = 1024
    return pl.pallas_call(
        _ada_body,
        grid=(depth, e // tn),
        in_specs=[
            pl.BlockSpec((cr, d), lambda l, j: (0, 0)),
            pl.BlockSpec((None, d, tn), lambda l, j: (l, 0, j)),
            pl.BlockSpec((None, 1, tn), lambda l, j: (l, 0, j)),
        ],
        out_specs=pl.BlockSpec((None, cr, tn), lambda l, j: (l, 0, j)),
        out_shape=jax.ShapeDtypeStruct((depth, cr, e), f32),
        compiler_params=_cparams(("arbitrary", "arbitrary")),
        name="ada_mod",
    )(c_all, ada_w, ada_b.reshape(depth, 1, e))


def _mod_specs(layer, k):
    p = pl.BlockSpec((None, None, None, 1, D_MODEL),
                     lambda i: (layer, k, jnp.minimum(i // TPB, BATCH - 1), 0, 0))
    s = pl.BlockSpec((None, None, NS, D_MODEL), lambda i: (layer, k, 0, 0))
    return p, s


def _norm_rows(x, g):
    return x * lax.rsqrt(jnp.mean(x * x, axis=-1, keepdims=True) + EPS) * g


def _normmod_body(x_ref, g_ref, scp_ref, shp_ref, scs_ref, shs_ref, h_ref):
    i = pl.program_id(0)
    y = _norm_rows(x_ref[...], g_ref[...])

    @pl.when(i < NPT)
    def _():
        h_ref[...] = (y * (1.0 + scp_ref[...]) + shp_ref[...]).astype(h_ref.dtype)

    @pl.when(i >= NPT)
    def _():
        h_ref[:NS] = (y[:NS] * (1.0 + scs_ref[...]) + shs_ref[...]).astype(h_ref.dtype)
        h_ref[NS:] = jnp.zeros((TM - NS, D_MODEL), h_ref.dtype)


def _normmod(x_all, g, modp, mods, layer, k_shift, k_scale):
    scp, scs = _mod_specs(layer, k_scale)
    shp, shs = _mod_specs(layer, k_shift)
    return pl.pallas_call(
        _normmod_body,
        grid=(N_TILES,),
        in_specs=[pl.BlockSpec((TM, D_MODEL), lambda i: (i, 0)),
                  pl.BlockSpec((1, D_MODEL), lambda i: (0, 0)),
                  scp, shp, scs, shs],
        out_specs=pl.BlockSpec((TM, D_MODEL), lambda i: (i, 0)),
        out_shape=jax.ShapeDtypeStruct((M_ALL, D_MODEL), bf16),
        compiler_params=_cparams(("arbitrary",)),
        name="normmod",
    )(x_all, g.reshape(1, D_MODEL), modp, modp, mods, mods)


def _proj_body(h_ref, w_ref, o_ref, wb_ref):
    @pl.when(pl.program_id(1) == 0)
    def _():
        wb_ref[...] = w_ref[...].astype(bf16)

    o_ref[...] = jnp.dot(h_ref[...], wb_ref[...], preferred_element_type=f32)


def _proj(h_all, w, tn):
    k, e = w.shape
    return pl.pallas_call(
        _proj_body,
        grid=(e // tn, N_TILES),
        in_specs=[pl.BlockSpec((TM, k), lambda j, i: (i, 0)),
                  pl.BlockSpec((k, tn), lambda j, i: (0, j))],
        out_specs=pl.BlockSpec((TM, tn), lambda j, i: (i, j)),
        out_shape=jax.ShapeDtypeStruct((M_ALL, e), f32),
        scratch_shapes=[pltpu.VMEM((k, tn), bf16)],
        compiler_params=_cparams(("arbitrary", "arbitrary")),
        name="proj",
    )(h_all, w)


def _oproj_body(*refs, n_lhs):
    ap_refs = refs[:n_lhs]
    as_refs = refs[n_lhs:2 * n_lhs]
    w_ref, x_ref, gp_ref, gs_ref, o_ref, wb_ref = refs[2 * n_lhs:]
    i = pl.program_id(1)

    @pl.when(i == 0)
    def _():
        wb_ref[...] = w_ref[...].astype(bf16)

    def matmul(a_refs, rows):
        off, acc = 0, None
        for a in a_refs:
            kk = a.shape[1]
            part = jnp.dot(a[:rows], wb_ref[off:off + kk, :], preferred_element_type=f32)
            acc = part if acc is None else acc + part
            off += kk
        return acc

    @pl.when(i < NPT)
    def _():
        o_ref[...] = x_ref[...] + gp_ref[...] * matmul(ap_refs, TM)

    @pl.when(i >= NPT)
    def _():
        o_ref[:NS] = x_ref[:NS] + gs_ref[...] * matmul(as_refs, NS)
        o_ref[NS:] = jnp.zeros((TM - NS, o_ref.shape[1]), f32)


def _oproj(a_prompt, a_sample, w, x_all, modp, mods, layer, k_gate):
    n_lhs = len(a_prompt)
    kdim, e = w.shape
    tn = 512
    in_specs = []
    for a in a_prompt:
        in_specs.append(pl.BlockSpec((TM, a.shape[1]), lambda j, i: (jnp.minimum(i, NPT - 1), 0)))
    for a in a_sample:
        in_specs.append(pl.BlockSpec((NS, a.shape[1]), lambda j, i: (0, 0)))
    in_specs += [
        pl.BlockSpec((kdim, tn), lambda j, i: (0, j)),
        pl.BlockSpec((TM, tn), lambda j, i: (i, j)),
        pl.BlockSpec((None, None, None, 1, tn),
                     lambda j, i: (layer, k_gate, jnp.minimum(i // TPB, BATCH - 1), 0, j)),
        pl.BlockSpec((None, None, NS, tn), lambda j, i: (layer, k_gate, 0, j)),
    ]
    return pl.pallas_call(
        functools.partial(_oproj_body, n_lhs=n_lhs),
        grid=(e // tn, N_TILES),
        in_specs=in_specs,
        out_specs=pl.BlockSpec((TM, tn), lambda j, i: (i, j)),
        out_shape=jax.ShapeDtypeStruct((M_ALL, e), f32),
        scratch_shapes=[pltpu.VMEM((kdim, tn), bf16)],
        compiler_params=_cparams(("arbitrary", "arbitrary")),
        name="oproj",
    )(*a_prompt, *a_sample, w, x_all, modp, mods)


def _head_norm(x, w_row, scale):
    lane = lax.broadcasted_iota(i32, x.shape, 1)
    lo = lane < HEAD_DIM
    ss = x * x
    m_lo = jnp.sum(jnp.where(lo, ss, 0.0), axis=-1, keepdims=True)
    m_hi = jnp.sum(jnp.where(lo, 0.0, ss), axis=-1, keepdims=True)
    r = jnp.where(lo, lax.rsqrt(m_lo * (1.0 / HEAD_DIM) + EPS), lax.rsqrt(m_hi * (1.0 / HEAD_DIM) + EPS))
    return x * r * (w_row * scale)


def _mix0_pre_body(cur_ref, halo_ref, q0_ref, q1_ref, q2_ref, k_ref, pw_ref, ps_ref, qw_ref, kw_ref,
                   pool_ref, qn_ref, kn_ref, ext_ref, *, tp, tiles_per_seq, full_count):
    i = pl.program_id(0)
    halo = halo_ref[...]
    if not full_count:
        halo = jnp.where(i % tiles_per_seq == 0, 0.0, halo)
    ext_ref[0:16, :] = halo
    ext_ref[16:16 + tp, :] = cur_ref[...]
    t_pos = (i % tiles_per_seq) * tp + lax.broadcasted_iota(i32, (tp, 1), 0)
    for g, w in enumerate(POOL_WINDOWS):
        ch = slice(g * POOL_GROUP, (g + 1) * POOL_GROUP)
        acc = ext_ref[16:16 + tp, ch]
        for dlt in range(1, w):
            acc = acc + ext_ref[16 - dlt:16 - dlt + tp, ch]
        if full_count:
            mean = acc * (1.0 / w)
        else:
            cnt = jnp.minimum(t_pos + 1, w).astype(f32)
            mean = acc / cnt
        d = mean - ext_ref[16:16 + tp, ch]
        y = jnp.dot(d.astype(bf16), pw_ref[g].astype(bf16), preferred_element_type=f32)
        pool_ref[:, ch] = (y * ps_ref[:, ch]).astype(pool_ref.dtype)
    for c3, q_ref in enumerate((q0_ref, q1_ref, q2_ref)):
        for c in range(4):
            sl = slice(c * LANES, (c + 1) * LANES)
            dst = slice((c3 * 4 + c) * LANES, (c3 * 4 + c + 1) * LANES)
            qn_ref[:, dst] = _head_norm(q_ref[:, sl], qw_ref[...], HEAD_DIM ** -0.5).astype(qn_ref.dtype)
    for c in range(2):
        sl = slice(c * LANES, (c + 1) * LANES)
        kn_ref[:, sl] = _head_norm(k_ref[:, sl], kw_ref[...], 1.0)


def _mix0_pre(u, halo_src, halo_map, rows, tp, tiles_per_seq, full_count, pool_w, pool_scale, q_norm, k_norm):
    qw = jnp.tile(q_norm.reshape(1, HEAD_DIM), (1, 2))
    kw = jnp.tile(k_norm.reshape(1, HEAD_DIM), (1, 2))
    body = functools.partial(_mix0_pre_body, tp=tp, tiles_per_seq=tiles_per_seq, full_count=full_count)
    return pl.pallas_call(
        body,
        grid=(rows // tp,),
        in_specs=[
            pl.BlockSpec((tp, D_POOL), lambda i: (i, 0)),
            pl.BlockSpec((16, D_POOL), halo_map),
            pl.BlockSpec((tp, 512), lambda i: (i, 1)),
            pl.BlockSpec((tp, 512), lambda i: (i, 2)),
            pl.BlockSpec((tp, 512), lambda i: (i, 3)),
            pl.BlockSpec((tp, D_KV), lambda i: (i, (D_POOL + D_ATTN) // D_KV)),
            pl.BlockSpec((4, POOL_GROUP, POOL_GROUP), lambda i: (0, 0, 0)),
            pl.BlockSpec((1, D_POOL), lambda i: (0, 0)),
            pl.BlockSpec((1, LANES), lambda i: (0, 0)),
            pl.BlockSpec((1, LANES), lambda i: (0, 0)),
        ],
        out_specs=[
            pl.BlockSpec((tp, D_POOL), lambda i: (i, 0)),
            pl.BlockSpec((tp, D_ATTN), lambda i: (i, 0)),
            pl.BlockSpec((tp, D_KV), lambda i: (i, 0)),
        ],
        out_shape=[
            jax.ShapeDtypeStruct((rows, D_POOL), bf16),
            jax.ShapeDtypeStruct((rows, D_ATTN), bf16),
            jax.ShapeDtypeStruct((rows, D_KV), f32),
        ],
        scratch_shapes=[pltpu.VMEM((16 + tp, D_POOL), f32)],
        compiler_params=_cparams(("arbitrary",)),
        name="mix0_pre",
    )(u, halo_src, u, u, u, u, pool_w, pool_scale.reshape(1, D_POOL), qw, kw)


def _alibi_slopes(n):
    def pow2_slopes(m):
        start = 2.0 ** (-8.0 / m)
        return [start ** (i + 1) for i in range(m)]
    if n & (n - 1) == 0:
        s = pow2_slopes(n)
    else:
        c = 2 ** int(math.floor(math.log2(n)))
        s = pow2_slopes(c) + pow2_slopes(2 * c)[0::2][: n - c]
    return [float(v) for v in np.asarray(s, dtype=np.float32)]


_SLOPES = _alibi_slopes(N_Q_HEADS)


def _attn_body(sink_ref, q_ref, kp_ref, kc_ref, vp_ref, vc_ref, o_ref, *, tq, prompt):
    lane = lax.broadcasted_iota(i32, (WINDOW, LANES), 1)
    lo = lane < HEAD_DIM
    qi = lax.broadcasted_iota(i32, (tq, WINDOW), 0)
    kj = lax.broadcasted_iota(i32, (tq, WINDOW), 1)
    dist_p = (qi + WINDOW - kj).astype(f32)
    dist_c = (qi - kj).astype(f32)
    ok_p = (qi + WINDOW - kj) < WINDOW
    if prompt:
        ok_p = ok_p & (pl.program_id(1) > 0)
    neg_p = jnp.where(ok_p, 0.0, NEG)
    neg_c = jnp.where(qi - kj >= 0, 0.0, NEG)
    qlane = lax.broadcasted_iota(i32, (tq, LANES), 1) < HEAD_DIM

    for j in range(N_KV_HEADS):
        pair = slice((j // 2) * LANES, (j // 2 + 1) * LANES)
        own_lo = (j % 2 == 0)

        def dup(kpair):
            sw = pltpu.roll(kpair, HEAD_DIM, 1)
            return jnp.where(lo, kpair, sw) if own_lo else jnp.where(lo, sw, kpair)

        def halves(vpair):
            own = jnp.where(lo, vpair, 0.0) if own_lo else jnp.where(lo, 0.0, vpair)
            sw = pltpu.roll(own, HEAD_DIM, 1)
            return (own, sw) if own_lo else (sw, own)

        kk_p = dup(kp_ref[:, pair]).astype(bf16)
        kk_c = dup(kc_ref[:, pair]).astype(bf16)
        v_p = [h.astype(bf16) for h in halves(vp_ref[:, pair])]
        v_c = [h.astype(bf16) for h in halves(vc_ref[:, pair])]

        heads = [(hf, c) for hf in (0, 1) for c in range(3)]
        q_rows = []
        for hf, c in heads:
            ch = slice((3 * j + c) * LANES, (3 * j + c + 1) * LANES)
            qc = q_ref[:, ch].astype(f32)
            q_rows.append(jnp.where(qlane, qc, 0.0) if hf == 0 else jnp.where(qlane, 0.0, qc))
        q_st = jnp.concatenate(q_rows, axis=0).astype(bf16)
        s_p = _nt_dot(q_st, kk_p)
        s_c = _nt_dot(q_st, kk_c)
        p_p, p_c, inv = [], [], []
        for r, (hf, c) in enumerate(heads):
            head = 6 * j + 2 * c + hf
            rows = slice(r * tq, (r + 1) * tq)
            sp = s_p[rows] - _SLOPES[head] * dist_p + neg_p
            sc = s_c[rows] - _SLOPES[head] * dist_c + neg_c
            sink = sink_ref[head]
            m = jnp.maximum(jnp.maximum(jnp.max(sp, axis=-1, keepdims=True),
                                        jnp.max(sc, axis=-1, keepdims=True)), sink)
            ep = jnp.exp(sp - m)
            ec = jnp.exp(sc - m)
            den = jnp.sum(ep, axis=-1, keepdims=True) + jnp.sum(ec, axis=-1, keepdims=True) + jnp.exp(sink - m)
            p_p.append(ep)
            p_c.append(ec)
            inv.append(1.0 / den)
        for c in range(3):
            out = None
            for hf in (0, 1):
                r = hf * 3 + c
                o = (jnp.dot(p_p[r].astype(bf16), v_p[hf], preferred_element_type=f32)
                     + jnp.dot(p_c[r].astype(bf16), v_c[hf], preferred_element_type=f32)) * inv[r]
                out = o if out is None else out + o
            ch = slice((3 * j + c) * LANES, (3 * j + c + 1) * LANES)
            o_ref[:, ch] = out.astype(o_ref.dtype)


def _attn_call(sinks, q, k_arr, v_arr, specs, grid, rows, tq, prompt):
    body = functools.partial(_attn_body, tq=tq, prompt=prompt)
    q_spec, kp_spec, kc_spec, vp_spec, vc_spec, o_spec = specs
    return pl.pallas_call(
        body,
        grid=grid,
        in_specs=[pl.BlockSpec(memory_space=pltpu.SMEM), q_spec, kp_spec, kc_spec, vp_spec, vc_spec],
        out_specs=o_spec,
        out_shape=jax.ShapeDtypeStruct((rows, D_ATTN), bf16),
        compiler_params=_cparams(("arbitrary",) * len(grid)),
        name="attn",
    )(sinks, q, k_arr, k_arr, v_arr, v_arr)


def _hgrn_inputs(q_ref, f_ref, i_ref, lb_ref, sl, *, L, n_valid):
    lb = lb_ref[:, sl]
    fp = f_ref[:, sl]
    q = _silu(q_ref[:, sl])
    t = jnp.exp(-jnp.abs(fp))
    inv = 1.0 / (1.0 + t)
    pos = fp >= 0.0
    logf = jnp.log(lb + (1.0 - lb) * jnp.where(pos, inv, t * inv))
    kk = (1.0 - lb) * jnp.where(pos, t * inv, inv)
    if n_valid < L:
        valid = lax.broadcasted_iota(i32, (L, 1), 0) < n_valid
        logf = jnp.where(valid, logf, 0.0)
        kk = jnp.where(valid, kk, 0.0)
    return q, kk, i_ref[:, sl], logf


def _cum_decay(logf, L):
    row = lax.broadcasted_iota(i32, (L, L), 0)
    col = lax.broadcasted_iota(i32, (L, L), 1)
    tri = (row >= col).astype(bf16)
    h1 = logf.astype(bf16)
    r1 = logf - h1.astype(f32)
    h2 = r1.astype(bf16)
    h3 = (r1 - h2.astype(f32)).astype(bf16)
    return (jnp.dot(tri, h1, preferred_element_type=f32) + jnp.dot(tri, h2, preferred_element_type=f32)
            + jnp.dot(tri, h3, preferred_element_type=f32))


def _sub_ref(b, lo):
    return b[lo - 1:lo] if lo > 0 else jnp.zeros((1, LANES), f32)


def _intra_factored(q, kk, vb, b, *, L, CF):
    rowid = lax.broadcasted_iota(i32, (L, 1), 0)
    row = lax.broadcasted_iota(i32, (L, L), 0)
    col = lax.broadcasted_iota(i32, (L, L), 1)
    blocks = []
    for lo in range(0, L, CF):
        r = _sub_ref(b, lo)
        qt = (q[lo:lo + CF] * jnp.exp(b[lo:lo + CF] - r)).astype(bf16)
        kt = jnp.where(rowid < lo + CF, kk * jnp.exp(jnp.minimum(r - b, DECAY_CLAMP)), 0.0).astype(bf16)
        blocks.append(_nt_dot(qt, kt))
    a = blocks[0] if len(blocks) == 1 else jnp.concatenate(blocks, axis=0)
    a = jnp.where(row >= col, a, 0.0)
    return jnp.dot(a.astype(bf16), vb, preferred_element_type=f32)


def _decay_span(b, *, L, CF):
    span = None
    for lo in range(0, L, CF):
        d = _sub_ref(b, lo) - b[lo + CF - 1:lo + CF]
        span = d if span is None else jnp.maximum(span, d)
    return span


def _intra_pairwise(q, kk, v, b, *, L, C):
    vb = v.astype(bf16)
    rowid = lax.broadcasted_iota(i32, (L, 1), 0)
    tloc = lax.broadcasted_iota(i32, (C, 1), 0)
    outs = []
    for lo in range(0, L, C):
        bi, qi, ki, vi = b[lo:lo + C], q[lo:lo + C], kk[lo:lo + C], v[lo:lo + C]
        if lo > 0:
            r = _sub_ref(b, lo)
            qt = (qi * jnp.exp(bi - r)).astype(bf16)
            kt = jnp.where(rowid < lo, kk * jnp.exp(jnp.minimum(r - b, 0.0)), 0.0).astype(bf16)
            oi = jnp.dot(_nt_dot(qt, kt).astype(bf16), vb, preferred_element_type=f32)
        else:
            oi = jnp.zeros((C, LANES), f32)
        for s in range(C):
            e = jnp.exp(jnp.minimum(bi - bi[s:s + 1], 0.0))
            a = jnp.sum(qi * ki[s:s + 1] * e, axis=-1, keepdims=True)
            oi = oi + jnp.where(tloc >= s, a, 0.0) * vi[s:s + 1]
        outs.append(oi)
    return outs[0] if len(outs) == 1 else jnp.concatenate(outs, axis=0)


def _hgrn_body(*refs, L, C, hpb, n_valid, has_state):
    if has_state:
        q_ref, f_ref, i_ref, g_ref, lb_ref, on_ref, s0_ref, og_ref, so_ref, st_ref, inter_ref = refs
    else:
        q_ref, f_ref, i_ref, g_ref, lb_ref, on_ref, og_ref, so_ref, st_ref, inter_ref = refs
    c = pl.program_id(2)
    nc = pl.num_programs(2)
    CF = min(L, FACTORED_BLOCK)

    @pl.when(c == 0)
    def _():
        for hh in range(hpb):
            st_ref[hh] = s0_ref[hh].T if has_state else jnp.zeros((C_DK, C_DK), f32)

    def finish(o, sl):
        o = o * lax.rsqrt(jnp.mean(o * o, axis=-1, keepdims=True) + EPS) * on_ref[...]
        og_ref[:, sl] = (o * _silu(g_ref[:, sl])).astype(og_ref.dtype)

    span = None
    for hh in range(hpb):
        sl = slice(hh * LANES, (hh + 1) * LANES)
        q, kk, v, logf = _hgrn_inputs(q_ref, f_ref, i_ref, lb_ref, sl, L=L, n_valid=n_valid)
        b = _cum_decay(logf, L)
        vb = v.astype(bf16)
        st = st_ref[hh]
        o_inter = _nt_dot((q * jnp.exp(b)).astype(bf16), st.astype(bf16))
        bl = b[L - 1:L]
        kt2 = (kk * jnp.exp(bl - b)).astype(bf16)
        st_ref[hh] = st * jnp.exp(bl) + _tn_dot(vb, kt2)
        inter_ref[hh] = o_inter
        finish(_intra_factored(q, kk, vb, b, L=L, CF=CF) + o_inter, sl)
        d = _decay_span(b, L=L, CF=CF)
        span = d if span is None else jnp.maximum(span, d)

    @pl.when(jnp.max(span) >= DECAY_CLAMP)
    def _():
        for hh in range(hpb):
            sl = slice(hh * LANES, (hh + 1) * LANES)
            q, kk, v, logf = _hgrn_inputs(q_ref, f_ref, i_ref, lb_ref, sl, L=L, n_valid=n_valid)
            b = _cum_decay(logf, L)
            finish(_intra_pairwise(q, kk, v, b, L=L, C=C) + inter_ref[hh], sl)

    @pl.when(c == nc - 1)
    def _():
        for hh in range(hpb):
            so_ref[hh] = st_ref[hh].T


def _hgrn(u, lb, o_norm, s0, *, nb, nc, L, C, hpb, n_valid, rows):
    nh = C_HEADS // hpb
    w = hpb * LANES
    has_state = s0 is not None
    body = functools.partial(_hgrn_body, L=L, C=C, hpb=hpb, n_valid=n_valid, has_state=has_state)

    def uspec(part):
        return pl.BlockSpec((L, w), lambda b, h, c: (b * nc + c, part * nh + h))

    in_specs = [uspec(0), uspec(1), uspec(2), uspec(3),
                pl.BlockSpec((1, w), lambda b, h, c: (0, h)),
                pl.BlockSpec((1, LANES), lambda b, h, c: (0, 0))]
    args = [u, u, u, u, lb.reshape(1, D_C), o_norm.reshape(1, LANES)]
    if has_state:
        in_specs.append(pl.BlockSpec((None, hpb, C_DK, C_DK), lambda b, h, c: (b, h, 0, 0)))
        args.append(s0)
    return pl.pallas_call(
        body,
        grid=(nb, nh, nc),
        in_specs=in_specs,
        out_specs=[pl.BlockSpec((L, w), lambda b, h, c: (b * nc + c, h)),
                   pl.BlockSpec((None, hpb, C_DK, C_DK), lambda b, h, c: (b, h, 0, 0))],
        out_shape=[jax.ShapeDtypeStruct((rows, D_C), bf16),
                   jax.ShapeDtypeStruct((nb, C_HEADS, C_DK, C_DK), f32)],
        scratch_shapes=[pltpu.VMEM((hpb, C_DK, C_DK), f32), pltpu.VMEM((hpb, L, LANES), f32)],
        compiler_params=_cparams(("arbitrary", "arbitrary", "arbitrary")),
        name="hgrn",
    )(*args)


def _moe_pre_body(x_ref, g_ref, scp_ref, shp_ref, scs_ref, shs_ref, wrt_ref, rb_ref,
                  h_ref, idx_ref, wts_ref, rank_ref, cnt_ref, carry):
    i = pl.program_id(0)

    @pl.when(i == 0)
    def _():
        carry[...] = jnp.zeros_like(carry)

    y = _norm_rows(x_ref[...], g_ref[...])

    @pl.when(i < NPT)
    def _():
        h_ref[...] = y * (1.0 + scp_ref[...]) + shp_ref[...]

    @pl.when(i >= NPT)
    def _():
        h_ref[:NS] = y[:NS] * (1.0 + scs_ref[...]) + shs_ref[...]
        h_ref[NS:] = jnp.zeros((TM - NS, D_MODEL), f32)

    h = h_ref[...]

    logits = _nt_dot(wrt_ref[...], h, precision=lax.Precision.HIGHEST)
    scores = jax.nn.sigmoid(logits)
    sel = scores + rb_ref[...]
    best = jnp.zeros((1, TM), i32)
    best_v = None
    for g in range(N_EXPERTS // EXP_PER_GROUP):
        r = [sel[EXP_PER_GROUP * g + t:EXP_PER_GROUP * g + t + 1] for t in range(EXP_PER_GROUP)]
        top2 = None
        for a in range(EXP_PER_GROUP):
            for bb in range(a + 1, EXP_PER_GROUP):
                s2 = r[a] + r[bb]
                top2 = s2 if top2 is None else jnp.maximum(top2, s2)
        if g == 0:
            best_v = top2
        else:
            upd = top2 > best_v
            best = jnp.where(upd, g, best)
            best_v = jnp.where(upd, top2, best_v)
    eidx = lax.broadcasted_iota(i32, (N_EXPERTS, TM), 0)
    masked = jnp.where(jnp.right_shift(eidx, 2) == best, sel, NEG)
    m1 = jnp.max(masked, axis=0, keepdims=True)
    i1 = jnp.min(jnp.where(masked == m1, eidx, N_EXPERTS), axis=0, keepdims=True)
    masked2 = jnp.where(eidx == i1, -jnp.inf, masked)
    m2 = jnp.max(masked2, axis=0, keepdims=True)
    i2 = jnp.min(jnp.where(masked2 == m2, eidx, N_EXPERTS), axis=0, keepdims=True)
    w1 = jnp.sum(jnp.where(eidx == i1, scores, 0.0), axis=0, keepdims=True)
    w2 = jnp.sum(jnp.where(eidx == i2, scores, 0.0), axis=0, keepdims=True)
    den = w1 + w2
    idx_ref[0:1, :] = i1
    idx_ref[1:2, :] = i2
    wts_ref[0:1, :] = w1 / den
    wts_ref[1:2, :] = w2 / den

    colg = i * TM + lax.broadcasted_iota(i32, (1, TM), 1)
    oh = jnp.where(((eidx == i1) | (eidx == i2)) & (colg < N_REAL), 1.0, 0.0)
    rr = lax.broadcasted_iota(i32, (TM, TM), 0)
    cc = lax.broadcasted_iota(i32, (TM, TM), 1)
    before = (rr < cc).astype(bf16)
    rank = jnp.dot(oh.astype(bf16), before, preferred_element_type=f32) + carry[:, 0:1]
    rank_ref[0:1, :] = jnp.sum(jnp.where(eidx == i1, rank, 0.0), axis=0, keepdims=True).astype(i32)
    rank_ref[1:2, :] = jnp.sum(jnp.where(eidx == i2, rank, 0.0), axis=0, keepdims=True).astype(i32)
    carry[...] = carry[...] + jnp.sum(oh, axis=1, keepdims=True)
    cnt_ref[...] = carry[...]


def _moe_pre(x_all, g, modp, mods, layer, router_wt, router_b):
    scp, scs = _mod_specs(layer, 4)
    shp, shs = _mod_specs(layer, 3)
    return pl.pallas_call(
        _moe_pre_body,
        grid=(N_TILES,),
        in_specs=[pl.BlockSpec((TM, D_MODEL), lambda i: (i, 0)),
                  pl.BlockSpec((1, D_MODEL), lambda i: (0, 0)),
                  scp, shp, scs, shs,
                  pl.BlockSpec((N_EXPERTS, D_MODEL), lambda i: (0, 0)),
                  pl.BlockSpec((N_EXPERTS, 1), lambda i: (0, 0))],
        out_specs=[pl.BlockSpec((TM, D_MODEL), lambda i: (i, 0)),
                   pl.BlockSpec((2, TM), lambda i: (0, i)),
                   pl.BlockSpec((2, TM), lambda i: (0, i)),
                   pl.BlockSpec((2, TM), lambda i: (0, i)),
                   pl.BlockSpec((N_EXPERTS, LANES), lambda i: (0, 0))],
        out_shape=[jax.ShapeDtypeStruct((M_ALL, D_MODEL), f32),
                   jax.ShapeDtypeStruct((2, M_ALL), i32),
                   jax.ShapeDtypeStruct((2, M_ALL), f32),
                   jax.ShapeDtypeStruct((2, M_ALL), i32),
                   jax.ShapeDtypeStruct((N_EXPERTS, LANES), f32)],
        scratch_shapes=[pltpu.VMEM((N_EXPERTS, LANES), f32)],
        compiler_params=_cparams(("arbitrary",)),
        name="moe_pre",
    )(x_all, g.reshape(1, D_MODEL), modp, modp, mods, mods, router_wt, router_b.reshape(N_EXPERTS, 1))


def _dispatch_body(pos_ref, fill_lo_ref, fill_hi_ref, nu_ref, h_ref, hs_hbm, zero_buf, sem, tile_sem):
    i = pl.program_id(0)
    base = i * TMD

    def row_copy(src, dst_row):
        return pltpu.make_async_copy(src, hs_hbm.at[pl.ds(dst_row, 1)], sem)

    def wait_row(n, carry):
        row_copy(h_ref.at[pl.ds(0, 1)], 0).wait()
        return carry

    @pl.when(i == 0)
    def _():
        zero_buf[...] = jnp.zeros_like(zero_buf)
        zero_row = zero_buf.at[pl.ds(0, 1)]

        def tile_copy(t):
            return pltpu.make_async_copy(zero_buf, hs_hbm.at[pl.ds(t * TME, TME)], tile_sem)

        def tail_start(t, carry):
            tile_copy(t).start()
            return carry

        def tail_wait(t, carry):
            tile_copy(t).wait()
            return carry

        def fill_expert(e, carry):
            def fill(r, c2):
                row_copy(zero_row, r).start()
                return c2
            lax.fori_loop(fill_lo_ref[e], fill_hi_ref[e], fill, 0)
            lax.fori_loop(fill_lo_ref[e], fill_hi_ref[e], wait_row, 0)
            return carry

        lax.fori_loop(nu_ref[0], NT_E, tail_start, 0)
        lax.fori_loop(0, N_EXPERTS, fill_expert, 0)
        lax.fori_loop(nu_ref[0], NT_E, tail_wait, 0)

    def issue(r, carry):
        src = h_ref.at[pl.ds(r, 1)]
        row_copy(src, pos_ref[base + r]).start()
        row_copy(src, pos_ref[M_ALL + base + r]).start()
        return carry

    lax.fori_loop(0, TMD, issue, 0, unroll=8)
    lax.fori_loop(0, 2 * TMD, wait_row, 0, unroll=8)


def _dispatch(pos_flat, fill_lo, fill_hi, n_used, h3):
    return pl.pallas_call(
        _dispatch_body,
        grid_spec=pltpu.PrefetchScalarGridSpec(
            num_scalar_prefetch=4,
            grid=(N_REAL // TMD,),
            in_specs=[pl.BlockSpec((TMD, D_MODEL), lambda i, *_: (i, 0))],
            out_specs=pl.BlockSpec(memory_space=pl.ANY),
            scratch_shapes=[pltpu.VMEM((TME, D_MODEL), f32), pltpu.SemaphoreType.DMA,
                            pltpu.SemaphoreType.DMA],
        ),
        out_shape=jax.ShapeDtypeStruct((R_MAX, D_MODEL), f32),
        compiler_params=_cparams(("arbitrary",)),
        name="moe_dispatch",
    )(pos_flat, fill_lo, fill_hi, n_used, h3)


def _expert_body(te_ref, nu_ref, hs_ref, w1_ref, w3_ref, w2_ref, y_ref, w1b, w3b, w2b):
    t = pl.program_id(0)

    @pl.when(t < nu_ref[0])
    def _():
        changed = (t == 0) | (te_ref[t] != te_ref[jnp.maximum(t - 1, 0)])

        @pl.when(changed)
        def _():
            w1b[...] = w1_ref[...].astype(bf16)
            w3b[...] = w3_ref[...].astype(bf16)
            w2b[...] = w2_ref[...].astype(bf16)

        hb = hs_ref[...].astype(bf16)
        a = jnp.dot(hb, w1b[...], preferred_element_type=f32)
        b = jnp.dot(hb, w3b[...], preferred_element_type=f32)
        hid = (_silu(a) * b).astype(bf16)
        y_ref[...] = jnp.dot(hid, w2b[...], preferred_element_type=f32)

    @pl.when(t >= nu_ref[0])
    def _():
        y_ref[...] = jnp.zeros_like(y_ref)


def _experts(tile_expert, n_used, hs, w1, w3, w2, layer):
    def row_map(t, te, nu):
        return (jnp.minimum(t, nu[0] - 1), 0)

    def out_map(t, te, nu):
        return (t, 0)

    def w_map(t, te, nu):
        return (layer, te[t], 0, 0)

    return pl.pallas_call(
        _expert_body,
        grid_spec=pltpu.PrefetchScalarGridSpec(
            num_scalar_prefetch=2,
            grid=(NT_E,),
            in_specs=[pl.BlockSpec((TME, D_MODEL), row_map),
                      pl.BlockSpec((None, None, D_MODEL, D_FF), w_map),
                      pl.BlockSpec((None, None, D_MODEL, D_FF), w_map),
                      pl.BlockSpec((None, None, D_FF, D_MODEL), w_map)],
            out_specs=pl.BlockSpec((TME, D_MODEL), out_map),
            scratch_shapes=[pltpu.VMEM((D_MODEL, D_FF), bf16), pltpu.VMEM((D_MODEL, D_FF), bf16),
                            pltpu.VMEM((D_FF, D_MODEL), bf16)],
        ),
        out_shape=jax.ShapeDtypeStruct((R_MAX, D_MODEL), f32),
        compiler_params=_cparams(("arbitrary",)),
        name="moe_experts",
    )(tile_expert, n_used, hs, w1, w3, w2)


def _combine_body(pos_ref, x_ref, w_ref, gp_ref, gs_ref, y_hbm, o_ref, buf0, buf1, sem):
    i = pl.program_id(0)
    base = i * TMC
    n_real_tiles = N_REAL // TMC

    def copies(r):
        return (pltpu.make_async_copy(y_hbm.at[pl.ds(pos_ref[base + r], 1)], buf0.at[pl.ds(r, 1)], sem),
                pltpu.make_async_copy(y_hbm.at[pl.ds(pos_ref[M_ALL + base + r], 1)], buf1.at[pl.ds(r, 1)], sem))

    def finish(gate):
        def start(r, c):
            for cp in copies(r):
                cp.start()
            return c

        def wait(r, c):
            for cp in copies(r):
                cp.wait()
            return c

        lax.fori_loop(0, TMC, start, 0, unroll=8)
        lax.fori_loop(0, TMC, wait, 0, unroll=8)
        m = w_ref[:, 0:1] * buf0[...] + w_ref[:, 1:2] * buf1[...]
        o_ref[...] = x_ref[...] + gate * m

    @pl.when(i < NP // TMC)
    def _():
        finish(gp_ref[...])

    @pl.when((i >= NP // TMC) & (i < n_real_tiles))
    def _():
        finish(gs_ref[...])

    @pl.when(i >= n_real_tiles)
    def _():
        o_ref[...] = jnp.zeros_like(o_ref)


def _combine(pos_flat, x_all, wts_t, modp, mods, layer, y):
    tiles_per_seq = SEQ // TMC
    return pl.pallas_call(
        _combine_body,
        grid_spec=pltpu.PrefetchScalarGridSpec(
            num_scalar_prefetch=1,
            grid=(M_ALL // TMC,),
            in_specs=[pl.BlockSpec((TMC, D_MODEL), lambda i, p: (i, 0)),
                      pl.BlockSpec((TMC, 2), lambda i, p: (i, 0)),
                      pl.BlockSpec((None, None, None, 1, D_MODEL),
                                   lambda i, p: (layer, 5, jnp.minimum(i // tiles_per_seq, BATCH - 1), 0, 0)),
                      pl.BlockSpec((None, None, NS, D_MODEL), lambda i, p: (layer, 5, 0, 0)),
                      pl.BlockSpec(memory_space=pl.ANY)],
            out_specs=pl.BlockSpec((TMC, D_MODEL), lambda i, p: (i, 0)),
            scratch_shapes=[pltpu.VMEM((TMC, D_MODEL), f32), pltpu.VMEM((TMC, D_MODEL), f32),
                            pltpu.SemaphoreType.DMA],
        ),
        out_shape=jax.ShapeDtypeStruct((M_ALL, D_MODEL), f32),
        compiler_params=_cparams(("arbitrary",)),
        name="moe_combine",
    )(pos_flat, x_all, wts_t, modp, mods, y)


def _moe(x_all, g, modp, mods, layer, router_wt, router_b, w1, w3, w2):
    h3, idx, wts, rank, cnt = _moe_pre(x_all, g, modp, mods, layer, router_wt, router_b)
    counts = cnt[:, 0].astype(i32)
    padded = ((counts + TME - 1) // TME) * TME
    ends = jnp.cumsum(padded)
    offs = ends - padded
    real = (jnp.arange(M_ALL) < N_REAL)[None, :]
    expert_ids = jnp.arange(N_EXPERTS, dtype=i32)[:, None, None]
    seg_start = jnp.sum(jnp.where(idx[None] == expert_ids, offs[:, None, None], 0), axis=0)
    pos = jnp.where(real, seg_start + rank, 0).astype(i32)
    pos_flat = pos.reshape(2 * M_ALL)
    n_used = (ends[-1] // TME).astype(i32).reshape(1)
    tile_start = jnp.minimum(jnp.arange(NT_E, dtype=i32), n_used[0] - 1) * TME
    tile_expert = jnp.sum((tile_start[:, None] >= ends[None, :]).astype(i32), axis=1).astype(i32)
    hs = _dispatch(pos_flat, (offs + counts).astype(i32), ends.astype(i32), n_used, h3)
    y = _experts(tile_expert, n_used, hs, w1, w3, w2, layer)
    return _combine(pos_flat, x_all, wts.T, modp, mods, layer, y)


def _sample_rows(a, width):
    return a.reshape(DEC_BATCH, SP, width)[:, :DEC_SEQ].reshape(NS, width)


def kernel(x_prompt, x_sample, c_prompt, c_sample, state_pool, cache_k, cache_v, state_hgrn, norm_g, ada_w, ada_b, w_in_ab, pool_w, pool_scale, q_norm, k_norm, attn_sinks, w_out_ab, w_in_c, lb_param, o_norm, w_out_c, router_w, router_b, moe_w1, moe_w3, moe_w2):
    n_c = BATCH + DEC_BATCH
    c_rows = ((n_c + 7) // 8) * 8
    c_all = jnp.concatenate([c_prompt, c_sample, jnp.zeros((c_rows - n_c, D_MODEL), f32)], axis=0)
    mod = _ada_mod(c_all, ada_w, ada_b).reshape(DEPTH, c_rows, 6, D_MODEL)
    modp = mod[:, :BATCH].transpose(0, 2, 1, 3).reshape(DEPTH, 6, BATCH, 1, D_MODEL)
    mods = jnp.repeat(mod[:, BATCH:n_c].transpose(0, 2, 1, 3), DEC_SEQ, axis=2)

    x_all = jnp.concatenate([x_prompt.reshape(NP, D_MODEL), x_sample.reshape(NS, D_MODEL),
                             jnp.zeros((M_ALL - N_REAL, D_MODEL), f32)], axis=0)
    router_wt = router_w.T

    h = _normmod(x_all, norm_g[0, 0], modp, mods, 0, 0, 1)
    u = _proj(h, w_in_ab[0], 512)
    tp = 256
    pool_p, qn_p, kn_p = _mix0_pre(
        u, u, lambda i: (jnp.maximum(i * (tp // 16) - 1, 0), 0), NP, tp, SEQ // tp, False,
        pool_w[0], pool_scale[0], q_norm[0], k_norm[0])
    nb = SEQ // WINDOW
    blk = lambda b, n: (b * nb + n, 0)
    prv = lambda b, n: (jnp.maximum(b * nb + n - 1, 0), 0)
    vcol = (D_POOL + D_ATTN + D_KV) // D_KV
    attn_p = _attn_call(
        attn_sinks[0], qn_p, kn_p, u,
        (pl.BlockSpec((WINDOW, D_ATTN), blk), pl.BlockSpec((WINDOW, D_KV), prv), pl.BlockSpec((WINDOW, D_KV), blk),
         pl.BlockSpec((WINDOW, D_KV), lambda b, n: (jnp.maximum(b * nb + n - 1, 0), vcol)),
         pl.BlockSpec((WINDOW, D_KV), lambda b, n: (b * nb + n, vcol)),
         pl.BlockSpec((WINDOW, D_ATTN), blk)),
        (BATCH, nb), NP, WINDOW, True)
    u_s = u[NP:N_REAL].reshape(DEC_BATCH, DEC_SEQ, D_IN_AB)
    u_s16 = jnp.pad(u_s, ((0, 0), (0, SP - DEC_SEQ), (0, 0))).reshape(NSP, D_IN_AB)
    halo_s = jnp.pad(state_pool[0], ((0, 0), (1, 0), (0, 0))).reshape(DEC_BATCH * 16, D_POOL)
    pool_s, qn_s, kn_s = _mix0_pre(
        u_s16, halo_s, lambda i: (i, 0), NSP, SP, 1, True,
        pool_w[0], pool_scale[0], q_norm[0], k_norm[0])
    kn_new = kn_s.reshape(DEC_BATCH, SP, D_KV)[:, :DEC_SEQ]
    v_new = u_s[:, :, D_POOL + D_ATTN + D_KV:]
    zpad = jnp.zeros((DEC_BATCH, WINDOW - DEC_SEQ, D_KV), f32)
    k_ext = jnp.concatenate([cache_k[0].reshape(DEC_BATCH, WINDOW, D_KV), kn_new, zpad], axis=1)
    v_ext = jnp.concatenate([cache_v[0].reshape(DEC_BATCH, WINDOW, D_KV), v_new, zpad], axis=1)
    ext_p = pl.BlockSpec((None, WINDOW, D_KV), lambda b: (b, 0, 0))
    ext_c = pl.BlockSpec((None, WINDOW, D_KV), lambda b: (b, 1, 0))
    attn_s = _attn_call(
        attn_sinks[0], qn_s, k_ext, v_ext,
        (pl.BlockSpec((SP, D_ATTN), lambda b: (b, 0)), ext_p, ext_c, ext_p, ext_c,
         pl.BlockSpec((SP, D_ATTN), lambda b: (b, 0))),
        (DEC_BATCH,), NSP, SP, False)
    x_all = _oproj([pool_p, attn_p], [_sample_rows(pool_s, D_POOL), _sample_rows(attn_s, D_ATTN)],
                   w_out_ab[0], x_all, modp, mods, 0, 2)
    x_all = _moe(x_all, norm_g[0, 1], modp, mods, 0, router_wt, router_b, moe_w1, moe_w3, moe_w2)

    lb_soft = jax.nn.softmax(lb_param.astype(f32), axis=0)
    lower_bounds = jnp.cumsum(lb_soft, axis=0) - lb_soft[0:1]
    h = _normmod(x_all, norm_g[1, 0], modp, mods, 1, 0, 1)
    u1 = _proj(h, w_in_c[0], 1024)
    og_p, hg_p = _hgrn(u1, lower_bounds[1], o_norm[0], None,
                       nb=BATCH, nc=SEQ // 64, L=64, C=16, hpb=4, n_valid=64, rows=NP)
    u1_s = jnp.pad(u1[NP:N_REAL].reshape(DEC_BATCH, DEC_SEQ, 4 * D_C),
                   ((0, 0), (0, SP - DEC_SEQ), (0, 0))).reshape(NSP, 4 * D_C)
    og_s, hg_s = _hgrn(u1_s, lower_bounds[1], o_norm[0], state_hgrn[0],
                       nb=DEC_BATCH, nc=1, L=SP, C=SP, hpb=4, n_valid=DEC_SEQ, rows=NSP)
    x_all = _oproj([og_p], [_sample_rows(og_s, D_C)], w_out_c[0], x_all, modp, mods, 1, 2)
    x_all = _moe(x_all, norm_g[1, 1], modp, mods, 1, router_wt, router_b, moe_w1, moe_w3, moe_w2)

    y_prompt = x_all[:NP].reshape(BATCH, SEQ, D_MODEL)
    y_sample = x_all[NP:N_REAL].reshape(DEC_BATCH, DEC_SEQ, D_MODEL)
    u_p = u[:NP].reshape(BATCH, SEQ, D_IN_AB)
    pool_prompt = u_p[:, SEQ - POOL_STATE:, :D_POOL][None]
    pool_sample = jnp.concatenate([state_pool[0], u_s[:, :, :D_POOL]], axis=1)[:, -POOL_STATE:][None]
    kv_shape = (N_KV_HEADS, HEAD_DIM)
    k_prompt = kn_p.reshape(BATCH, SEQ, *kv_shape)[:, SEQ - WINDOW:][None]
    v_prompt = u_p[:, SEQ - WINDOW:, D_POOL + D_ATTN + D_KV:].reshape(BATCH, WINDOW, *kv_shape)[None]
    k_sample = jnp.concatenate([cache_k[0], kn_new.reshape(DEC_BATCH, DEC_SEQ, *kv_shape)], axis=1)[:, -WINDOW:][None]
    v_sample = jnp.concatenate([cache_v[0], v_new.reshape(DEC_BATCH, DEC_SEQ, *kv_shape)], axis=1)[:, -WINDOW:][None]
    return (y_prompt, y_sample, pool_prompt, pool_sample, k_prompt, k_sample, v_prompt, v_sample,
            hg_p[None], hg_s[None])
```

```python
import functools
import math

import numpy as np
import jax
import jax.numpy as jnp
from jax import lax
from jax.experimental import pallas as pl
from jax.experimental.pallas import tpu as pltpu

f32 = jnp.float32
bf16 = jnp.bfloat16
i32 = jnp.int32

D_MODEL = 2048
BATCH = 4
SEQ = 2048
DEPTH = 2
DEC_BATCH = 32
DEC_SEQ = 4
EPS = 1e-6
POOL_WINDOWS = (2, 4, 8, 16)
D_POOL = 512
POOL_GROUP = 128
POOL_STATE = 15
HEAD_DIM = 64
N_Q_HEADS = 24
N_KV_HEADS = 4
Q_PER_KV = 6
WINDOW = 128
D_ATTN = 1536
D_KV = 256
D_IN_AB = 2560
C_DK = 128
C_HEADS = 16
D_C = 2048
N_EXPERTS = 16
EXP_PER_GROUP = 4
D_FF = 512
NEG = -1e30

LANES = 128
VMEM_LIMIT = 56 * 1024 * 1024

NP = BATCH * SEQ
NS = DEC_BATCH * DEC_SEQ
N_REAL = NP + NS
TM = 512
M_ALL = ((N_REAL + TM - 1) // TM) * TM
N_TILES = M_ALL // TM
NPT = NP // TM
TPB = SEQ // TM
SP = 16
NSP = DEC_BATCH * SP

TME = 256
R_MAX = ((2 * N_REAL + N_EXPERTS * (TME - 1)) // TME + 1) * TME
NT_E = R_MAX // TME
TMC = 128
TMD = 640

FACTORED_BLOCK = 32
DECAY_CLAMP = 60.0


def _cparams(sem, vmem=VMEM_LIMIT):
    return pltpu.CompilerParams(dimension_semantics=sem, vmem_limit_bytes=vmem)


def _silu(x):
    return x * jax.nn.sigmoid(x)


def _nt_dot(a, b, **kw):
    return lax.dot_general(a, b, (((1,), (1,)), ((), ())), preferred_element_type=f32, **kw)


def _tn_dot(a, b):
    return lax.dot_general(a, b, (((0,), (0,)), ((), ())), preferred_element_type=f32)


def _ada_body(c_ref, w_ref, b_ref, o_ref):
    s = _silu(c_ref[...]).astype(bf16)
    o_ref[...] = jnp.dot(s, w_ref[...].astype(bf16), preferred_element_type=f32) + b_ref[...]


def _ada_mod(c_all, ada_w, ada_b):
    depth, d, e = ada_w.shape
    cr = c_all.shape[0]
    tn = 1024
    return pl.pallas_call(
        _ada_body,
        grid=(depth, e // tn),
        in_specs=[
            pl.BlockSpec((cr, d), lambda l, j: (0, 0)),
            pl.BlockSpec((None, d, tn), lambda l, j: (l, 0, j)),
            pl.BlockSpec((None, 1, tn), lambda l, j: (l, 0, j)),
        ],
        out_specs=pl.BlockSpec((None, cr, tn), lambda l, j: (l, 0, j)),
        out_shape=jax.ShapeDtypeStruct((depth, cr, e), f32),
        compiler_params=_cparams(("arbitrary", "arbitrary")),
        name="ada_mod",
    )(c_all, ada_w, ada_b.reshape(depth, 1, e))


def _mod_specs(layer, k):
    p = pl.BlockSpec((None, None, None, 1, D_MODEL),
                     lambda i: (layer, k, jnp.minimum(i // TPB, BATCH - 1), 0, 0))
    s = pl.BlockSpec((None, None, NS, D_MODEL), lambda i: (layer, k, 0, 0))
    return p, s


def _norm_rows(x, g):
    return x * lax.rsqrt(jnp.mean(x * x, axis=-1, keepdims=True) + EPS) * g


def _normmod_body(x_ref, g_ref, scp_ref, shp_ref, scs_ref, shs_ref, h_ref):
    i = pl.program_id(0)
    y = _norm_rows(x_ref[...], g_ref[...])

    @pl.when(i < NPT)
    def _():
        h_ref[...] = (y * (1.0 + scp_ref[...]) + shp_ref[...]).astype(h_ref.dtype)

    @pl.when(i >= NPT)
    def _():
        h_ref[:NS] = (y[:NS] * (1.0 + scs_ref[...]) + shs_ref[...]).astype(h_ref.dtype)
        h_ref[NS:] = jnp.zeros((TM - NS, D_MODEL), h_ref.dtype)


def _normmod(x_all, g, modp, mods, layer, k_shift, k_scale):
    scp, scs = _mod_specs(layer, k_scale)
    shp, shs = _mod_specs(layer, k_shift)
    return pl.pallas_call(
        _normmod_body,
        grid=(N_TILES,),
        in_specs=[pl.BlockSpec((TM, D_MODEL), lambda i: (i, 0)),
                  pl.BlockSpec((1, D_MODEL), lambda i: (0, 0)),
                  scp, shp, scs, shs],
        out_specs=pl.BlockSpec((TM, D_MODEL), lambda i: (i, 0)),
        out_shape=jax.ShapeDtypeStruct((M_ALL, D_MODEL), bf16),
        compiler_params=_cparams(("arbitrary",)),
        name="normmod",
    )(x_all, g.reshape(1, D_MODEL), modp, modp, mods, mods)


def _proj_body(h_ref, w_ref, o_ref, wb_ref):
    @pl.when(pl.program_id(1) == 0)
    def _():
        wb_ref[...] = w_ref[...].astype(bf16)

    o_ref[...] = jnp.dot(h_ref[...], wb_ref[...], preferred_element_type=f32)


def _proj(h_all, w, tn):
    k, e = w.shape
    return pl.pallas_call(
        _proj_body,
        grid=(e // tn, N_TILES),
        in_specs=[pl.BlockSpec((TM, k), lambda j, i: (i, 0)),
                  pl.BlockSpec((k, tn), lambda j, i: (0, j))],
        out_specs=pl.BlockSpec((TM, tn), lambda j, i: (i, j)),
        out_shape=jax.ShapeDtypeStruct((M_ALL, e), f32),
        scratch_shapes=[pltpu.VMEM((k, tn), bf16)],
        compiler_params=_cparams(("arbitrary", "arbitrary")),
        name="proj",
    )(h_all, w)


def _oproj_body(*refs, n_lhs):
    ap_refs = refs[:n_lhs]
    as_refs = refs[n_lhs:2 * n_lhs]
    w_ref, x_ref, gp_ref, gs_ref, o_ref, wb_ref = refs[2 * n_lhs:]
    i = pl.program_id(1)

    @pl.when(i == 0)
    def _():
        wb_ref[...] = w_ref[...].astype(bf16)

    def matmul(a_refs, rows):
        off, acc = 0, None
        for a in a_refs:
            kk = a.shape[1]
            part = jnp.dot(a[:rows], wb_ref[off:off + kk, :], preferred_element_type=f32)
            acc = part if acc is None else acc + part
            off += kk
        return acc

    @pl.when(i < NPT)
    def _():
        o_ref[...] = x_ref[...] + gp_ref[...] * matmul(ap_refs, TM)

    @pl.when(i >= NPT)
    def _():
        o_ref[:NS] = x_ref[:NS] + gs_ref[...] * matmul(as_refs, NS)
        o_ref[NS:] = jnp.zeros((TM - NS, o_ref.shape[1]), f32)


def _oproj(a_prompt, a_sample, w, x_all, modp, mods, layer, k_gate):
    n_lhs = len(a_prompt)
    kdim, e = w.shape
    tn = 1024
    in_specs = []
    for a in a_prompt:
        in_specs.append(pl.BlockSpec((TM, a.shape[1]), lambda j, i: (jnp.minimum(i, NPT - 1), 0)))
    for a in a_sample:
        in_specs.append(pl.BlockSpec((NS, a.shape[1]), lambda j, i: (0, 0)))
    in_specs += [
        pl.BlockSpec((kdim, tn), lambda j, i: (0, j)),
        pl.BlockSpec((TM, tn), lambda j, i: (i, j)),
        pl.BlockSpec((None, None, None, 1, tn),
                     lambda j, i: (layer, k_gate, jnp.minimum(i // TPB, BATCH - 1), 0, j)),
        pl.BlockSpec((None, None, NS, tn), lambda j, i: (layer, k_gate, 0, j)),
    ]
    return pl.pallas_call(
        functools.partial(_oproj_body, n_lhs=n_lhs),
        grid=(e // tn, N_TILES),
        in_specs=in_specs,
        out_specs=pl.BlockSpec((TM, tn), lambda j, i: (i, j)),
        out_shape=jax.ShapeDtypeStruct((M_ALL, e), f32),
        scratch_shapes=[pltpu.VMEM((kdim, tn), bf16)],
        compiler_params=_cparams(("arbitrary", "arbitrary")),
        name="oproj",
    )(*a_prompt, *a_sample, w, x_all, modp, mods)


def _head_norm(x, w_row, scale):
    lane = lax.broadcasted_iota(i32, x.shape, 1)
    lo = lane < HEAD_DIM
    ss = x * x
    m_lo = jnp.sum(jnp.where(lo, ss, 0.0), axis=-1, keepdims=True)
    m_hi = jnp.sum(jnp.where(lo, 0.0, ss), axis=-1, keepdims=True)
    r = jnp.where(lo, lax.rsqrt(m_lo * (1.0 / HEAD_DIM) + EPS), lax.rsqrt(m_hi * (1.0 / HEAD_DIM) + EPS))
    return x * r * (w_row * scale)


def _mix0_pre_body(cur_ref, halo_ref, q0_ref, q1_ref, q2_ref, k_ref, pw_ref, ps_ref, qw_ref, kw_ref,
                   pool_ref, qn_ref, kn_ref, ext_ref, *, tp, tiles_per_seq, full_count):
    i = pl.program_id(0)
    halo = halo_ref[...]
    if not full_count:
        halo = jnp.where(i % tiles_per_seq == 0, 0.0, halo)
    ext_ref[0:16, :] = halo
    ext_ref[16:16 + tp, :] = cur_ref[...]
    t_pos = (i % tiles_per_seq) * tp + lax.broadcasted_iota(i32, (tp, 1), 0)
    for g, w in enumerate(POOL_WINDOWS):
        ch = slice(g * POOL_GROUP, (g + 1) * POOL_GROUP)
        acc = ext_ref[16:16 + tp, ch]
        for dlt in range(1, w):
            acc = acc + ext_ref[16 - dlt:16 - dlt + tp, ch]
        if full_count:
            mean = acc * (1.0 / w)
        else:
            cnt = jnp.minimum(t_pos + 1, w).astype(f32)
            mean = acc / cnt
        d = mean - ext_ref[16:16 + tp, ch]
        y = jnp.dot(d.astype(bf16), pw_ref[g].astype(bf16), preferred_element_type=f32)
        pool_ref[:, ch] = (y * ps_ref[:, ch]).astype(pool_ref.dtype)
    for c3, q_ref in enumerate((q0_ref, q1_ref, q2_ref)):
        for c in range(4):
            sl = slice(c * LANES, (c + 1) * LANES)
            dst = slice((c3 * 4 + c) * LANES, (c3 * 4 + c + 1) * LANES)
            qn_ref[:, dst] = _head_norm(q_ref[:, sl], qw_ref[...], HEAD_DIM ** -0.5).astype(qn_ref.dtype)
    for c in range(2):
        sl = slice(c * LANES, (c + 1) * LANES)
        kn_ref[:, sl] = _head_norm(k_ref[:, sl], kw_ref[...], 1.0)


def _mix0_pre(u, halo_src, halo_map, rows, tp, tiles_per_seq, full_count, pool_w, pool_scale, q_norm, k_norm):
    qw = jnp.tile(q_norm.reshape(1, HEAD_DIM), (1, 2))
    kw = jnp.tile(k_norm.reshape(1, HEAD_DIM), (1, 2))
    body = functools.partial(_mix0_pre_body, tp=tp, tiles_per_seq=tiles_per_seq, full_count=full_count)
    return pl.pallas_call(
        body,
        grid=(rows // tp,),
        in_specs=[
            pl.BlockSpec((tp, D_POOL), lambda i: (i, 0)),
            pl.BlockSpec((16, D_POOL), halo_map),
            pl.BlockSpec((tp, 512), lambda i: (i, 1)),
            pl.BlockSpec((tp, 512), lambda i: (i, 2)),
            pl.BlockSpec((tp, 512), lambda i: (i, 3)),
            pl.BlockSpec((tp, D_KV), lambda i: (i, (D_POOL + D_ATTN) // D_KV)),
            pl.BlockSpec((4, POOL_GROUP, POOL_GROUP), lambda i: (0, 0, 0)),
            pl.BlockSpec((1, D_POOL), lambda i: (0, 0)),
            pl.BlockSpec((1, LANES), lambda i: (0, 0)),
            pl.BlockSpec((1, LANES), lambda i: (0, 0)),
        ],
        out_specs=[
            pl.BlockSpec((tp, D_POOL), lambda i: (i, 0)),
            pl.BlockSpec((tp, D_ATTN), lambda i: (i, 0)),
            pl.BlockSpec((tp, D_KV), lambda i: (i, 0)),
        ],
        out_shape=[
            jax.ShapeDtypeStruct((rows, D_POOL), bf16),
            jax.ShapeDtypeStruct((rows, D_ATTN), bf16),
            jax.ShapeDtypeStruct((rows, D_KV), f32),
        ],
        scratch_shapes=[pltpu.VMEM((16 + tp, D_POOL), f32)],
        compiler_params=_cparams(("arbitrary",)),
        name="mix0_pre",
    )(u, halo_src, u, u, u, u, pool_w, pool_scale.reshape(1, D_POOL), qw, kw)


def _alibi_slopes(n):
    def pow2_slopes(m):
        start = 2.0 ** (-8.0 / m)
        return [start ** (i + 1) for i in range(m)]
    if n & (n - 1) == 0:
        s = pow2_slopes(n)
    else:
        c = 2 ** int(math.floor(math.log2(n)))
        s = pow2_slopes(c) + pow2_slopes(2 * c)[0::2][: n - c]
    return [float(v) for v in np.asarray(s, dtype=np.float32)]


_SLOPES = _alibi_slopes(N_Q_HEADS)


def _attn_body(sink_ref, q_ref, kp_ref, kc_ref, vp_ref, vc_ref, o_ref, *, tq, prompt):
    lane = lax.broadcasted_iota(i32, (WINDOW, LANES), 1)
    lo = lane < HEAD_DIM
    qi = lax.broadcasted_iota(i32, (tq, WINDOW), 0)
    kj = lax.broadcasted_iota(i32, (tq, WINDOW), 1)
    dist_p = (qi + WINDOW - kj).astype(f32)
    dist_c = (qi - kj).astype(f32)
    ok_p = (qi + WINDOW - kj) < WINDOW
    if prompt:
        ok_p = ok_p & (pl.program_id(1) > 0)
    neg_p = jnp.where(ok_p, 0.0, NEG)
    neg_c = jnp.where(qi - kj >= 0, 0.0, NEG)
    qlane = lax.broadcasted_iota(i32, (tq, LANES), 1) < HEAD_DIM

    for j in range(N_KV_HEADS):
        pair = slice((j // 2) * LANES, (j // 2 + 1) * LANES)
        own_lo = (j % 2 == 0)

        def dup(kpair):
            sw = pltpu.roll(kpair, HEAD_DIM, 1)
            return jnp.where(lo, kpair, sw) if own_lo else jnp.where(lo, sw, kpair)

        def halves(vpair):
            own = jnp.where(lo, vpair, 0.0) if own_lo else jnp.where(lo, 0.0, vpair)
            sw = pltpu.roll(own, HEAD_DIM, 1)
            return (own, sw) if own_lo else (sw, own)

        kk_p = dup(kp_ref[:, pair]).astype(bf16)
        kk_c = dup(kc_ref[:, pair]).astype(bf16)
        v_p = [h.astype(bf16) for h in halves(vp_ref[:, pair])]
        v_c = [h.astype(bf16) for h in halves(vc_ref[:, pair])]

        heads = [(hf, c) for hf in (0, 1) for c in range(3)]
        q_rows = []
        for hf, c in heads:
            ch = slice((3 * j + c) * LANES, (3 * j + c + 1) * LANES)
            qc = q_ref[:, ch].astype(f32)
            q_rows.append(jnp.where(qlane, qc, 0.0) if hf == 0 else jnp.where(qlane, 0.0, qc))
        q_st = jnp.concatenate(q_rows, axis=0).astype(bf16)
        s_p = _nt_dot(q_st, kk_p)
        s_c = _nt_dot(q_st, kk_c)
        p_p, p_c, inv = [], [], []
        for r, (hf, c) in enumerate(heads):
            head = 6 * j + 2 * c + hf
            rows = slice(r * tq, (r + 1) * tq)
            sp = s_p[rows] - _SLOPES[head] * dist_p + neg_p
            sc = s_c[rows] - _SLOPES[head] * dist_c + neg_c
            sink = sink_ref[head]
            m = jnp.maximum(jnp.max(jnp.maximum(sp, sc), axis=-1, keepdims=True), sink)
            ep = jnp.exp(sp - m)
            ec = jnp.exp(sc - m)
            den = jnp.sum(ep + ec, axis=-1, keepdims=True) + jnp.exp(sink - m)
            p_p.append(ep)
            p_c.append(ec)
            inv.append(1.0 / den)
        for c in range(3):
            out = None
            for hf in (0, 1):
                r = hf * 3 + c
                o = (jnp.dot(p_p[r].astype(bf16), v_p[hf], preferred_element_type=f32)
                     + jnp.dot(p_c[r].astype(bf16), v_c[hf], preferred_element_type=f32)) * inv[r]
                out = o if out is None else out + o
            ch = slice((3 * j + c) * LANES, (3 * j + c + 1) * LANES)
            o_ref[:, ch] = out.astype(o_ref.dtype)


def _attn_call(sinks, q, k_arr, v_arr, specs, grid, rows, tq, prompt):
    body = functools.partial(_attn_body, tq=tq, prompt=prompt)
    q_spec, kp_spec, kc_spec, vp_spec, vc_spec, o_spec = specs
    return pl.pallas_call(
        body,
        grid=grid,
        in_specs=[pl.BlockSpec(memory_space=pltpu.SMEM), q_spec, kp_spec, kc_spec, vp_spec, vc_spec],
        out_specs=o_spec,
        out_shape=jax.ShapeDtypeStruct((rows, D_ATTN), bf16),
        compiler_params=_cparams(("arbitrary",) * len(grid)),
        name="attn",
    )(sinks, q, k_arr, k_arr, v_arr, v_arr)


def _hgrn_inputs(q_ref, f_ref, i_ref, lb_ref, sl, *, L, n_valid):
    lb = lb_ref[:, sl]
    fp = f_ref[:, sl]
    q = _silu(q_ref[:, sl])
    t = jnp.exp(-jnp.abs(fp))
    inv = 1.0 / (1.0 + t)
    pos = fp >= 0.0
    logf = jnp.log(lb + (1.0 - lb) * jnp.where(pos, inv, t * inv))
    kk = (1.0 - lb) * jnp.where(pos, t * inv, inv)
    if n_valid < L:
        valid = lax.broadcasted_iota(i32, (L, 1), 0) < n_valid
        logf = jnp.where(valid, logf, 0.0)
        kk = jnp.where(valid, kk, 0.0)
    return q, kk, i_ref[:, sl], logf


def _cum_decay(logf, L):
    row = lax.broadcasted_iota(i32, (L, L), 0)
    col = lax.broadcasted_iota(i32, (L, L), 1)
    tri = (row >= col).astype(bf16)
    h1 = logf.astype(bf16)
    r1 = logf - h1.astype(f32)
    h2 = r1.astype(bf16)
    h3 = (r1 - h2.astype(f32)).astype(bf16)
    return (jnp.dot(tri, h1, preferred_element_type=f32) + jnp.dot(tri, h2, preferred_element_type=f32)
            + jnp.dot(tri, h3, preferred_element_type=f32))


def _sub_ref(b, lo):
    return b[lo - 1:lo] if lo > 0 else jnp.zeros((1, b.shape[1]), f32)


def _factored_operands(q, kk, b, *, L, CF):
    rowid = lax.broadcasted_iota(i32, (L, 1), 0)
    qts, kts = [], []
    for lo in range(0, L, CF):
        r = _sub_ref(b, lo)
        qts.append((q[lo:lo + CF] * jnp.exp(b[lo:lo + CF] - r)).astype(bf16))
        kts.append(jnp.where(rowid < lo + CF, kk * jnp.exp(jnp.minimum(r - b, DECAY_CLAMP)), 0.0).astype(bf16))
    return qts, kts


def _decay_span(b, *, L, CF):
    span = None
    for lo in range(0, L, CF):
        d = _sub_ref(b, lo) - b[lo + CF - 1:lo + CF]
        span = d if span is None else jnp.maximum(span, d)
    return span


def _intra_pairwise(q, kk, v, b, *, L, C):
    vb = v.astype(bf16)
    rowid = lax.broadcasted_iota(i32, (L, 1), 0)
    tloc = lax.broadcasted_iota(i32, (C, 1), 0)
    outs = []
    for lo in range(0, L, C):
        bi, qi, ki, vi = b[lo:lo + C], q[lo:lo + C], kk[lo:lo + C], v[lo:lo + C]
        if lo > 0:
            r = _sub_ref(b, lo)
            qt = (qi * jnp.exp(bi - r)).astype(bf16)
            kt = jnp.where(rowid < lo, kk * jnp.exp(jnp.minimum(r - b, 0.0)), 0.0).astype(bf16)
            oi = jnp.dot(_nt_dot(qt, kt).astype(bf16), vb, preferred_element_type=f32)
        else:
            oi = jnp.zeros((C, LANES), f32)
        for s in range(C):
            e = jnp.exp(jnp.minimum(bi - bi[s:s + 1], 0.0))
            a = jnp.sum(qi * ki[s:s + 1] * e, axis=-1, keepdims=True)
            oi = oi + jnp.where(tloc >= s, a, 0.0) * vi[s:s + 1]
        outs.append(oi)
    return outs[0] if len(outs) == 1 else jnp.concatenate(outs, axis=0)


def _hgrn_body(*refs, L, C, hpb, n_valid, has_state):
    if has_state:
        q_ref, f_ref, i_ref, g_ref, lb_ref, on_ref, s0_ref, og_ref, so_ref, st_ref, inter_ref = refs
    else:
        q_ref, f_ref, i_ref, g_ref, lb_ref, on_ref, og_ref, so_ref, st_ref, inter_ref = refs
    c = pl.program_id(2)
    nc = pl.num_programs(2)
    CF = min(L, FACTORED_BLOCK)

    @pl.when(c == 0)
    def _():
        for hh in range(hpb):
            st_ref[hh] = s0_ref[hh].T if has_state else jnp.zeros((C_DK, C_DK), f32)

    def finish(o, sl):
        o = o * lax.rsqrt(jnp.mean(o * o, axis=-1, keepdims=True) + EPS) * on_ref[...]
        og_ref[:, sl] = (o * _silu(g_ref[:, sl])).astype(og_ref.dtype)

    q, kk, v, logf = _hgrn_inputs(q_ref, f_ref, i_ref, lb_ref, slice(0, hpb * LANES), L=L, n_valid=n_valid)
    b = _cum_decay(logf, L)
    vb = v.astype(bf16)
    q_hat = (q * jnp.exp(b)).astype(bf16)
    bl = b[L - 1:L]
    k_end = (kk * jnp.exp(bl - b)).astype(bf16)
    st_decay = jnp.exp(bl)
    qts, kts = _factored_operands(q, kk, b, L=L, CF=CF)
    span = _decay_span(b, L=L, CF=CF)
    causal = lax.broadcasted_iota(i32, (L, L), 0) >= lax.broadcasted_iota(i32, (L, L), 1)
    for hh in range(hpb):
        sl = slice(hh * LANES, (hh + 1) * LANES)
        st = st_ref[hh]
        o_inter = _nt_dot(q_hat[:, sl], st.astype(bf16))
        st_ref[hh] = st * st_decay[:, sl] + _tn_dot(vb[:, sl], k_end[:, sl])
        inter_ref[hh] = o_inter
        blocks = [_nt_dot(qt[:, sl], kt[:, sl]) for qt, kt in zip(qts, kts)]
        a = blocks[0] if len(blocks) == 1 else jnp.concatenate(blocks, axis=0)
        a = jnp.where(causal, a, 0.0).astype(bf16)
        finish(jnp.dot(a, vb[:, sl], preferred_element_type=f32) + o_inter, sl)

    @pl.when(jnp.max(span) >= DECAY_CLAMP)
    def _():
        for hh in range(hpb):
            sl = slice(hh * LANES, (hh + 1) * LANES)
            q, kk, v, logf = _hgrn_inputs(q_ref, f_ref, i_ref, lb_ref, sl, L=L, n_valid=n_valid)
            b = _cum_decay(logf, L)
            finish(_intra_pairwise(q, kk, v, b, L=L, C=C) + inter_ref[hh], sl)

    @pl.when(c == nc - 1)
    def _():
        for hh in range(hpb):
            so_ref[hh] = st_ref[hh].T


def _hgrn(u, lb, o_norm, s0, *, nb, nc, L, C, hpb, n_valid, rows):
    nh = C_HEADS // hpb
    w = hpb * LANES
    has_state = s0 is not None
    body = functools.partial(_hgrn_body, L=L, C=C, hpb=hpb, n_valid=n_valid, has_state=has_state)

    def uspec(part):
        return pl.BlockSpec((L, w), lambda b, h, c: (b * nc + c, part * nh + h))

    in_specs = [uspec(0), uspec(1), uspec(2), uspec(3),
                pl.BlockSpec((1, w), lambda b, h, c: (0, h)),
                pl.BlockSpec((1, LANES), lambda b, h, c: (0, 0))]
    args = [u, u, u, u, lb.reshape(1, D_C), o_norm.reshape(1, LANES)]
    if has_state:
        in_specs.append(pl.BlockSpec((None, hpb, C_DK, C_DK), lambda b, h, c: (b, h, 0, 0)))
        args.append(s0)
    return pl.pallas_call(
        body,
        grid=(nb, nh, nc),
        in_specs=in_specs,
        out_specs=[pl.BlockSpec((L, w), lambda b, h, c: (b * nc + c, h)),
                   pl.BlockSpec((None, hpb, C_DK, C_DK), lambda b, h, c: (b, h, 0, 0))],
        out_shape=[jax.ShapeDtypeStruct((rows, D_C), bf16),
                   jax.ShapeDtypeStruct((nb, C_HEADS, C_DK, C_DK), f32)],
        scratch_shapes=[pltpu.VMEM((hpb, C_DK, C_DK), f32), pltpu.VMEM((hpb, L, LANES), f32)],
        compiler_params=_cparams(("arbitrary", "arbitrary", "arbitrary")),
        name="hgrn",
    )(*args)


def _moe_pre_body(x_ref, g_ref, scp_ref, shp_ref, scs_ref, shs_ref, wrt_ref, rb_ref,
                  h_ref, idx_ref, wts_ref, rank_ref, cnt_ref, carry):
    i = pl.program_id(0)

    @pl.when(i == 0)
    def _():
        carry[...] = jnp.zeros_like(carry)

    y = _norm_rows(x_ref[...], g_ref[...])

    @pl.when(i < NPT)
    def _():
        h_ref[...] = y * (1.0 + scp_ref[...]) + shp_ref[...]

    @pl.when(i >= NPT)
    def _():
        h_ref[:NS] = y[:NS] * (1.0 + scs_ref[...]) + shs_ref[...]
        h_ref[NS:] = jnp.zeros((TM - NS, D_MODEL), f32)

    h = h_ref[...]

    logits = _nt_dot(wrt_ref[...], h, precision=lax.Precision.HIGHEST)
    scores = jax.nn.sigmoid(logits)
    sel = scores + rb_ref[...]
    best = jnp.zeros((1, TM), i32)
    best_v = None
    for g in range(N_EXPERTS // EXP_PER_GROUP):
        r = [sel[EXP_PER_GROUP * g + t:EXP_PER_GROUP * g + t + 1] for t in range(EXP_PER_GROUP)]
        top2 = None
        for a in range(EXP_PER_GROUP):
            for bb in range(a + 1, EXP_PER_GROUP):
                s2 = r[a] + r[bb]
                top2 = s2 if top2 is None else jnp.maximum(top2, s2)
        if g == 0:
            best_v = top2
        else:
            upd = top2 > best_v
            best = jnp.where(upd, g, best)
            best_v = jnp.where(upd, top2, best_v)
    eidx = lax.broadcasted_iota(i32, (N_EXPERTS, TM), 0)
    masked = jnp.where(jnp.right_shift(eidx, 2) == best, sel, NEG)
    m1 = jnp.max(masked, axis=0, keepdims=True)
    i1 = jnp.min(jnp.where(masked == m1, eidx, N_EXPERTS), axis=0, keepdims=True)
    masked2 = jnp.where(eidx == i1, -jnp.inf, masked)
    m2 = jnp.max(masked2, axis=0, keepdims=True)
    i2 = jnp.min(jnp.where(masked2 == m2, eidx, N_EXPERTS), axis=0, keepdims=True)
    w1 = jnp.sum(jnp.where(eidx == i1, scores, 0.0), axis=0, keepdims=True)
    w2 = jnp.sum(jnp.where(eidx == i2, scores, 0.0), axis=0, keepdims=True)
    den = w1 + w2
    idx_ref[0:1, :] = i1
    idx_ref[1:2, :] = i2
    wts_ref[0:1, :] = w1 / den
    wts_ref[1:2, :] = w2 / den

    colg = i * TM + lax.broadcasted_iota(i32, (1, TM), 1)
    oh = jnp.where(((eidx == i1) | (eidx == i2)) & (colg < N_REAL), 1.0, 0.0)
    rr = lax.broadcasted_iota(i32, (TM, TM), 0)
    cc = lax.broadcasted_iota(i32, (TM, TM), 1)
    before = (rr < cc).astype(bf16)
    rank = jnp.dot(oh.astype(bf16), before, preferred_element_type=f32) + carry[:, 0:1]
    rank_ref[0:1, :] = jnp.sum(jnp.where(eidx == i1, rank, 0.0), axis=0, keepdims=True).astype(i32)
    rank_ref[1:2, :] = jnp.sum(jnp.where(eidx == i2, rank, 0.0), axis=0, keepdims=True).astype(i32)
    carry[...] = carry[...] + jnp.sum(oh, axis=1, keepdims=True)
    cnt_ref[...] = carry[...]


def _moe_pre(x_all, g, modp, mods, layer, router_wt, router_b):
    scp, scs = _mod_specs(layer, 4)
    shp, shs = _mod_specs(layer, 3)
    return pl.pallas_call(
        _moe_pre_body,
        grid=(N_TILES,),
        in_specs=[pl.BlockSpec((TM, D_MODEL), lambda i: (i, 0)),
                  pl.BlockSpec((1, D_MODEL), lambda i: (0, 0)),
                  scp, shp, scs, shs,
                  pl.BlockSpec((N_EXPERTS, D_MODEL), lambda i: (0, 0)),
                  pl.BlockSpec((N_EXPERTS, 1), lambda i: (0, 0))],
        out_specs=[pl.BlockSpec((TM, D_MODEL), lambda i: (i, 0)),
                   pl.BlockSpec((2, TM), lambda i: (0, i)),
                   pl.BlockSpec((2, TM), lambda i: (0, i)),
                   pl.BlockSpec((2, TM), lambda i: (0, i)),
                   pl.BlockSpec((N_EXPERTS, LANES), lambda i: (0, 0))],
        out_shape=[jax.ShapeDtypeStruct((M_ALL, D_MODEL), f32),
                   jax.ShapeDtypeStruct((2, M_ALL), i32),
                   jax.ShapeDtypeStruct((2, M_ALL), f32),
                   jax.ShapeDtypeStruct((2, M_ALL), i32),
                   jax.ShapeDtypeStruct((N_EXPERTS, LANES), f32)],
        scratch_shapes=[pltpu.VMEM((N_EXPERTS, LANES), f32)],
        compiler_params=_cparams(("arbitrary",)),
        name="moe_pre",
    )(x_all, g.reshape(1, D_MODEL), modp, modp, mods, mods, router_wt, router_b.reshape(N_EXPERTS, 1))


def _dispatch_body(pos_ref, fill_lo_ref, fill_hi_ref, nu_ref, h_ref, hs_hbm, zero_buf, sem, tile_sem):
    i = pl.program_id(0)
    base = i * TMD

    def row_copy(src, dst_row):
        return pltpu.make_async_copy(src, hs_hbm.at[pl.ds(dst_row, 1)], sem)

    def wait_row(n, carry):
        row_copy(h_ref.at[pl.ds(0, 1)], 0).wait()
        return carry

    @pl.when(i == 0)
    def _():
        zero_buf[...] = jnp.zeros_like(zero_buf)
        zero_row = zero_buf.at[pl.ds(0, 1)]

        def tile_copy(t):
            return pltpu.make_async_copy(zero_buf, hs_hbm.at[pl.ds(t * TME, TME)], tile_sem)

        def tail_start(t, carry):
            tile_copy(t).start()
            return carry

        def tail_wait(t, carry):
            tile_copy(t).wait()
            return carry

        def fill_expert(e, carry):
            def fill(r, c2):
                row_copy(zero_row, r).start()
                return c2
            lax.fori_loop(fill_lo_ref[e], fill_hi_ref[e], fill, 0)
            lax.fori_loop(fill_lo_ref[e], fill_hi_ref[e], wait_row, 0)
            return carry

        lax.fori_loop(nu_ref[0], NT_E, tail_start, 0)
        lax.fori_loop(0, N_EXPERTS, fill_expert, 0)
        lax.fori_loop(nu_ref[0], NT_E, tail_wait, 0)

    def issue(r, carry):
        src = h_ref.at[pl.ds(r, 1)]
        row_copy(src, pos_ref[base + r]).start()
        row_copy(src, pos_ref[M_ALL + base + r]).start()
        return carry

    lax.fori_loop(0, TMD, issue, 0, unroll=8)
    lax.fori_loop(0, 2 * TMD, wait_row, 0, unroll=8)


def _dispatch(pos_flat, fill_lo, fill_hi, n_used, h3):
    return pl.pallas_call(
        _dispatch_body,
        grid_spec=pltpu.PrefetchScalarGridSpec(
            num_scalar_prefetch=4,
            grid=(N_REAL // TMD,),
            in_specs=[pl.BlockSpec((TMD, D_MODEL), lambda i, *_: (i, 0))],
            out_specs=pl.BlockSpec(memory_space=pl.ANY),
            scratch_shapes=[pltpu.VMEM((TME, D_MODEL), f32), pltpu.SemaphoreType.DMA,
                            pltpu.SemaphoreType.DMA],
        ),
        out_shape=jax.ShapeDtypeStruct((R_MAX, D_MODEL), f32),
        compiler_params=_cparams(("arbitrary",)),
        name="moe_dispatch",
    )(pos_flat, fill_lo, fill_hi, n_used, h3)


def _expert_body(te_ref, nu_ref, hs_ref, w1_ref, w3_ref, w2_ref, y_ref, w1b, w3b, w2b):
    t = pl.program_id(0)

    @pl.when(t < nu_ref[0])
    def _():
        changed = (t == 0) | (te_ref[t] != te_ref[jnp.maximum(t - 1, 0)])

        @pl.when(changed)
        def _():
            w1b[...] = w1_ref[...].astype(bf16)
            w3b[...] = w3_ref[...].astype(bf16)
            w2b[...] = w2_ref[...].astype(bf16)

        hb = hs_ref[...].astype(bf16)
        a = jnp.dot(hb, w1b[...], preferred_element_type=f32)
        b = jnp.dot(hb, w3b[...], preferred_element_type=f32)
        hid = (_silu(a) * b).astype(bf16)
        y_ref[...] = jnp.dot(hid, w2b[...], preferred_element_type=f32)

    @pl.when(t >= nu_ref[0])
    def _():
        y_ref[...] = jnp.zeros_like(y_ref)


def _experts(tile_expert, n_used, hs, w1, w3, w2, layer):
    def row_map(t, te, nu):
        return (jnp.minimum(t, nu[0] - 1), 0)

    def out_map(t, te, nu):
        return (t, 0)

    def w_map(t, te, nu):
        return (layer, te[t], 0, 0)

    return pl.pallas_call(
        _expert_body,
        grid_spec=pltpu.PrefetchScalarGridSpec(
            num_scalar_prefetch=2,
            grid=(NT_E,),
            in_specs=[pl.BlockSpec((TME, D_MODEL), row_map),
                      pl.BlockSpec((None, None, D_MODEL, D_FF), w_map),
                      pl.BlockSpec((None, None, D_MODEL, D_FF), w_map),
                      pl.BlockSpec((None, None, D_FF, D_MODEL), w_map)],
            out_specs=pl.BlockSpec((TME, D_MODEL), out_map),
            scratch_shapes=[pltpu.VMEM((D_MODEL, D_FF), bf16), pltpu.VMEM((D_MODEL, D_FF), bf16),
                            pltpu.VMEM((D_FF, D_MODEL), bf16)],
        ),
        out_shape=jax.ShapeDtypeStruct((R_MAX, D_MODEL), f32),
        compiler_params=_cparams(("arbitrary",)),
        name="moe_experts",
    )(tile_expert, n_used, hs, w1, w3, w2)


def _combine_body(pos_ref, x_ref, w_ref, gp_ref, gs_ref, y_hbm, *refs, split):
    if split:
        yp_ref, ys_ref, buf0, buf1, sems = refs
    else:
        o_ref, buf0, buf1, sems = refs
    i = pl.program_id(0)
    n_real_tiles = N_REAL // TMC

    def copies(tile, r):
        slot = tile % 2
        base = tile * TMC
        return (pltpu.make_async_copy(y_hbm.at[pl.ds(pos_ref[base + r], 1)],
                                      buf0.at[slot, pl.ds(r, 1)], sems.at[slot]),
                pltpu.make_async_copy(y_hbm.at[pl.ds(pos_ref[M_ALL + base + r], 1)],
                                      buf1.at[slot, pl.ds(r, 1)], sems.at[slot]))

    def start_tile(tile):
        def start(r, c):
            for cp in copies(tile, r):
                cp.start()
            return c
        lax.fori_loop(0, TMC, start, 0, unroll=8)

    def wait_tile(tile):
        def wait(r, c):
            for cp in copies(tile, r):
                cp.wait()
            return c
        lax.fori_loop(0, TMC, wait, 0, unroll=8)

    @pl.when(i == 0)
    def _():
        start_tile(0)

    @pl.when(i + 1 < n_real_tiles)
    def _():
        start_tile(i + 1)

    def combined(gate):
        wait_tile(i)
        slot = i % 2
        m = w_ref[:, 0:1] * buf0[slot] + w_ref[:, 1:2] * buf1[slot]
        return x_ref[...] + gate * m

    @pl.when(i < NP // TMC)
    def _():
        (yp_ref if split else o_ref)[...] = combined(gp_ref[...])

    @pl.when((i >= NP // TMC) & (i < n_real_tiles))
    def _():
        (ys_ref if split else o_ref)[...] = combined(gs_ref[...])

    if not split:
        @pl.when(i >= n_real_tiles)
        def _():
            o_ref[...] = jnp.zeros_like(o_ref)


def _combine(pos_flat, x_all, wts_t, modp, mods, layer, y, split):
    tiles_per_seq = SEQ // TMC
    n_prompt_tiles = NP // TMC
    if split:
        grid = (N_REAL // TMC,)
        out_specs = [pl.BlockSpec((TMC, D_MODEL), lambda i, p: (jnp.minimum(i, n_prompt_tiles - 1), 0)),
                     pl.BlockSpec((NS, D_MODEL), lambda i, p: (0, 0))]
        out_shape = [jax.ShapeDtypeStruct((NP, D_MODEL), f32), jax.ShapeDtypeStruct((NS, D_MODEL), f32)]
    else:
        grid = (M_ALL // TMC,)
        out_specs = pl.BlockSpec((TMC, D_MODEL), lambda i, p: (i, 0))
        out_shape = jax.ShapeDtypeStruct((M_ALL, D_MODEL), f32)
    return pl.pallas_call(
        functools.partial(_combine_body, split=split),
        grid_spec=pltpu.PrefetchScalarGridSpec(
            num_scalar_prefetch=1,
            grid=grid,
            in_specs=[pl.BlockSpec((TMC, D_MODEL), lambda i, p: (i, 0)),
                      pl.BlockSpec((TMC, 2), lambda i, p: (i, 0)),
                      pl.BlockSpec((None, None, None, 1, D_MODEL),
                                   lambda i, p: (layer, 5, jnp.minimum(i // tiles_per_seq, BATCH - 1), 0, 0)),
                      pl.BlockSpec((None, None, NS, D_MODEL), lambda i, p: (layer, 5, 0, 0)),
                      pl.BlockSpec(memory_space=pl.ANY)],
            out_specs=out_specs,
            scratch_shapes=[pltpu.VMEM((2, TMC, D_MODEL), f32), pltpu.VMEM((2, TMC, D_MODEL), f32),
                            pltpu.SemaphoreType.DMA((2,))],
        ),
        out_shape=out_shape,
        compiler_params=_cparams(("arbitrary",)),
        name="moe_combine",
    )(pos_flat, x_all, wts_t, modp, mods, y)


def _moe(x_all, g, modp, mods, layer, router_wt, router_b, w1, w3, w2, split=False):
    h3, idx, wts, rank, cnt = _moe_pre(x_all, g, modp, mods, layer, router_wt, router_b)
    counts = cnt[:, 0].astype(i32)
    padded = ((counts + TME - 1) // TME) * TME
    ends = jnp.cumsum(padded)
    offs = ends - padded
    real = (jnp.arange(M_ALL) < N_REAL)[None, :]
    expert_ids = jnp.arange(N_EXPERTS, dtype=i32)[:, None, None]
    seg_start = jnp.sum(jnp.where(idx[None] == expert_ids, offs[:, None, None], 0), axis=0)
    pos = jnp.where(real, seg_start + rank, 0).astype(i32)
    pos_flat = pos.reshape(2 * M_ALL)
    n_used = (ends[-1] // TME).astype(i32).reshape(1)
    tile_start = jnp.minimum(jnp.arange(NT_E, dtype=i32), n_used[0] - 1) * TME
    tile_expert = jnp.sum((tile_start[:, None] >= ends[None, :]).astype(i32), axis=1).astype(i32)
    hs = _dispatch(pos_flat, (offs + counts).astype(i32), ends.astype(i32), n_used, h3)
    y = _experts(tile_expert, n_used, hs, w1, w3, w2, layer)
    return _combine(pos_flat, x_all, wts.T, modp, mods, layer, y, split)


def _sample_rows(a, width):
    return a.reshape(DEC_BATCH, SP, width)[:, :DEC_SEQ].reshape(NS, width)


def kernel(x_prompt, x_sample, c_prompt, c_sample, state_pool, cache_k, cache_v, state_hgrn, norm_g, ada_w, ada_b, w_in_ab, pool_w, pool_scale, q_norm, k_norm, attn_sinks, w_out_ab, w_in_c, lb_param, o_norm, w_out_c, router_w, router_b, moe_w1, moe_w3, moe_w2):
    n_c = BATCH + DEC_BATCH
    c_rows = ((n_c + 7) // 8) * 8
    c_all = jnp.concatenate([c_prompt, c_sample, jnp.zeros((c_rows - n_c, D_MODEL), f32)], axis=0)
    mod = _ada_mod(c_all, ada_w, ada_b).reshape(DEPTH, c_rows, 6, D_MODEL)
    modp = mod[:, :BATCH].transpose(0, 2, 1, 3).reshape(DEPTH, 6, BATCH, 1, D_MODEL)
    mods = jnp.repeat(mod[:, BATCH:n_c].transpose(0, 2, 1, 3), DEC_SEQ, axis=2)

    x_all = jnp.concatenate([x_prompt.reshape(NP, D_MODEL), x_sample.reshape(NS, D_MODEL),
                             jnp.zeros((M_ALL - N_REAL, D_MODEL), f32)], axis=0)
    router_wt = router_w.T

    h = _normmod(x_all, norm_g[0, 0], modp, mods, 0, 0, 1)
    u = _proj(h, w_in_ab[0], 1280)
    tp = 256
    pool_p, qn_p, kn_p = _mix0_pre(
        u, u, lambda i: (jnp.maximum(i * (tp // 16) - 1, 0), 0), NP, tp, SEQ // tp, False,
        pool_w[0], pool_scale[0], q_norm[0], k_norm[0])
    nb = SEQ // WINDOW
    blk = lambda b, n: (b * nb + n, 0)
    prv = lambda b, n: (jnp.maximum(b * nb + n - 1, 0), 0)
    vcol = (D_POOL + D_ATTN + D_KV) // D_KV
    attn_p = _attn_call(
        attn_sinks[0], qn_p, kn_p, u,
        (pl.BlockSpec((WINDOW, D_ATTN), blk), pl.BlockSpec((WINDOW, D_KV), prv), pl.BlockSpec((WINDOW, D_KV), blk),
         pl.BlockSpec((WINDOW, D_KV), lambda b, n: (jnp.maximum(b * nb + n - 1, 0), vcol)),
         pl.BlockSpec((WINDOW, D_KV), lambda b, n: (b * nb + n, vcol)),
         pl.BlockSpec((WINDOW, D_ATTN), blk)),
        (BATCH, nb), NP, WINDOW, True)
    u_s = u[NP:N_REAL].reshape(DEC_BATCH, DEC_SEQ, D_IN_AB)
    u_s16 = jnp.pad(u_s, ((0, 0), (0, SP - DEC_SEQ), (0, 0))).reshape(NSP, D_IN_AB)
    halo_s = jnp.pad(state_pool[0], ((0, 0), (1, 0), (0, 0))).reshape(DEC_BATCH * 16, D_POOL)
    pool_s, qn_s, kn_s = _mix0_pre(
        u_s16, halo_s, lambda i: (i, 0), NSP, SP, 1, True,
        pool_w[0], pool_scale[0], q_norm[0], k_norm[0])
    kn_new = kn_s.reshape(DEC_BATCH, SP, D_KV)[:, :DEC_SEQ]
    v_new = u_s[:, :, D_POOL + D_ATTN + D_KV:]
    zpad = jnp.zeros((DEC_BATCH, WINDOW - DEC_SEQ, D_KV), f32)
    k_ext = jnp.concatenate([cache_k[0].reshape(DEC_BATCH, WINDOW, D_KV), kn_new, zpad], axis=1)
    v_ext = jnp.concatenate([cache_v[0].reshape(DEC_BATCH, WINDOW, D_KV), v_new, zpad], axis=1)
    ext_p = pl.BlockSpec((None, WINDOW, D_KV), lambda b: (b, 0, 0))
    ext_c = pl.BlockSpec((None, WINDOW, D_KV), lambda b: (b, 1, 0))
    attn_s = _attn_call(
        attn_sinks[0], qn_s, k_ext, v_ext,
        (pl.BlockSpec((SP, D_ATTN), lambda b: (b, 0)), ext_p, ext_c, ext_p, ext_c,
         pl.BlockSpec((SP, D_ATTN), lambda b: (b, 0))),
        (DEC_BATCH,), NSP, SP, False)
    x_all = _oproj([pool_p, attn_p], [_sample_rows(pool_s, D_POOL), _sample_rows(attn_s, D_ATTN)],
                   w_out_ab[0], x_all, modp, mods, 0, 2)
    x_all = _moe(x_all, norm_g[0, 1], modp, mods, 0, router_wt, router_b, moe_w1, moe_w3, moe_w2)

    lb_soft = jax.nn.softmax(lb_param.astype(f32), axis=0)
    lower_bounds = jnp.cumsum(lb_soft, axis=0) - lb_soft[0:1]
    h = _normmod(x_all, norm_g[1, 0], modp, mods, 1, 0, 1)
    u1 = _proj(h, w_in_c[0], 1024)
    og_p, hg_p = _hgrn(u1, lower_bounds[1], o_norm[0], None,
                       nb=BATCH, nc=SEQ // 64, L=64, C=16, hpb=8, n_valid=64, rows=NP)
    u1_s = jnp.pad(u1[NP:N_REAL].reshape(DEC_BATCH, DEC_SEQ, 4 * D_C),
                   ((0, 0), (0, SP - DEC_SEQ), (0, 0))).reshape(NSP, 4 * D_C)
    og_s, hg_s = _hgrn(u1_s, lower_bounds[1], o_norm[0], state_hgrn[0],
                       nb=DEC_BATCH, nc=1, L=SP, C=SP, hpb=C_HEADS, n_valid=DEC_SEQ, rows=NSP)
    x_all = _oproj([og_p], [_sample_rows(og_s, D_C)], w_out_c[0], x_all, modp, mods, 1, 2)
    y_p, y_s = _moe(x_all, norm_g[1, 1], modp, mods, 1, router_wt, router_b, moe_w1, moe_w3, moe_w2, split=True)

    y_prompt = y_p.reshape(BATCH, SEQ, D_MODEL)
    y_sample = y_s.reshape(DEC_BATCH, DEC_SEQ, D_MODEL)
    def seq_tail(a, n, c0, c1):
        return jnp.stack([a[(b + 1) * SEQ - n:(b + 1) * SEQ, c0:c1] for b in range(BATCH)])

    pool_prompt = seq_tail(u, POOL_STATE, 0, D_POOL)[None]
    pool_sample = jnp.concatenate([state_pool[0], u_s[:, :, :D_POOL]], axis=1)[:, -POOL_STATE:][None]
    kv_shape = (N_KV_HEADS, HEAD_DIM)
    k_prompt = seq_tail(kn_p, WINDOW, 0, D_KV).reshape(BATCH, WINDOW, *kv_shape)[None]
    v_prompt = seq_tail(u, WINDOW, D_POOL + D_ATTN + D_KV, D_IN_AB).reshape(BATCH, WINDOW, *kv_shape)[None]
    k_sample = jnp.concatenate([cache_k[0], kn_new.reshape(DEC_BATCH, DEC_SEQ, *kv_shape)], axis=1)[:, -WINDOW:][None]
    v_sample = jnp.concatenate([cache_v[0], v_new.reshape(DEC_BATCH, DEC_SEQ, *kv_shape)], axis=1)[:, -WINDOW:][None]
    return (y_prompt, y_sample, pool_prompt, pool_sample, k_prompt, k_sample, v_prompt, v_sample,
            hg_p[None], hg_s[None])
```

```python
import functools
import math

import numpy as np
import jax
import jax.numpy as jnp
from jax import lax
from jax.experimental import pallas as pl
from jax.experimental.pallas import tpu as pltpu

f32 = jnp.float32
bf16 = jnp.bfloat16
i32 = jnp.int32

D_MODEL = 2048
BATCH = 4
SEQ = 2048
DEPTH = 2
DEC_BATCH = 32
DEC_SEQ = 4
EPS = 1e-6
POOL_WINDOWS = (2, 4, 8, 16)
D_POOL = 512
POOL_GROUP = 128
POOL_STATE = 15
HEAD_DIM = 64
N_Q_HEADS = 24
N_KV_HEADS = 4
Q_PER_KV = 6
WINDOW = 128
D_ATTN = 1536
D_KV = 256
D_IN_AB = 2560
C_DK = 128
C_HEADS = 16
D_C = 2048
N_EXPERTS = 16
EXP_PER_GROUP = 4
D_FF = 512
NEG = -1e30

LANES = 128
VMEM_LIMIT = 56 * 1024 * 1024

NP = BATCH * SEQ
NS = DEC_BATCH * DEC_SEQ
N_REAL = NP + NS
TM = 512
M_ALL = ((N_REAL + TM - 1) // TM) * TM
N_TILES = M_ALL // TM
NPT = NP // TM
TPB = SEQ // TM
SP = 16
NSP = DEC_BATCH * SP

TME = 256
R_MAX = ((2 * N_REAL + N_EXPERTS * (TME - 1)) // TME + 1) * TME
NT_E = R_MAX // TME
TMC = 128

FACTORED_BLOCK = 32
DECAY_CLAMP = 60.0


def _cparams(sem, vmem=VMEM_LIMIT):
    return pltpu.CompilerParams(dimension_semantics=sem, vmem_limit_bytes=vmem)


def _silu(x):
    return x * jax.nn.sigmoid(x)


def _nt_dot(a, b, **kw):
    return lax.dot_general(a, b, (((1,), (1,)), ((), ())), preferred_element_type=f32, **kw)


def _tn_dot(a, b):
    return lax.dot_general(a, b, (((0,), (0,)), ((), ())), preferred_element_type=f32)


def _ada_body(c_ref, w_ref, b_ref, o_ref):
    s = _silu(c_ref[...]).astype(bf16)
    o_ref[...] = jnp.dot(s, w_ref[...].astype(bf16), preferred_element_type=f32) + b_ref[...]


def _ada_mod(c_all, ada_w, ada_b):
    depth, d, e = ada_w.shape
    cr = c_all.shape[0]
    tn = 1024
    return pl.pallas_call(
        _ada_body,
        grid=(depth, e // tn),
        in_specs=[
            pl.BlockSpec((cr, d), lambda l, j: (0, 0)),
            pl.BlockSpec((None, d, tn), lambda l, j: (l, 0, j)),
            pl.BlockSpec((None, 1, tn), lambda l, j: (l, 0, j)),
        ],
        out_specs=pl.BlockSpec((None, cr, tn), lambda l, j: (l, 0, j)),
        out_shape=jax.ShapeDtypeStruct((depth, cr, e), f32),
        compiler_params=_cparams(("arbitrary", "arbitrary")),
        name="ada_mod",
    )(c_all, ada_w, ada_b.reshape(depth, 1, e))


def _mod_specs(layer, k):
    p = pl.BlockSpec((None, None, None, 1, D_MODEL),
                     lambda i: (layer, k, jnp.minimum(i // TPB, BATCH - 1), 0, 0))
    s = pl.BlockSpec((None, None, NS, D_MODEL), lambda i: (layer, k, 0, 0))
    return p, s


def _norm_rows(x, g):
    return x * lax.rsqrt(jnp.mean(x * x, axis=-1, keepdims=True) + EPS) * g


def _normmod_body(x_ref, g_ref, scp_ref, shp_ref, scs_ref, shs_ref, h_ref):
    i = pl.program_id(0)
    y = _norm_rows(x_ref[...], g_ref[...])

    @pl.when(i < NPT)
    def _():
        h_ref[...] = (y * (1.0 + scp_ref[...]) + shp_ref[...]).astype(h_ref.dtype)

    @pl.when(i >= NPT)
    def _():
        h_ref[:NS] = (y[:NS] * (1.0 + scs_ref[...]) + shs_ref[...]).astype(h_ref.dtype)
        h_ref[NS:] = jnp.zeros((TM - NS, D_MODEL), h_ref.dtype)


def _normmod(x_all, g, modp, mods, layer, k_shift, k_scale):
    scp, scs = _mod_specs(layer, k_scale)
    shp, shs = _mod_specs(layer, k_shift)
    return pl.pallas_call(
        _normmod_body,
        grid=(N_TILES,),
        in_specs=[pl.BlockSpec((TM, D_MODEL), lambda i: (i, 0)),
                  pl.BlockSpec((1, D_MODEL), lambda i: (0, 0)),
                  scp, shp, scs, shs],
        out_specs=pl.BlockSpec((TM, D_MODEL), lambda i: (i, 0)),
        out_shape=jax.ShapeDtypeStruct((M_ALL, D_MODEL), bf16),
        compiler_params=_cparams(("arbitrary",)),
        name="normmod",
    )(x_all, g.reshape(1, D_MODEL), modp, modp, mods, mods)


def _proj_body(h_ref, w_ref, o_ref, wb_ref):
    @pl.when(pl.program_id(1) == 0)
    def _():
        wb_ref[...] = w_ref[...].astype(bf16)

    o_ref[...] = jnp.dot(h_ref[...], wb_ref[...], preferred_element_type=f32)


def _proj(h_all, w, tn):
    k, e = w.shape
    return pl.pallas_call(
        _proj_body,
        grid=(e // tn, N_TILES),
        in_specs=[pl.BlockSpec((TM, k), lambda j, i: (i, 0)),
                  pl.BlockSpec((k, tn), lambda j, i: (0, j))],
        out_specs=pl.BlockSpec((TM, tn), lambda j, i: (i, j)),
        out_shape=jax.ShapeDtypeStruct((M_ALL, e), f32),
        scratch_shapes=[pltpu.VMEM((k, tn), bf16)],
        compiler_params=_cparams(("arbitrary", "arbitrary")),
        name="proj",
    )(h_all, w)


def _oproj_body(*refs, n_lhs):
    ap_refs = refs[:n_lhs]
    as_refs = refs[n_lhs:2 * n_lhs]
    w_ref, x_ref, gp_ref, gs_ref, o_ref, wb_ref = refs[2 * n_lhs:]
    i = pl.program_id(1)

    @pl.when(i == 0)
    def _():
        wb_ref[...] = w_ref[...].astype(bf16)

    def matmul(a_refs, rows):
        off, acc = 0, None
        for a in a_refs:
            kk = a.shape[1]
            part = jnp.dot(a[:rows], wb_ref[off:off + kk, :], preferred_element_type=f32)
            acc = part if acc is None else acc + part
            off += kk
        return acc

    @pl.when(i < NPT)
    def _():
        o_ref[...] = x_ref[...] + gp_ref[...] * matmul(ap_refs, TM)

    @pl.when(i >= NPT)
    def _():
        o_ref[:NS] = x_ref[:NS] + gs_ref[...] * matmul(as_refs, NS)
        o_ref[NS:] = jnp.zeros((TM - NS, o_ref.shape[1]), f32)


def _oproj(a_prompt, a_sample, w, x_all, modp, mods, layer, k_gate):
    n_lhs = len(a_prompt)
    kdim, e = w.shape
    tn = 1024
    in_specs = []
    for a in a_prompt:
        in_specs.append(pl.BlockSpec((TM, a.shape[1]), lambda j, i: (jnp.minimum(i, NPT - 1), 0)))
    for a in a_sample:
        in_specs.append(pl.BlockSpec((NS, a.shape[1]), lambda j, i: (0, 0)))
    in_specs += [
        pl.BlockSpec((kdim, tn), lambda j, i: (0, j)),
        pl.BlockSpec((TM, tn), lambda j, i: (i, j)),
        pl.BlockSpec((None, None, None, 1, tn),
                     lambda j, i: (layer, k_gate, jnp.minimum(i // TPB, BATCH - 1), 0, j)),
        pl.BlockSpec((None, None, NS, tn), lambda j, i: (layer, k_gate, 0, j)),
    ]
    return pl.pallas_call(
        functools.partial(_oproj_body, n_lhs=n_lhs),
        grid=(e // tn, N_TILES),
        in_specs=in_specs,
        out_specs=pl.BlockSpec((TM, tn), lambda j, i: (i, j)),
        out_shape=jax.ShapeDtypeStruct((M_ALL, e), f32),
        scratch_shapes=[pltpu.VMEM((kdim, tn), bf16)],
        compiler_params=_cparams(("arbitrary", "arbitrary")),
        name="oproj",
    )(*a_prompt, *a_sample, w, x_all, modp, mods)


def _head_norm(x, w_row, scale):
    lane = lax.broadcasted_iota(i32, x.shape, 1)
    lo = lane < HEAD_DIM
    ss = x * x
    m_lo = jnp.sum(jnp.where(lo, ss, 0.0), axis=-1, keepdims=True)
    m_hi = jnp.sum(jnp.where(lo, 0.0, ss), axis=-1, keepdims=True)
    r = jnp.where(lo, lax.rsqrt(m_lo * (1.0 / HEAD_DIM) + EPS), lax.rsqrt(m_hi * (1.0 / HEAD_DIM) + EPS))
    return x * r * (w_row * scale)


def _mix0_pre_body(cur_ref, halo_ref, q0_ref, q1_ref, q2_ref, k_ref, pw_ref, ps_ref, qw_ref, kw_ref,
                   pool_ref, qn_ref, kn_ref, ext_ref, *, tp, tiles_per_seq, full_count):
    i = pl.program_id(0)
    halo = halo_ref[...]
    if not full_count:
        halo = jnp.where(i % tiles_per_seq == 0, 0.0, halo)
    ext_ref[0:16, :] = halo
    ext_ref[16:16 + tp, :] = cur_ref[...]
    t_pos = (i % tiles_per_seq) * tp + lax.broadcasted_iota(i32, (tp, 1), 0)
    for g, w in enumerate(POOL_WINDOWS):
        ch = slice(g * POOL_GROUP, (g + 1) * POOL_GROUP)
        acc = ext_ref[16:16 + tp, ch]
        for dlt in range(1, w):
            acc = acc + ext_ref[16 - dlt:16 - dlt + tp, ch]
        if full_count:
            mean = acc * (1.0 / w)
        else:
            cnt = jnp.minimum(t_pos + 1, w).astype(f32)
            mean = acc / cnt
        d = mean - ext_ref[16:16 + tp, ch]
        y = jnp.dot(d.astype(bf16), pw_ref[g].astype(bf16), preferred_element_type=f32)
        pool_ref[:, ch] = (y * ps_ref[:, ch]).astype(pool_ref.dtype)
    for c3, q_ref in enumerate((q0_ref, q1_ref, q2_ref)):
        for c in range(4):
            sl = slice(c * LANES, (c + 1) * LANES)
            dst = slice((c3 * 4 + c) * LANES, (c3 * 4 + c + 1) * LANES)
            qn_ref[:, dst] = _head_norm(q_ref[:, sl], qw_ref[...], HEAD_DIM ** -0.5).astype(qn_ref.dtype)
    for c in range(2):
        sl = slice(c * LANES, (c + 1) * LANES)
        kn_ref[:, sl] = _head_norm(k_ref[:, sl], kw_ref[...], 1.0)


def _mix0_pre(u, halo_src, halo_map, rows, tp, tiles_per_seq, full_count, pool_w, pool_scale, q_norm, k_norm):
    qw = jnp.tile(q_norm.reshape(1, HEAD_DIM), (1, 2))
    kw = jnp.tile(k_norm.reshape(1, HEAD_DIM), (1, 2))
    body = functools.partial(_mix0_pre_body, tp=tp, tiles_per_seq=tiles_per_seq, full_count=full_count)
    return pl.pallas_call(
        body,
        grid=(rows // tp,),
        in_specs=[
            pl.BlockSpec((tp, D_POOL), lambda i: (i, 0)),
            pl.BlockSpec((16, D_POOL), halo_map),
            pl.BlockSpec((tp, 512), lambda i: (i, 1)),
            pl.BlockSpec((tp, 512), lambda i: (i, 2)),
            pl.BlockSpec((tp, 512), lambda i: (i, 3)),
            pl.BlockSpec((tp, D_KV), lambda i: (i, (D_POOL + D_ATTN) // D_KV)),
            pl.BlockSpec((4, POOL_GROUP, POOL_GROUP), lambda i: (0, 0, 0)),
            pl.BlockSpec((1, D_POOL), lambda i: (0, 0)),
            pl.BlockSpec((1, LANES), lambda i: (0, 0)),
            pl.BlockSpec((1, LANES), lambda i: (0, 0)),
        ],
        out_specs=[
            pl.BlockSpec((tp, D_POOL), lambda i: (i, 0)),
            pl.BlockSpec((tp, D_ATTN), lambda i: (i, 0)),
            pl.BlockSpec((tp, D_KV), lambda i: (i, 0)),
        ],
        out_shape=[
            jax.ShapeDtypeStruct((rows, D_POOL), bf16),
            jax.ShapeDtypeStruct((rows, D_ATTN), bf16),
            jax.ShapeDtypeStruct((rows, D_KV), f32),
        ],
        scratch_shapes=[pltpu.VMEM((16 + tp, D_POOL), f32)],
        compiler_params=_cparams(("arbitrary",)),
        name="mix0_pre",
    )(u, halo_src, u, u, u, u, pool_w, pool_scale.reshape(1, D_POOL), qw, kw)


def _alibi_slopes(n):
    def pow2_slopes(m):
        start = 2.0 ** (-8.0 / m)
        return [start ** (i + 1) for i in range(m)]
    if n & (n - 1) == 0:
        s = pow2_slopes(n)
    else:
        c = 2 ** int(math.floor(math.log2(n)))
        s = pow2_slopes(c) + pow2_slopes(2 * c)[0::2][: n - c]
    return [float(v) for v in np.asarray(s, dtype=np.float32)]


_SLOPES = _alibi_slopes(N_Q_HEADS)


def _attn_body(sink_ref, q_ref, kp_ref, kc_ref, vp_ref, vc_ref, o_ref, *, tq, prompt):
    lane = lax.broadcasted_iota(i32, (WINDOW, LANES), 1)
    lo = lane < HEAD_DIM
    qi = lax.broadcasted_iota(i32, (tq, WINDOW), 0)
    kj = lax.broadcasted_iota(i32, (tq, WINDOW), 1)
    dist_p = (qi + WINDOW - kj).astype(f32)
    dist_c = (qi - kj).astype(f32)
    ok_p = (qi + WINDOW - kj) < WINDOW
    if prompt:
        ok_p = ok_p & (pl.program_id(1) > 0)
    neg_p = jnp.where(ok_p, 0.0, NEG)
    neg_c = jnp.where(qi - kj >= 0, 0.0, NEG)
    qlane = lax.broadcasted_iota(i32, (tq, LANES), 1) < HEAD_DIM

    for j in range(N_KV_HEADS):
        pair = slice((j // 2) * LANES, (j // 2 + 1) * LANES)
        own_lo = (j % 2 == 0)

        def dup(kpair):
            sw = pltpu.roll(kpair, HEAD_DIM, 1)
            return jnp.where(lo, kpair, sw) if own_lo else jnp.where(lo, sw, kpair)

        def halves(vpair):
            own = jnp.where(lo, vpair, 0.0) if own_lo else jnp.where(lo, 0.0, vpair)
            sw = pltpu.roll(own, HEAD_DIM, 1)
            return (own, sw) if own_lo else (sw, own)

        kk_p = dup(kp_ref[:, pair]).astype(bf16)
        kk_c = dup(kc_ref[:, pair]).astype(bf16)
        v_p = [h.astype(bf16) for h in halves(vp_ref[:, pair])]
        v_c = [h.astype(bf16) for h in halves(vc_ref[:, pair])]

        heads = [(hf, c) for hf in (0, 1) for c in range(3)]
        q_rows = []
        for hf, c in heads:
            ch = slice((3 * j + c) * LANES, (3 * j + c + 1) * LANES)
            qc = q_ref[:, ch].astype(f32)
            q_rows.append(jnp.where(qlane, qc, 0.0) if hf == 0 else jnp.where(qlane, 0.0, qc))
        q_st = jnp.concatenate(q_rows, axis=0).astype(bf16)
        s_p = _nt_dot(q_st, kk_p)
        s_c = _nt_dot(q_st, kk_c)
        p_p, p_c, inv = [], [], []
        for r, (hf, c) in enumerate(heads):
            head = 6 * j + 2 * c + hf
            rows = slice(r * tq, (r + 1) * tq)
            sp = s_p[rows] - _SLOPES[head] * dist_p + neg_p
            sc = s_c[rows] - _SLOPES[head] * dist_c + neg_c
            sink = sink_ref[head]
            m = jnp.maximum(jnp.max(jnp.maximum(sp, sc), axis=-1, keepdims=True), sink)
            ep = jnp.exp(sp - m)
            ec = jnp.exp(sc - m)
            den = jnp.sum(ep + ec, axis=-1, keepdims=True) + jnp.exp(sink - m)
            p_p.append(ep)
            p_c.append(ec)
            inv.append(1.0 / den)
        for c in range(3):
            out = None
            for hf in (0, 1):
                r = hf * 3 + c
                o = (jnp.dot(p_p[r].astype(bf16), v_p[hf], preferred_element_type=f32)
                     + jnp.dot(p_c[r].astype(bf16), v_c[hf], preferred_element_type=f32)) * inv[r]
                out = o if out is None else out + o
            ch = slice((3 * j + c) * LANES, (3 * j + c + 1) * LANES)
            o_ref[:, ch] = out.astype(o_ref.dtype)


def _attn_call(sinks, q, k_arr, v_arr, specs, grid, rows, tq, prompt):
    body = functools.partial(_attn_body, tq=tq, prompt=prompt)
    q_spec, kp_spec, kc_spec, vp_spec, vc_spec, o_spec = specs
    return pl.pallas_call(
        body,
        grid=grid,
        in_specs=[pl.BlockSpec(memory_space=pltpu.SMEM), q_spec, kp_spec, kc_spec, vp_spec, vc_spec],
        out_specs=o_spec,
        out_shape=jax.ShapeDtypeStruct((rows, D_ATTN), bf16),
        compiler_params=_cparams(("arbitrary",) * len(grid)),
        name="attn",
    )(sinks, q, k_arr, k_arr, v_arr, v_arr)


def _hgrn_inputs(q_ref, f_ref, i_ref, lb_ref, sl, *, L, n_valid):
    lb = lb_ref[:, sl]
    fp = f_ref[:, sl]
    q = _silu(q_ref[:, sl])
    t = jnp.exp(-jnp.abs(fp))
    inv = 1.0 / (1.0 + t)
    pos = fp >= 0.0
    logf = jnp.log(lb + (1.0 - lb) * jnp.where(pos, inv, t * inv))
    kk = (1.0 - lb) * jnp.where(pos, t * inv, inv)
    if n_valid < L:
        valid = lax.broadcasted_iota(i32, (L, 1), 0) < n_valid
        logf = jnp.where(valid, logf, 0.0)
        kk = jnp.where(valid, kk, 0.0)
    return q, kk, i_ref[:, sl], logf


def _cum_decay(logf, L):
    row = lax.broadcasted_iota(i32, (L, L), 0)
    col = lax.broadcasted_iota(i32, (L, L), 1)
    tri = (row >= col).astype(bf16)
    h1 = logf.astype(bf16)
    r1 = logf - h1.astype(f32)
    h2 = r1.astype(bf16)
    h3 = (r1 - h2.astype(f32)).astype(bf16)
    return (jnp.dot(tri, h1, preferred_element_type=f32) + jnp.dot(tri, h2, preferred_element_type=f32)
            + jnp.dot(tri, h3, preferred_element_type=f32))


def _sub_ref(b, lo):
    return b[lo - 1:lo] if lo > 0 else jnp.zeros((1, b.shape[1]), f32)


def _factored_operands(q, kk, b, *, L, CF):
    rowid = lax.broadcasted_iota(i32, (L, 1), 0)
    qts, kts = [], []
    for lo in range(0, L, CF):
        r = _sub_ref(b, lo)
        qts.append((q[lo:lo + CF] * jnp.exp(b[lo:lo + CF] - r)).astype(bf16))
        kts.append(jnp.where(rowid < lo + CF, kk * jnp.exp(jnp.minimum(r - b, DECAY_CLAMP)), 0.0).astype(bf16))
    return qts, kts


def _decay_span(b, *, L, CF):
    span = None
    for lo in range(0, L, CF):
        d = _sub_ref(b, lo) - b[lo + CF - 1:lo + CF]
        span = d if span is None else jnp.maximum(span, d)
    return span


def _intra_pairwise(q, kk, v, b, *, L, C):
    vb = v.astype(bf16)
    rowid = lax.broadcasted_iota(i32, (L, 1), 0)
    tloc = lax.broadcasted_iota(i32, (C, 1), 0)
    outs = []
    for lo in range(0, L, C):
        bi, qi, ki, vi = b[lo:lo + C], q[lo:lo + C], kk[lo:lo + C], v[lo:lo + C]
        if lo > 0:
            r = _sub_ref(b, lo)
            qt = (qi * jnp.exp(bi - r)).astype(bf16)
            kt = jnp.where(rowid < lo, kk * jnp.exp(jnp.minimum(r - b, 0.0)), 0.0).astype(bf16)
            oi = jnp.dot(_nt_dot(qt, kt).astype(bf16), vb, preferred_element_type=f32)
        else:
            oi = jnp.zeros((C, LANES), f32)
        for s in range(C):
            e = jnp.exp(jnp.minimum(bi - bi[s:s + 1], 0.0))
            a = jnp.sum(qi * ki[s:s + 1] * e, axis=-1, keepdims=True)
            oi = oi + jnp.where(tloc >= s, a, 0.0) * vi[s:s + 1]
        outs.append(oi)
    return outs[0] if len(outs) == 1 else jnp.concatenate(outs, axis=0)


def _hgrn_body(*refs, L, C, hpb, n_valid, has_state):
    if has_state:
        q_ref, f_ref, i_ref, g_ref, lb_ref, on_ref, s0_ref, og_ref, so_ref, st_ref, inter_ref = refs
    else:
        q_ref, f_ref, i_ref, g_ref, lb_ref, on_ref, og_ref, so_ref, st_ref, inter_ref = refs
    c = pl.program_id(2)
    nc = pl.num_programs(2)
    CF = min(L, FACTORED_BLOCK)

    @pl.when(c == 0)
    def _():
        for hh in range(hpb):
            st_ref[hh] = s0_ref[hh].T if has_state else jnp.zeros((C_DK, C_DK), f32)

    def finish(o, sl):
        o = o * lax.rsqrt(jnp.mean(o * o, axis=-1, keepdims=True) + EPS) * on_ref[...]
        og_ref[:, sl] = (o * _silu(g_ref[:, sl])).astype(og_ref.dtype)

    q, kk, v, logf = _hgrn_inputs(q_ref, f_ref, i_ref, lb_ref, slice(0, hpb * LANES), L=L, n_valid=n_valid)
    b = _cum_decay(logf, L)
    vb = v.astype(bf16)
    q_hat = (q * jnp.exp(b)).astype(bf16)
    bl = b[L - 1:L]
    k_end = (kk * jnp.exp(bl - b)).astype(bf16)
    st_decay = jnp.exp(bl)
    qts, kts = _factored_operands(q, kk, b, L=L, CF=CF)
    span = _decay_span(b, L=L, CF=CF)
    causal = lax.broadcasted_iota(i32, (L, L), 0) >= lax.broadcasted_iota(i32, (L, L), 1)
    for hh in range(hpb):
        sl = slice(hh * LANES, (hh + 1) * LANES)
        st = st_ref[hh]
        o_inter = _nt_dot(q_hat[:, sl], st.astype(bf16))
        st_ref[hh] = st * st_decay[:, sl] + _tn_dot(vb[:, sl], k_end[:, sl])
        inter_ref[hh] = o_inter
        blocks = [_nt_dot(qt[:, sl], kt[:, sl]) for qt, kt in zip(qts, kts)]
        a = blocks[0] if len(blocks) == 1 else jnp.concatenate(blocks, axis=0)
        a = jnp.where(causal, a, 0.0).astype(bf16)
        finish(jnp.dot(a, vb[:, sl], preferred_element_type=f32) + o_inter, sl)

    @pl.when(jnp.max(span) >= DECAY_CLAMP)
    def _():
        for hh in range(hpb):
            sl = slice(hh * LANES, (hh + 1) * LANES)
            q, kk, v, logf = _hgrn_inputs(q_ref, f_ref, i_ref, lb_ref, sl, L=L, n_valid=n_valid)
            b = _cum_decay(logf, L)
            finish(_intra_pairwise(q, kk, v, b, L=L, C=C) + inter_ref[hh], sl)

    @pl.when(c == nc - 1)
    def _():
        for hh in range(hpb):
            so_ref[hh] = st_ref[hh].T


def _hgrn(u, lb, o_norm, s0, *, nb, nc, L, C, hpb, n_valid, rows):
    nh = C_HEADS // hpb
    w = hpb * LANES
    has_state = s0 is not None
    body = functools.partial(_hgrn_body, L=L, C=C, hpb=hpb, n_valid=n_valid, has_state=has_state)

    def uspec(part):
        return pl.BlockSpec((L, w), lambda b, h, c: (b * nc + c, part * nh + h))

    in_specs = [uspec(0), uspec(1), uspec(2), uspec(3),
                pl.BlockSpec((1, w), lambda b, h, c: (0, h)),
                pl.BlockSpec((1, LANES), lambda b, h, c: (0, 0))]
    args = [u, u, u, u, lb.reshape(1, D_C), o_norm.reshape(1, LANES)]
    if has_state:
        in_specs.append(pl.BlockSpec((None, hpb, C_DK, C_DK), lambda b, h, c: (b, h, 0, 0)))
        args.append(s0)
    return pl.pallas_call(
        body,
        grid=(nb, nh, nc),
        in_specs=in_specs,
        out_specs=[pl.BlockSpec((L, w), lambda b, h, c: (b * nc + c, h)),
                   pl.BlockSpec((None, hpb, C_DK, C_DK), lambda b, h, c: (b, h, 0, 0))],
        out_shape=[jax.ShapeDtypeStruct((rows, D_C), bf16),
                   jax.ShapeDtypeStruct((nb, C_HEADS, C_DK, C_DK), f32)],
        scratch_shapes=[pltpu.VMEM((hpb, C_DK, C_DK), f32), pltpu.VMEM((hpb, L, LANES), f32)],
        compiler_params=_cparams(("arbitrary", "arbitrary", "arbitrary")),
        name="hgrn",
    )(*args)


def _moe_pre_body(x_ref, g_ref, scp_ref, shp_ref, scs_ref, shs_ref, wrt_ref, rb_ref,
                  h_ref, idx_ref, wts_ref, rank_ref, cnt_ref, carry):
    i = pl.program_id(0)

    @pl.when(i == 0)
    def _():
        carry[...] = jnp.zeros_like(carry)

    y = _norm_rows(x_ref[...], g_ref[...])

    @pl.when(i < NPT)
    def _():
        h_ref[...] = y * (1.0 + scp_ref[...]) + shp_ref[...]

    @pl.when(i >= NPT)
    def _():
        h_ref[:NS] = y[:NS] * (1.0 + scs_ref[...]) + shs_ref[...]
        h_ref[NS:] = jnp.zeros((TM - NS, D_MODEL), f32)

    h = h_ref[...]

    logits = _nt_dot(wrt_ref[...], h, precision=lax.Precision.HIGHEST)
    scores = jax.nn.sigmoid(logits)
    sel = scores + rb_ref[...]
    best = jnp.zeros((1, TM), i32)
    best_v = None
    for g in range(N_EXPERTS // EXP_PER_GROUP):
        r = [sel[EXP_PER_GROUP * g + t:EXP_PER_GROUP * g + t + 1] for t in range(EXP_PER_GROUP)]
        top2 = None
        for a in range(EXP_PER_GROUP):
            for bb in range(a + 1, EXP_PER_GROUP):
                s2 = r[a] + r[bb]
                top2 = s2 if top2 is None else jnp.maximum(top2, s2)
        if g == 0:
            best_v = top2
        else:
            upd = top2 > best_v
            best = jnp.where(upd, g, best)
            best_v = jnp.where(upd, top2, best_v)
    eidx = lax.broadcasted_iota(i32, (N_EXPERTS, TM), 0)
    masked = jnp.where(jnp.right_shift(eidx, 2) == best, sel, NEG)
    m1 = jnp.max(masked, axis=0, keepdims=True)
    i1 = jnp.min(jnp.where(masked == m1, eidx, N_EXPERTS), axis=0, keepdims=True)
    masked2 = jnp.where(eidx == i1, -jnp.inf, masked)
    m2 = jnp.max(masked2, axis=0, keepdims=True)
    i2 = jnp.min(jnp.where(masked2 == m2, eidx, N_EXPERTS), axis=0, keepdims=True)
    w1 = jnp.sum(jnp.where(eidx == i1, scores, 0.0), axis=0, keepdims=True)
    w2 = jnp.sum(jnp.where(eidx == i2, scores, 0.0), axis=0, keepdims=True)
    den = w1 + w2
    idx_ref[0:1, :] = i1
    idx_ref[1:2, :] = i2
    wts_ref[0:1, :] = w1 / den
    wts_ref[1:2, :] = w2 / den

    colg = i * TM + lax.broadcasted_iota(i32, (1, TM), 1)
    oh = jnp.where(((eidx == i1) | (eidx == i2)) & (colg < N_REAL), 1.0, 0.0)
    rr = lax.broadcasted_iota(i32, (TM, TM), 0)
    cc = lax.broadcasted_iota(i32, (TM, TM), 1)
    before = (rr < cc).astype(bf16)
    rank = jnp.dot(oh.astype(bf16), before, preferred_element_type=f32) + carry[:, 0:1]
    rank_ref[0:1, :] = jnp.sum(jnp.where(eidx == i1, rank, 0.0), axis=0, keepdims=True).astype(i32)
    rank_ref[1:2, :] = jnp.sum(jnp.where(eidx == i2, rank, 0.0), axis=0, keepdims=True).astype(i32)
    carry[...] = carry[...] + jnp.sum(oh, axis=1, keepdims=True)
    cnt_ref[...] = carry[...]


def _moe_pre(x_all, g, modp, mods, layer, router_wt, router_b):
    scp, scs = _mod_specs(layer, 4)
    shp, shs = _mod_specs(layer, 3)
    return pl.pallas_call(
        _moe_pre_body,
        grid=(N_TILES,),
        in_specs=[pl.BlockSpec((TM, D_MODEL), lambda i: (i, 0)),
                  pl.BlockSpec((1, D_MODEL), lambda i: (0, 0)),
                  scp, shp, scs, shs,
                  pl.BlockSpec((N_EXPERTS, D_MODEL), lambda i: (0, 0)),
                  pl.BlockSpec((N_EXPERTS, 1), lambda i: (0, 0))],
        out_specs=[pl.BlockSpec((TM, D_MODEL), lambda i: (i, 0)),
                   pl.BlockSpec((2, TM), lambda i: (0, i)),
                   pl.BlockSpec((2, TM), lambda i: (0, i)),
                   pl.BlockSpec((2, TM), lambda i: (0, i)),
                   pl.BlockSpec((N_EXPERTS, LANES), lambda i: (0, 0))],
        out_shape=[jax.ShapeDtypeStruct((M_ALL, D_MODEL), f32),
                   jax.ShapeDtypeStruct((2, M_ALL), i32),
                   jax.ShapeDtypeStruct((2, M_ALL), f32),
                   jax.ShapeDtypeStruct((2, M_ALL), i32),
                   jax.ShapeDtypeStruct((N_EXPERTS, LANES), f32)],
        scratch_shapes=[pltpu.VMEM((N_EXPERTS, LANES), f32)],
        compiler_params=_cparams(("arbitrary",)),
        name="moe_pre",
    )(x_all, g.reshape(1, D_MODEL), modp, modp, mods, mods, router_wt, router_b.reshape(N_EXPERTS, 1))


def _row_tokens_body(pos_ref, tok_ref):
    def init(r, carry):
        tok_ref[r] = 0
        return carry

    def put(n, carry):
        tok_ref[pos_ref[n]] = n
        tok_ref[pos_ref[M_ALL + n]] = n
        return carry

    lax.fori_loop(0, R_MAX, init, 0, unroll=8)
    lax.fori_loop(0, N_REAL, put, 0, unroll=8)


def _row_tokens(pos_flat):
    return pl.pallas_call(
        _row_tokens_body,
        grid_spec=pltpu.PrefetchScalarGridSpec(
            num_scalar_prefetch=1,
            grid=(1,),
            in_specs=[],
            out_specs=pl.BlockSpec(memory_space=pltpu.SMEM),
        ),
        out_shape=jax.ShapeDtypeStruct((R_MAX,), i32),
        compiler_params=_cparams(("arbitrary",)),
        name="moe_row_tokens",
    )(pos_flat)


def _expert_body(te_ref, nu_ref, tok_ref, h_hbm, w1_ref, w3_ref, w2_ref, y_ref, w1b, w3b, w2b, h_buf, sems):
    t = pl.program_id(0)
    n_used = nu_ref[0]

    def row_copy(tile, r):
        slot = tile % 2
        return pltpu.make_async_copy(h_hbm.at[pl.ds(tok_ref[tile * TME + r], 1)],
                                     h_buf.at[slot, pl.ds(r, 1)], sems.at[slot])

    def start_tile(tile):
        def start(r, carry):
            row_copy(tile, r).start()
            return carry
        lax.fori_loop(0, TME, start, 0, unroll=8)

    def wait_tile(tile):
        def wait(r, carry):
            row_copy(tile, r).wait()
            return carry
        lax.fori_loop(0, TME, wait, 0, unroll=8)

    @pl.when(t == 0)
    def _():
        start_tile(0)

    @pl.when(t + 1 < n_used)
    def _():
        start_tile(t + 1)

    @pl.when(t < n_used)
    def _():
        changed = (t == 0) | (te_ref[t] != te_ref[jnp.maximum(t - 1, 0)])

        @pl.when(changed)
        def _():
            w1b[...] = w1_ref[...].astype(bf16)
            w3b[...] = w3_ref[...].astype(bf16)
            w2b[...] = w2_ref[...].astype(bf16)

        wait_tile(t)
        hb = h_buf[t % 2].astype(bf16)
        a = jnp.dot(hb, w1b[...], preferred_element_type=f32)
        b = jnp.dot(hb, w3b[...], preferred_element_type=f32)
        hid = (_silu(a) * b).astype(bf16)
        y_ref[...] = jnp.dot(hid, w2b[...], preferred_element_type=f32)

    @pl.when(t >= nu_ref[0])
    def _():
        y_ref[...] = jnp.zeros_like(y_ref)


def _experts(tile_expert, n_used, row_tokens, h, w1, w3, w2, layer):
    def out_map(t, te, nu, tok):
        return (t, 0)

    def w_map(t, te, nu, tok):
        return (layer, te[t], 0, 0)

    return pl.pallas_call(
        _expert_body,
        grid_spec=pltpu.PrefetchScalarGridSpec(
            num_scalar_prefetch=3,
            grid=(NT_E,),
            in_specs=[pl.BlockSpec(memory_space=pl.ANY),
                      pl.BlockSpec((None, None, D_MODEL, D_FF), w_map),
                      pl.BlockSpec((None, None, D_MODEL, D_FF), w_map),
                      pl.BlockSpec((None, None, D_FF, D_MODEL), w_map)],
            out_specs=pl.BlockSpec((TME, D_MODEL), out_map),
            scratch_shapes=[pltpu.VMEM((D_MODEL, D_FF), bf16), pltpu.VMEM((D_MODEL, D_FF), bf16),
                            pltpu.VMEM((D_FF, D_MODEL), bf16), pltpu.VMEM((2, TME, D_MODEL), f32),
                            pltpu.SemaphoreType.DMA((2,))],
        ),
        out_shape=jax.ShapeDtypeStruct((R_MAX, D_MODEL), f32),
        compiler_params=_cparams(("arbitrary",)),
        name="moe_experts",
    )(tile_expert, n_used, row_tokens, h, w1, w3, w2)


def _combine_body(pos_ref, x_ref, w_ref, gp_ref, gs_ref, y_hbm, *refs, split):
    if split:
        yp_ref, ys_ref, buf0, buf1, sems = refs
    else:
        o_ref, buf0, buf1, sems = refs
    i = pl.program_id(0)
    n_real_tiles = N_REAL // TMC

    def copies(tile, r):
        slot = tile % 2
        base = tile * TMC
        return (pltpu.make_async_copy(y_hbm.at[pl.ds(pos_ref[base + r], 1)],
                                      buf0.at[slot, pl.ds(r, 1)], sems.at[slot]),
                pltpu.make_async_copy(y_hbm.at[pl.ds(pos_ref[M_ALL + base + r], 1)],
                                      buf1.at[slot, pl.ds(r, 1)], sems.at[slot]))

    def start_tile(tile):
        def start(r, c):
            for cp in copies(tile, r):
                cp.start()
            return c
        lax.fori_loop(0, TMC, start, 0, unroll=8)

    def wait_tile(tile):
        def wait(r, c):
            for cp in copies(tile, r):
                cp.wait()
            return c
        lax.fori_loop(0, TMC, wait, 0, unroll=8)

    @pl.when(i == 0)
    def _():
        start_tile(0)

    @pl.when(i + 1 < n_real_tiles)
    def _():
        start_tile(i + 1)

    def combined(gate):
        wait_tile(i)
        slot = i % 2
        m = w_ref[:, 0:1] * buf0[slot] + w_ref[:, 1:2] * buf1[slot]
        return x_ref[...] + gate * m

    @pl.when(i < NP // TMC)
    def _():
        (yp_ref if split else o_ref)[...] = combined(gp_ref[...])

    @pl.when((i >= NP // TMC) & (i < n_real_tiles))
    def _():
        (ys_ref if split else o_ref)[...] = combined(gs_ref[...])

    if not split:
        @pl.when(i >= n_real_tiles)
        def _():
            o_ref[...] = jnp.zeros_like(o_ref)


def _combine(pos_flat, x_all, wts_t, modp, mods, layer, y, split):
    tiles_per_seq = SEQ // TMC
    n_prompt_tiles = NP // TMC
    if split:
        grid = (N_REAL // TMC,)
        out_specs = [pl.BlockSpec((TMC, D_MODEL), lambda i, p: (jnp.minimum(i, n_prompt_tiles - 1), 0)),
                     pl.BlockSpec((NS, D_MODEL), lambda i, p: (0, 0))]
        out_shape = [jax.ShapeDtypeStruct((NP, D_MODEL), f32), jax.ShapeDtypeStruct((NS, D_MODEL), f32)]
    else:
        grid = (M_ALL // TMC,)
        out_specs = pl.BlockSpec((TMC, D_MODEL), lambda i, p: (i, 0))
        out_shape = jax.ShapeDtypeStruct((M_ALL, D_MODEL), f32)
    return pl.pallas_call(
        functools.partial(_combine_body, split=split),
        grid_spec=pltpu.PrefetchScalarGridSpec(
            num_scalar_prefetch=1,
            grid=grid,
            in_specs=[pl.BlockSpec((TMC, D_MODEL), lambda i, p: (i, 0)),
                      pl.BlockSpec((TMC, 2), lambda i, p: (i, 0)),
                      pl.BlockSpec((None, None, None, 1, D_MODEL),
                                   lambda i, p: (layer, 5, jnp.minimum(i // tiles_per_seq, BATCH - 1), 0, 0)),
                      pl.BlockSpec((None, None, NS, D_MODEL), lambda i, p: (layer, 5, 0, 0)),
                      pl.BlockSpec(memory_space=pl.ANY)],
            out_specs=out_specs,
            scratch_shapes=[pltpu.VMEM((2, TMC, D_MODEL), f32), pltpu.VMEM((2, TMC, D_MODEL), f32),
                            pltpu.SemaphoreType.DMA((2,))],
        ),
        out_shape=out_shape,
        compiler_params=_cparams(("arbitrary",)),
        name="moe_combine",
    )(pos_flat, x_all, wts_t, modp, mods, y)


def _moe(x_all, g, modp, mods, layer, router_wt, router_b, w1, w3, w2, split=False):
    h, idx, wts, rank, cnt = _moe_pre(x_all, g, modp, mods, layer, router_wt, router_b)
    counts = cnt[:, 0].astype(i32)
    padded = ((counts + TME - 1) // TME) * TME
    ends = jnp.cumsum(padded)
    offs = ends - padded
    real = (jnp.arange(M_ALL) < N_REAL)[None, :]
    expert_ids = jnp.arange(N_EXPERTS, dtype=i32)[:, None, None]
    seg_start = jnp.sum(jnp.where(idx[None] == expert_ids, offs[:, None, None], 0), axis=0)
    pos = jnp.where(real, seg_start + rank, 0).astype(i32)
    pos_flat = pos.reshape(2 * M_ALL)
    n_used = (ends[-1] // TME).astype(i32).reshape(1)
    tile_start = jnp.minimum(jnp.arange(NT_E, dtype=i32), n_used[0] - 1) * TME
    tile_expert = jnp.sum((tile_start[:, None] >= ends[None, :]).astype(i32), axis=1).astype(i32)
    y = _experts(tile_expert, n_used, _row_tokens(pos_flat), h, w1, w3, w2, layer)
    return _combine(pos_flat, x_all, wts.T, modp, mods, layer, y, split)


def _sample_rows(a, width):
    return a.reshape(DEC_BATCH, SP, width)[:, :DEC_SEQ].reshape(NS, width)


def kernel(x_prompt, x_sample, c_prompt, c_sample, state_pool, cache_k, cache_v, state_hgrn, norm_g, ada_w, ada_b, w_in_ab, pool_w, pool_scale, q_norm, k_norm, attn_sinks, w_out_ab, w_in_c, lb_param, o_norm, w_out_c, router_w, router_b, moe_w1, moe_w3, moe_w2):
    n_c = BATCH + DEC_BATCH
    c_rows = ((n_c + 7) // 8) * 8
    c_all = jnp.concatenate([c_prompt, c_sample, jnp.zeros((c_rows - n_c, D_MODEL), f32)], axis=0)
    mod = _ada_mod(c_all, ada_w, ada_b).reshape(DEPTH, c_rows, 6, D_MODEL)
    modp = mod[:, :BATCH].transpose(0, 2, 1, 3).reshape(DEPTH, 6, BATCH, 1, D_MODEL)
    mods = jnp.repeat(mod[:, BATCH:n_c].transpose(0, 2, 1, 3), DEC_SEQ, axis=2)

    x_all = jnp.concatenate([x_prompt.reshape(NP, D_MODEL), x_sample.reshape(NS, D_MODEL),
                             jnp.zeros((M_ALL - N_REAL, D_MODEL), f32)], axis=0)
    router_wt = router_w.T

    h = _normmod(x_all, norm_g[0, 0], modp, mods, 0, 0, 1)
    u = _proj(h, w_in_ab[0], 1280)
    tp = 256
    pool_p, qn_p, kn_p = _mix0_pre(
        u, u, lambda i: (jnp.maximum(i * (tp // 16) - 1, 0), 0), NP, tp, SEQ // tp, False,
        pool_w[0], pool_scale[0], q_norm[0], k_norm[0])
    nb = SEQ // WINDOW
    blk = lambda b, n: (b * nb + n, 0)
    prv = lambda b, n: (jnp.maximum(b * nb + n - 1, 0), 0)
    vcol = (D_POOL + D_ATTN + D_KV) // D_KV
    attn_p = _attn_call(
        attn_sinks[0], qn_p, kn_p, u,
        (pl.BlockSpec((WINDOW, D_ATTN), blk), pl.BlockSpec((WINDOW, D_KV), prv), pl.BlockSpec((WINDOW, D_KV), blk),
         pl.BlockSpec((WINDOW, D_KV), lambda b, n: (jnp.maximum(b * nb + n - 1, 0), vcol)),
         pl.BlockSpec((WINDOW, D_KV), lambda b, n: (b * nb + n, vcol)),
         pl.BlockSpec((WINDOW, D_ATTN), blk)),
        (BATCH, nb), NP, WINDOW, True)
    u_s = u[NP:N_REAL].reshape(DEC_BATCH, DEC_SEQ, D_IN_AB)
    u_s16 = jnp.pad(u_s, ((0, 0), (0, SP - DEC_SEQ), (0, 0))).reshape(NSP, D_IN_AB)
    halo_s = jnp.pad(state_pool[0], ((0, 0), (1, 0), (0, 0))).reshape(DEC_BATCH * 16, D_POOL)
    pool_s, qn_s, kn_s = _mix0_pre(
        u_s16, halo_s, lambda i: (i, 0), NSP, SP, 1, True,
        pool_w[0], pool_scale[0], q_norm[0], k_norm[0])
    kn_new = kn_s.reshape(DEC_BATCH, SP, D_KV)[:, :DEC_SEQ]
    v_new = u_s[:, :, D_POOL + D_ATTN + D_KV:]
    zpad = jnp.zeros((DEC_BATCH, WINDOW - DEC_SEQ, D_KV), f32)
    k_ext = jnp.concatenate([cache_k[0].reshape(DEC_BATCH, WINDOW, D_KV), kn_new, zpad], axis=1)
    v_ext = jnp.concatenate([cache_v[0].reshape(DEC_BATCH, WINDOW, D_KV), v_new, zpad], axis=1)
    ext_p = pl.BlockSpec((None, WINDOW, D_KV), lambda b: (b, 0, 0))
    ext_c = pl.BlockSpec((None, WINDOW, D_KV), lambda b: (b, 1, 0))
    attn_s = _attn_call(
        attn_sinks[0], qn_s, k_ext, v_ext,
        (pl.BlockSpec((SP, D_ATTN), lambda b: (b, 0)), ext_p, ext_c, ext_p, ext_c,
         pl.BlockSpec((SP, D_ATTN), lambda b: (b, 0))),
        (DEC_BATCH,), NSP, SP, False)
    x_all = _oproj([pool_p, attn_p], [_sample_rows(pool_s, D_POOL), _sample_rows(attn_s, D_ATTN)],
                   w_out_ab[0], x_all, modp, mods, 0, 2)
    x_all = _moe(x_all, norm_g[0, 1], modp, mods, 0, router_wt, router_b, moe_w1, moe_w3, moe_w2)

    lb_soft = jax.nn.softmax(lb_param.astype(f32), axis=0)
    lower_bounds = jnp.cumsum(lb_soft, axis=0) - lb_soft[0:1]
    h = _normmod(x_all, norm_g[1, 0], modp, mods, 1, 0, 1)
    u1 = _proj(h, w_in_c[0], 1024)
    og_p, hg_p = _hgrn(u1, lower_bounds[1], o_norm[0], None,
                       nb=BATCH, nc=SEQ // 64, L=64, C=16, hpb=8, n_valid=64, rows=NP)
    u1_s = jnp.pad(u1[NP:N_REAL].reshape(DEC_BATCH, DEC_SEQ, 4 * D_C),
                   ((0, 0), (0, SP - DEC_SEQ), (0, 0))).reshape(NSP, 4 * D_C)
    og_s, hg_s = _hgrn(u1_s, lower_bounds[1], o_norm[0], state_hgrn[0],
                       nb=DEC_BATCH, nc=1, L=SP, C=SP, hpb=C_HEADS, n_valid=DEC_SEQ, rows=NSP)
    x_all = _oproj([og_p], [_sample_rows(og_s, D_C)], w_out_c[0], x_all, modp, mods, 1, 2)
    y_p, y_s = _moe(x_all, norm_g[1, 1], modp, mods, 1, router_wt, router_b, moe_w1, moe_w3, moe_w2, split=True)

    y_prompt = y_p.reshape(BATCH, SEQ, D_MODEL)
    y_sample = y_s.reshape(DEC_BATCH, DEC_SEQ, D_MODEL)
    def seq_tail(a, n, c0, c1):
        return jnp.stack([a[(b + 1) * SEQ - n:(b + 1) * SEQ, c0:c1] for b in range(BATCH)])

    pool_prompt = seq_tail(u, POOL_STATE, 0, D_POOL)[None]
    pool_sample = jnp.concatenate([state_pool[0], u_s[:, :, :D_POOL]], axis=1)[:, -POOL_STATE:][None]
    kv_shape = (N_KV_HEADS, HEAD_DIM)
    k_prompt = seq_tail(kn_p, WINDOW, 0, D_KV).reshape(BATCH, WINDOW, *kv_shape)[None]
    v_prompt = seq_tail(u, WINDOW, D_POOL + D_ATTN + D_KV, D_IN_AB).reshape(BATCH, WINDOW, *kv_shape)[None]
    k_sample = jnp.concatenate([cache_k[0], kn_new.reshape(DEC_BATCH, DEC_SEQ, *kv_shape)], axis=1)[:, -WINDOW:][None]
    v_sample = jnp.concatenate([cache_v[0], v_new.reshape(DEC_BATCH, DEC_SEQ, *kv_shape)], axis=1)[:, -WINDOW:][None]
    return (y_prompt, y_sample, pool_prompt, pool_sample, k_prompt, k_sample, v_prompt, v_sample,
            hg_p[None], hg_s[None])
```

```python
import functools
import math

import numpy as np
import jax
import jax.numpy as jnp
from jax import lax
from jax.experimental import pallas as pl
from jax.experimental.pallas import tpu as pltpu

f32 = jnp.float32
bf16 = jnp.bfloat16
i32 = jnp.int32

D_MODEL = 2048
BATCH = 4
SEQ = 2048
DEPTH = 2
DEC_BATCH = 32
DEC_SEQ = 4
EPS = 1e-6
POOL_WINDOWS = (2, 4, 8, 16)
D_POOL = 512
POOL_GROUP = 128
POOL_STATE = 15
HEAD_DIM = 64
N_Q_HEADS = 24
N_KV_HEADS = 4
Q_PER_KV = 6
WINDOW = 128
D_ATTN = 1536
D_KV = 256
D_IN_AB = 2560
C_DK = 128
C_HEADS = 16
D_C = 2048
N_EXPERTS = 16
EXP_PER_GROUP = 4
D_FF = 512
NEG = -1e30

LANES = 128
VMEM_LIMIT = 56 * 1024 * 1024

NP = BATCH * SEQ
NS = DEC_BATCH * DEC_SEQ
N_REAL = NP + NS
TM = 512
M_ALL = ((N_REAL + TM - 1) // TM) * TM
N_TILES = M_ALL // TM
NPT = NP // TM
TPB = SEQ // TM
SP = 16
NSP = DEC_BATCH * SP

TME = 256
R_MAX = ((2 * N_REAL + N_EXPERTS * (TME - 1)) // TME + 1) * TME
NT_E = R_MAX // TME
TMC = 128
TMD = 640

FACTORED_BLOCK = 32
DECAY_CLAMP = 60.0


def _cparams(sem, vmem=VMEM_LIMIT):
    return pltpu.CompilerParams(dimension_semantics=sem, vmem_limit_bytes=vmem)


def _silu(x):
    return x * jax.nn.sigmoid(x)


def _nt_dot(a, b, **kw):
    return lax.dot_general(a, b, (((1,), (1,)), ((), ())), preferred_element_type=f32, **kw)


def _tn_dot(a, b):
    return lax.dot_general(a, b, (((0,), (0,)), ((), ())), preferred_element_type=f32)


def _ada_body(c_ref, w_ref, b_ref, o_ref):
    s = _silu(c_ref[...]).astype(bf16)
    o_ref[...] = jnp.dot(s, w_ref[...].astype(bf16), preferred_element_type=f32) + b_ref[...]


def _ada_mod(c_all, ada_w, ada_b):
    depth, d, e = ada_w.shape
    cr = c_all.shape[0]
    tn = 1024
    return pl.pallas_call(
        _ada_body,
        grid=(depth, e // tn),
        in_specs=[
            pl.BlockSpec((cr, d), lambda l, j: (0, 0)),
            pl.BlockSpec((None, d, tn), lambda l, j: (l, 0, j)),
            pl.BlockSpec((None, 1, tn), lambda l, j: (l, 0, j)),
        ],
        out_specs=pl.BlockSpec((None, cr, tn), lambda l, j: (l, 0, j)),
        out_shape=jax.ShapeDtypeStruct((depth, cr, e), f32),
        compiler_params=_cparams(("arbitrary", "arbitrary")),
        name="ada_mod",
    )(c_all, ada_w, ada_b.reshape(depth, 1, e))


def _mod_specs(layer, k):
    p = pl.BlockSpec((None, None, None, 1, D_MODEL),
                     lambda i: (layer, k, jnp.minimum(i // TPB, BATCH - 1), 0, 0))
    s = pl.BlockSpec((None, None, NS, D_MODEL), lambda i: (layer, k, 0, 0))
    return p, s


def _norm_rows(x, g):
    return x * lax.rsqrt(jnp.mean(x * x, axis=-1, keepdims=True) + EPS) * g


def _x_specs(x, width, col):
    if len(x) == 1:
        return [pl.BlockSpec((TM, width), lambda *g: (g[-1], col(*g)))]
    return [pl.BlockSpec((TM, width), lambda *g: (jnp.minimum(g[-1], NPT - 1), col(*g))),
            pl.BlockSpec((NS, width), lambda *g: (0, col(*g)))]


def _normmod_body(*refs):
    *x_refs, g_ref, scp_ref, shp_ref, scs_ref, shs_ref, h_ref = refs
    i = pl.program_id(0)

    @pl.when(i < NPT)
    def _():
        y = _norm_rows(x_refs[0][...], g_ref[...])
        h_ref[...] = (y * (1.0 + scp_ref[...]) + shp_ref[...]).astype(h_ref.dtype)

    @pl.when(i >= NPT)
    def _():
        y = _norm_rows(x_refs[-1][:NS], g_ref[...])
        h_ref[:NS] = (y * (1.0 + scs_ref[...]) + shs_ref[...]).astype(h_ref.dtype)
        h_ref[NS:] = jnp.zeros((TM - NS, D_MODEL), h_ref.dtype)


def _normmod(x, g, modp, mods, layer, k_shift, k_scale):
    scp, scs = _mod_specs(layer, k_scale)
    shp, shs = _mod_specs(layer, k_shift)
    return pl.pallas_call(
        _normmod_body,
        grid=(N_TILES,),
        in_specs=_x_specs(x, D_MODEL, lambda i: 0) + [pl.BlockSpec((1, D_MODEL), lambda i: (0, 0)),
                                                       scp, shp, scs, shs],
        out_specs=pl.BlockSpec((TM, D_MODEL), lambda i: (i, 0)),
        out_shape=jax.ShapeDtypeStruct((M_ALL, D_MODEL), bf16),
        compiler_params=_cparams(("arbitrary",)),
        name="normmod",
    )(*x, g.reshape(1, D_MODEL), modp, modp, mods, mods)


def _proj_body(h_ref, w_ref, o_ref, wb_ref):
    @pl.when(pl.program_id(1) == 0)
    def _():
        wb_ref[...] = w_ref[...].astype(bf16)

    o_ref[...] = jnp.dot(h_ref[...], wb_ref[...], preferred_element_type=f32)


def _proj(h_all, w, tn):
    k, e = w.shape
    return pl.pallas_call(
        _proj_body,
        grid=(e // tn, N_TILES),
        in_specs=[pl.BlockSpec((TM, k), lambda j, i: (i, 0)),
                  pl.BlockSpec((k, tn), lambda j, i: (0, j))],
        out_specs=pl.BlockSpec((TM, tn), lambda j, i: (i, j)),
        out_shape=jax.ShapeDtypeStruct((M_ALL, e), f32),
        scratch_shapes=[pltpu.VMEM((k, tn), bf16)],
        compiler_params=_cparams(("arbitrary", "arbitrary")),
        name="proj",
    )(h_all, w)


def _oproj_body(*refs, n_lhs):
    ap_refs = refs[:n_lhs]
    as_refs = refs[n_lhs:2 * n_lhs]
    w_ref, *x_refs, gp_ref, gs_ref, o_ref, wb_ref = refs[2 * n_lhs:]
    i = pl.program_id(1)

    @pl.when(i == 0)
    def _():
        wb_ref[...] = w_ref[...].astype(bf16)

    def matmul(a_refs, rows):
        off, acc = 0, None
        for a in a_refs:
            kk = a.shape[1]
            part = jnp.dot(a[:rows], wb_ref[off:off + kk, :], preferred_element_type=f32)
            acc = part if acc is None else acc + part
            off += kk
        return acc

    @pl.when(i < NPT)
    def _():
        o_ref[...] = x_refs[0][...] + gp_ref[...] * matmul(ap_refs, TM)

    @pl.when(i >= NPT)
    def _():
        o_ref[:NS] = x_refs[-1][:NS] + gs_ref[...] * matmul(as_refs, NS)
        o_ref[NS:] = jnp.zeros((TM - NS, o_ref.shape[1]), f32)


def _oproj(a_prompt, a_sample, w, x, modp, mods, layer, k_gate):
    n_lhs = len(a_prompt)
    kdim, e = w.shape
    tn = 1024
    in_specs = []
    for a in a_prompt:
        in_specs.append(pl.BlockSpec((TM, a.shape[1]), lambda j, i: (jnp.minimum(i, NPT - 1), 0)))
    for a in a_sample:
        in_specs.append(pl.BlockSpec((NS, a.shape[1]), lambda j, i: (0, 0)))
    in_specs += [pl.BlockSpec((kdim, tn), lambda j, i: (0, j))] + _x_specs(x, tn, lambda j, i: j) + [
        pl.BlockSpec((None, None, None, 1, tn),
                     lambda j, i: (layer, k_gate, jnp.minimum(i // TPB, BATCH - 1), 0, j)),
        pl.BlockSpec((None, None, NS, tn), lambda j, i: (layer, k_gate, 0, j)),
    ]
    return pl.pallas_call(
        functools.partial(_oproj_body, n_lhs=n_lhs),
        grid=(e // tn, N_TILES),
        in_specs=in_specs,
        out_specs=pl.BlockSpec((TM, tn), lambda j, i: (i, j)),
        out_shape=jax.ShapeDtypeStruct((M_ALL, e), f32),
        scratch_shapes=[pltpu.VMEM((kdim, tn), bf16)],
        compiler_params=_cparams(("arbitrary", "arbitrary")),
        name="oproj",
    )(*a_prompt, *a_sample, w, *x, modp, mods)


def _head_norm(x, w_row, scale):
    lane = lax.broadcasted_iota(i32, x.shape, 1)
    lo = lane < HEAD_DIM
    ss = x * x
    m_lo = jnp.sum(jnp.where(lo, ss, 0.0), axis=-1, keepdims=True)
    m_hi = jnp.sum(jnp.where(lo, 0.0, ss), axis=-1, keepdims=True)
    r = jnp.where(lo, lax.rsqrt(m_lo * (1.0 / HEAD_DIM) + EPS), lax.rsqrt(m_hi * (1.0 / HEAD_DIM) + EPS))
    return x * r * (w_row * scale)


def _mix0_pre_body(cur_ref, halo_ref, q0_ref, q1_ref, q2_ref, k_ref, pw_ref, ps_ref, qw_ref, kw_ref,
                   pool_ref, qn_ref, kn_ref, ext_ref, *, tp, tiles_per_seq, full_count):
    i = pl.program_id(0)
    halo = halo_ref[...]
    if not full_count:
        halo = jnp.where(i % tiles_per_seq == 0, 0.0, halo)
    ext_ref[0:16, :] = halo
    ext_ref[16:16 + tp, :] = cur_ref[...]
    t_pos = (i % tiles_per_seq) * tp + lax.broadcasted_iota(i32, (tp, 1), 0)
    for g, w in enumerate(POOL_WINDOWS):
        ch = slice(g * POOL_GROUP, (g + 1) * POOL_GROUP)
        acc = ext_ref[16:16 + tp, ch]
        for dlt in range(1, w):
            acc = acc + ext_ref[16 - dlt:16 - dlt + tp, ch]
        if full_count:
            mean = acc * (1.0 / w)
        else:
            cnt = jnp.minimum(t_pos + 1, w).astype(f32)
            mean = acc / cnt
        d = mean - ext_ref[16:16 + tp, ch]
        y = jnp.dot(d.astype(bf16), pw_ref[g].astype(bf16), preferred_element_type=f32)
        pool_ref[:, ch] = (y * ps_ref[:, ch]).astype(pool_ref.dtype)
    for c3, q_ref in enumerate((q0_ref, q1_ref, q2_ref)):
        for c in range(4):
            sl = slice(c * LANES, (c + 1) * LANES)
            dst = slice((c3 * 4 + c) * LANES, (c3 * 4 + c + 1) * LANES)
            qn_ref[:, dst] = _head_norm(q_ref[:, sl], qw_ref[...], HEAD_DIM ** -0.5).astype(qn_ref.dtype)
    for c in range(2):
        sl = slice(c * LANES, (c + 1) * LANES)
        kn_ref[:, sl] = _head_norm(k_ref[:, sl], kw_ref[...], 1.0)


def _mix0_pre(u, halo_src, halo_map, rows, tp, tiles_per_seq, full_count, pool_w, pool_scale, q_norm, k_norm):
    qw = jnp.tile(q_norm.reshape(1, HEAD_DIM), (1, 2))
    kw = jnp.tile(k_norm.reshape(1, HEAD_DIM), (1, 2))
    body = functools.partial(_mix0_pre_body, tp=tp, tiles_per_seq=tiles_per_seq, full_count=full_count)
    return pl.pallas_call(
        body,
        grid=(rows // tp,),
        in_specs=[
            pl.BlockSpec((tp, D_POOL), lambda i: (i, 0)),
            pl.BlockSpec((16, D_POOL), halo_map),
            pl.BlockSpec((tp, 512), lambda i: (i, 1)),
            pl.BlockSpec((tp, 512), lambda i: (i, 2)),
            pl.BlockSpec((tp, 512), lambda i: (i, 3)),
            pl.BlockSpec((tp, D_KV), lambda i: (i, (D_POOL + D_ATTN) // D_KV)),
            pl.BlockSpec((4, POOL_GROUP, POOL_GROUP), lambda i: (0, 0, 0)),
            pl.BlockSpec((1, D_POOL), lambda i: (0, 0)),
            pl.BlockSpec((1, LANES), lambda i: (0, 0)),
            pl.BlockSpec((1, LANES), lambda i: (0, 0)),
        ],
        out_specs=[
            pl.BlockSpec((tp, D_POOL), lambda i: (i, 0)),
            pl.BlockSpec((tp, D_ATTN), lambda i: (i, 0)),
            pl.BlockSpec((tp, D_KV), lambda i: (i, 0)),
        ],
        out_shape=[
            jax.ShapeDtypeStruct((rows, D_POOL), bf16),
            jax.ShapeDtypeStruct((rows, D_ATTN), bf16),
            jax.ShapeDtypeStruct((rows, D_KV), f32),
        ],
        scratch_shapes=[pltpu.VMEM((16 + tp, D_POOL), f32)],
        compiler_params=_cparams(("arbitrary",)),
        name="mix0_pre",
    )(u, halo_src, u, u, u, u, pool_w, pool_scale.reshape(1, D_POOL), qw, kw)


def _alibi_slopes(n):
    def pow2_slopes(m):
        start = 2.0 ** (-8.0 / m)
        return [start ** (i + 1) for i in range(m)]
    if n & (n - 1) == 0:
        s = pow2_slopes(n)
    else:
        c = 2 ** int(math.floor(math.log2(n)))
        s = pow2_slopes(c) + pow2_slopes(2 * c)[0::2][: n - c]
    return [float(v) for v in np.asarray(s, dtype=np.float32)]


_SLOPES = _alibi_slopes(N_Q_HEADS)


def _attn_body(sink_ref, q_ref, kp_ref, kc_ref, vp_ref, vc_ref, o_ref, *, tq, prompt):
    lane = lax.broadcasted_iota(i32, (WINDOW, LANES), 1)
    lo = lane < HEAD_DIM
    qi = lax.broadcasted_iota(i32, (tq, WINDOW), 0)
    kj = lax.broadcasted_iota(i32, (tq, WINDOW), 1)
    dist_p = (qi + WINDOW - kj).astype(f32)
    dist_c = (qi - kj).astype(f32)
    ok_p = (qi + WINDOW - kj) < WINDOW
    if prompt:
        ok_p = ok_p & (pl.program_id(1) > 0)
    neg_p = jnp.where(ok_p, 0.0, NEG)
    neg_c = jnp.where(qi - kj >= 0, 0.0, NEG)
    qlane = lax.broadcasted_iota(i32, (tq, LANES), 1) < HEAD_DIM

    for j in range(N_KV_HEADS):
        pair = slice((j // 2) * LANES, (j // 2 + 1) * LANES)
        own_lo = (j % 2 == 0)

        def dup(kpair):
            sw = pltpu.roll(kpair, HEAD_DIM, 1)
            return jnp.where(lo, kpair, sw) if own_lo else jnp.where(lo, sw, kpair)

        def halves(vpair):
            own = jnp.where(lo, vpair, 0.0) if own_lo else jnp.where(lo, 0.0, vpair)
            sw = pltpu.roll(own, HEAD_DIM, 1)
            return (own, sw) if own_lo else (sw, own)

        kk_p = dup(kp_ref[:, pair]).astype(bf16)
        kk_c = dup(kc_ref[:, pair]).astype(bf16)
        v_p = [h.astype(bf16) for h in halves(vp_ref[:, pair])]
        v_c = [h.astype(bf16) for h in halves(vc_ref[:, pair])]

        heads = [(hf, c) for hf in (0, 1) for c in range(3)]
        q_rows = []
        for hf, c in heads:
            ch = slice((3 * j + c) * LANES, (3 * j + c + 1) * LANES)
            qc = q_ref[:, ch].astype(f32)
            q_rows.append(jnp.where(qlane, qc, 0.0) if hf == 0 else jnp.where(qlane, 0.0, qc))
        q_st = jnp.concatenate(q_rows, axis=0).astype(bf16)
        s_p = _nt_dot(q_st, kk_p)
        s_c = _nt_dot(q_st, kk_c)
        p_p, p_c, inv = [], [], []
        for r, (hf, c) in enumerate(heads):
            head = 6 * j + 2 * c + hf
            rows = slice(r * tq, (r + 1) * tq)
            sp = s_p[rows] - _SLOPES[head] * dist_p + neg_p
            sc = s_c[rows] - _SLOPES[head] * dist_c + neg_c
            sink = sink_ref[head]
            m = jnp.maximum(jnp.max(jnp.maximum(sp, sc), axis=-1, keepdims=True), sink)
            ep = jnp.exp(sp - m)
            ec = jnp.exp(sc - m)
            den = jnp.sum(ep + ec, axis=-1, keepdims=True) + jnp.exp(sink - m)
            p_p.append(ep)
            p_c.append(ec)
            inv.append(1.0 / den)
        for c in range(3):
            out = None
            for hf in (0, 1):
                r = hf * 3 + c
                o = (jnp.dot(p_p[r].astype(bf16), v_p[hf], preferred_element_type=f32)
                     + jnp.dot(p_c[r].astype(bf16), v_c[hf], preferred_element_type=f32)) * inv[r]
                out = o if out is None else out + o
            ch = slice((3 * j + c) * LANES, (3 * j + c + 1) * LANES)
            o_ref[:, ch] = out.astype(o_ref.dtype)


def _attn_call(sinks, q, k_arr, v_arr, specs, grid, rows, tq, prompt):
    body = functools.partial(_attn_body, tq=tq, prompt=prompt)
    q_spec, kp_spec, kc_spec, vp_spec, vc_spec, o_spec = specs
    return pl.pallas_call(
        body,
        grid=grid,
        in_specs=[pl.BlockSpec(memory_space=pltpu.SMEM), q_spec, kp_spec, kc_spec, vp_spec, vc_spec],
        out_specs=o_spec,
        out_shape=jax.ShapeDtypeStruct((rows, D_ATTN), bf16),
        compiler_params=_cparams(("arbitrary",) * len(grid)),
        name="attn",
    )(sinks, q, k_arr, k_arr, v_arr, v_arr)


def _hgrn_inputs(q_ref, f_ref, i_ref, lb_ref, sl, *, L, n_valid):
    lb = lb_ref[:, sl]
    fp = f_ref[:, sl]
    q = _silu(q_ref[:, sl])
    t = jnp.exp(-jnp.abs(fp))
    inv = 1.0 / (1.0 + t)
    pos = fp >= 0.0
    logf = jnp.log(lb + (1.0 - lb) * jnp.where(pos, inv, t * inv))
    kk = (1.0 - lb) * jnp.where(pos, t * inv, inv)
    if n_valid < L:
        valid = lax.broadcasted_iota(i32, (L, 1), 0) < n_valid
        logf = jnp.where(valid, logf, 0.0)
        kk = jnp.where(valid, kk, 0.0)
    return q, kk, i_ref[:, sl], logf


def _cum_decay(logf, L):
    row = lax.broadcasted_iota(i32, (L, L), 0)
    col = lax.broadcasted_iota(i32, (L, L), 1)
    tri = (row >= col).astype(bf16)
    h1 = logf.astype(bf16)
    r1 = logf - h1.astype(f32)
    h2 = r1.astype(bf16)
    h3 = (r1 - h2.astype(f32)).astype(bf16)
    return (jnp.dot(tri, h1, preferred_element_type=f32) + jnp.dot(tri, h2, preferred_element_type=f32)
            + jnp.dot(tri, h3, preferred_element_type=f32))


def _sub_ref(b, lo):
    return b[lo - 1:lo] if lo > 0 else jnp.zeros((1, b.shape[1]), f32)


def _factored_operands(q, kk, b, *, L, CF):
    rowid = lax.broadcasted_iota(i32, (L, 1), 0)
    qts, kts = [], []
    for lo in range(0, L, CF):
        r = _sub_ref(b, lo)
        qts.append((q[lo:lo + CF] * jnp.exp(b[lo:lo + CF] - r)).astype(bf16))
        kts.append(jnp.where(rowid < lo + CF, kk * jnp.exp(jnp.minimum(r - b, DECAY_CLAMP)), 0.0).astype(bf16))
    return qts, kts


def _decay_span(b, *, L, CF):
    span = None
    for lo in range(0, L, CF):
        d = _sub_ref(b, lo) - b[lo + CF - 1:lo + CF]
        span = d if span is None else jnp.maximum(span, d)
    return span


def _intra_pairwise(q, kk, v, b, *, L, C):
    vb = v.astype(bf16)
    rowid = lax.broadcasted_iota(i32, (L, 1), 0)
    tloc = lax.broadcasted_iota(i32, (C, 1), 0)
    outs = []
    for lo in range(0, L, C):
        bi, qi, ki, vi = b[lo:lo + C], q[lo:lo + C], kk[lo:lo + C], v[lo:lo + C]
        if lo > 0:
            r = _sub_ref(b, lo)
            qt = (qi * jnp.exp(bi - r)).astype(bf16)
            kt = jnp.where(rowid < lo, kk * jnp.exp(jnp.minimum(r - b, 0.0)), 0.0).astype(bf16)
            oi = jnp.dot(_nt_dot(qt, kt).astype(bf16), vb, preferred_element_type=f32)
        else:
            oi = jnp.zeros((C, LANES), f32)
        for s in range(C):
            e = jnp.exp(jnp.minimum(bi - bi[s:s + 1], 0.0))
            a = jnp.sum(qi * ki[s:s + 1] * e, axis=-1, keepdims=True)
            oi = oi + jnp.where(tloc >= s, a, 0.0) * vi[s:s + 1]
        outs.append(oi)
    return outs[0] if len(outs) == 1 else jnp.concatenate(outs, axis=0)


def _hgrn_body(*refs, L, C, hpb, n_valid, has_state):
    if has_state:
        q_ref, f_ref, i_ref, g_ref, lb_ref, on_ref, s0_ref, og_ref, so_ref, st_ref, inter_ref = refs
    else:
        q_ref, f_ref, i_ref, g_ref, lb_ref, on_ref, og_ref, so_ref, st_ref, inter_ref = refs
    c = pl.program_id(2)
    nc = pl.num_programs(2)
    CF = min(L, FACTORED_BLOCK)

    @pl.when(c == 0)
    def _():
        for hh in range(hpb):
            st_ref[hh] = s0_ref[hh].T if has_state else jnp.zeros((C_DK, C_DK), f32)

    def finish(o, sl):
        o = o * lax.rsqrt(jnp.mean(o * o, axis=-1, keepdims=True) + EPS) * on_ref[...]
        og_ref[:, sl] = (o * _silu(g_ref[:, sl])).astype(og_ref.dtype)

    q, kk, v, logf = _hgrn_inputs(q_ref, f_ref, i_ref, lb_ref, slice(0, hpb * LANES), L=L, n_valid=n_valid)
    b = _cum_decay(logf, L)
    vb = v.astype(bf16)
    q_hat = (q * jnp.exp(b)).astype(bf16)
    bl = b[L - 1:L]
    k_end = (kk * jnp.exp(bl - b)).astype(bf16)
    st_decay = jnp.exp(bl)
    qts, kts = _factored_operands(q, kk, b, L=L, CF=CF)
    span = _decay_span(b, L=L, CF=CF)
    causal = lax.broadcasted_iota(i32, (L, L), 0) >= lax.broadcasted_iota(i32, (L, L), 1)
    for hh in range(hpb):
        sl = slice(hh * LANES, (hh + 1) * LANES)
        st = st_ref[hh]
        o_inter = _nt_dot(q_hat[:, sl], st.astype(bf16))
        st_ref[hh] = st * st_decay[:, sl] + _tn_dot(vb[:, sl], k_end[:, sl])
        inter_ref[hh] = o_inter
        blocks = [_nt_dot(qt[:, sl], kt[:, sl]) for qt, kt in zip(qts, kts)]
        a = blocks[0] if len(blocks) == 1 else jnp.concatenate(blocks, axis=0)
        a = jnp.where(causal, a, 0.0).astype(bf16)
        finish(jnp.dot(a, vb[:, sl], preferred_element_type=f32) + o_inter, sl)

    @pl.when(jnp.max(span) >= DECAY_CLAMP)
    def _():
        for hh in range(hpb):
            sl = slice(hh * LANES, (hh + 1) * LANES)
            q, kk, v, logf = _hgrn_inputs(q_ref, f_ref, i_ref, lb_ref, sl, L=L, n_valid=n_valid)
            b = _cum_decay(logf, L)
            finish(_intra_pairwise(q, kk, v, b, L=L, C=C) + inter_ref[hh], sl)

    @pl.when(c == nc - 1)
    def _():
        for hh in range(hpb):
            so_ref[hh] = st_ref[hh].T


def _hgrn(u, lb, o_norm, s0, *, nb, nc, L, C, hpb, n_valid, rows):
    nh = C_HEADS // hpb
    w = hpb * LANES
    has_state = s0 is not None
    body = functools.partial(_hgrn_body, L=L, C=C, hpb=hpb, n_valid=n_valid, has_state=has_state)

    def uspec(part):
        return pl.BlockSpec((L, w), lambda b, h, c: (b * nc + c, part * nh + h))

    in_specs = [uspec(0), uspec(1), uspec(2), uspec(3),
                pl.BlockSpec((1, w), lambda b, h, c: (0, h)),
                pl.BlockSpec((1, LANES), lambda b, h, c: (0, 0))]
    args = [u, u, u, u, lb.reshape(1, D_C), o_norm.reshape(1, LANES)]
    if has_state:
        in_specs.append(pl.BlockSpec((None, hpb, C_DK, C_DK), lambda b, h, c: (b, h, 0, 0)))
        args.append(s0)
    return pl.pallas_call(
        body,
        grid=(nb, nh, nc),
        in_specs=in_specs,
        out_specs=[pl.BlockSpec((L, w), lambda b, h, c: (b * nc + c, h)),
                   pl.BlockSpec((None, hpb, C_DK, C_DK), lambda b, h, c: (b, h, 0, 0))],
        out_shape=[jax.ShapeDtypeStruct((rows, D_C), bf16),
                   jax.ShapeDtypeStruct((nb, C_HEADS, C_DK, C_DK), f32)],
        scratch_shapes=[pltpu.VMEM((hpb, C_DK, C_DK), f32), pltpu.VMEM((hpb, L, LANES), f32)],
        compiler_params=_cparams(("arbitrary", "arbitrary", "arbitrary")),
        name="hgrn",
    )(*args)


def _moe_pre_body(x_ref, g_ref, scp_ref, shp_ref, scs_ref, shs_ref, wrt_ref, rb_ref,
                  h_ref, idx_ref, wts_ref, rank_ref, cnt_ref, carry):
    i = pl.program_id(0)

    @pl.when(i == 0)
    def _():
        carry[...] = jnp.zeros_like(carry)

    y = _norm_rows(x_ref[...], g_ref[...])

    @pl.when(i < NPT)
    def _():
        h_ref[...] = y * (1.0 + scp_ref[...]) + shp_ref[...]

    @pl.when(i >= NPT)
    def _():
        h_ref[:NS] = y[:NS] * (1.0 + scs_ref[...]) + shs_ref[...]
        h_ref[NS:] = jnp.zeros((TM - NS, D_MODEL), f32)

    h = h_ref[...]

    logits = _nt_dot(wrt_ref[...], h, precision=lax.Precision.HIGHEST)
    scores = jax.nn.sigmoid(logits)
    sel = scores + rb_ref[...]
    best = jnp.zeros((1, TM), i32)
    best_v = None
    for g in range(N_EXPERTS // EXP_PER_GROUP):
        r = [sel[EXP_PER_GROUP * g + t:EXP_PER_GROUP * g + t + 1] for t in range(EXP_PER_GROUP)]
        top2 = None
        for a in range(EXP_PER_GROUP):
            for bb in range(a + 1, EXP_PER_GROUP):
                s2 = r[a] + r[bb]
                top2 = s2 if top2 is None else jnp.maximum(top2, s2)
        if g == 0:
            best_v = top2
        else:
            upd = top2 > best_v
            best = jnp.where(upd, g, best)
            best_v = jnp.where(upd, top2, best_v)
    eidx = lax.broadcasted_iota(i32, (N_EXPERTS, TM), 0)
    masked = jnp.where(jnp.right_shift(eidx, 2) == best, sel, NEG)
    m1 = jnp.max(masked, axis=0, keepdims=True)
    i1 = jnp.min(jnp.where(masked == m1, eidx, N_EXPERTS), axis=0, keepdims=True)
    masked2 = jnp.where(eidx == i1, -jnp.inf, masked)
    m2 = jnp.max(masked2, axis=0, keepdims=True)
    i2 = jnp.min(jnp.where(masked2 == m2, eidx, N_EXPERTS), axis=0, keepdims=True)
    w1 = jnp.sum(jnp.where(eidx == i1, scores, 0.0), axis=0, keepdims=True)
    w2 = jnp.sum(jnp.where(eidx == i2, scores, 0.0), axis=0, keepdims=True)
    den = w1 + w2
    idx_ref[0:1, :] = i1
    idx_ref[1:2, :] = i2
    wts_ref[0:1, :] = w1 / den
    wts_ref[1:2, :] = w2 / den

    colg = i * TM + lax.broadcasted_iota(i32, (1, TM), 1)
    oh = jnp.where(((eidx == i1) | (eidx == i2)) & (colg < N_REAL), 1.0, 0.0)
    rr = lax.broadcasted_iota(i32, (TM, TM), 0)
    cc = lax.broadcasted_iota(i32, (TM, TM), 1)
    before = (rr < cc).astype(bf16)
    rank = jnp.dot(oh.astype(bf16), before, preferred_element_type=f32) + carry[:, 0:1]
    rank_ref[0:1, :] = jnp.sum(jnp.where(eidx == i1, rank, 0.0), axis=0, keepdims=True).astype(i32)
    rank_ref[1:2, :] = jnp.sum(jnp.where(eidx == i2, rank, 0.0), axis=0, keepdims=True).astype(i32)
    carry[...] = carry[...] + jnp.sum(oh, axis=1, keepdims=True)
    cnt_ref[...] = carry[...]


def _moe_pre(x_all, g, modp, mods, layer, router_wt, router_b):
    scp, scs = _mod_specs(layer, 4)
    shp, shs = _mod_specs(layer, 3)
    return pl.pallas_call(
        _moe_pre_body,
        grid=(N_TILES,),
        in_specs=[pl.BlockSpec((TM, D_MODEL), lambda i: (i, 0)),
                  pl.BlockSpec((1, D_MODEL), lambda i: (0, 0)),
                  scp, shp, scs, shs,
                  pl.BlockSpec((N_EXPERTS, D_MODEL), lambda i: (0, 0)),
                  pl.BlockSpec((N_EXPERTS, 1), lambda i: (0, 0))],
        out_specs=[pl.BlockSpec((TM, D_MODEL), lambda i: (i, 0)),
                   pl.BlockSpec((2, TM), lambda i: (0, i)),
                   pl.BlockSpec((2, TM), lambda i: (0, i)),
                   pl.BlockSpec((2, TM), lambda i: (0, i)),
                   pl.BlockSpec((N_EXPERTS, LANES), lambda i: (0, 0))],
        out_shape=[jax.ShapeDtypeStruct((M_ALL, D_MODEL), f32),
                   jax.ShapeDtypeStruct((2, M_ALL), i32),
                   jax.ShapeDtypeStruct((2, M_ALL), f32),
                   jax.ShapeDtypeStruct((2, M_ALL), i32),
                   jax.ShapeDtypeStruct((N_EXPERTS, LANES), f32)],
        scratch_shapes=[pltpu.VMEM((N_EXPERTS, LANES), f32)],
        compiler_params=_cparams(("arbitrary",)),
        name="moe_pre",
    )(x_all, g.reshape(1, D_MODEL), modp, modp, mods, mods, router_wt, router_b.reshape(N_EXPERTS, 1))


def _dispatch_body(pos_ref, fill_lo_ref, fill_hi_ref, nu_ref, h_ref, hs_hbm, zero_buf, sem, tile_sem):
    i = pl.program_id(0)
    base = i * TMD

    def row_copy(src, dst_row):
        return pltpu.make_async_copy(src, hs_hbm.at[pl.ds(dst_row, 1)], sem)

    def wait_row(n, carry):
        row_copy(h_ref.at[pl.ds(0, 1)], 0).wait()
        return carry

    @pl.when(i == 0)
    def _():
        zero_buf[...] = jnp.zeros_like(zero_buf)
        zero_row = zero_buf.at[pl.ds(0, 1)]

        def tile_copy(t):
            return pltpu.make_async_copy(zero_buf, hs_hbm.at[pl.ds(t * TME, TME)], tile_sem)

        def tail_start(t, carry):
            tile_copy(t).start()
            return carry

        def tail_wait(t, carry):
            tile_copy(t).wait()
            return carry

        def fill_expert(e, carry):
            def fill(r, c2):
                row_copy(zero_row, r).start()
                return c2
            lax.fori_loop(fill_lo_ref[e], fill_hi_ref[e], fill, 0)
            lax.fori_loop(fill_lo_ref[e], fill_hi_ref[e], wait_row, 0)
            return carry

        lax.fori_loop(nu_ref[0], NT_E, tail_start, 0)
        lax.fori_loop(0, N_EXPERTS, fill_expert, 0)
        lax.fori_loop(nu_ref[0], NT_E, tail_wait, 0)

    def issue(r, carry):
        src = h_ref.at[pl.ds(r, 1)]
        row_copy(src, pos_ref[base + r]).start()
        row_copy(src, pos_ref[M_ALL + base + r]).start()
        return carry

    lax.fori_loop(0, TMD, issue, 0, unroll=8)
    lax.fori_loop(0, 2 * TMD, wait_row, 0, unroll=8)


def _dispatch(pos_flat, fill_lo, fill_hi, n_used, h):
    return pl.pallas_call(
        _dispatch_body,
        grid_spec=pltpu.PrefetchScalarGridSpec(
            num_scalar_prefetch=4,
            grid=(N_REAL // TMD,),
            in_specs=[pl.BlockSpec((TMD, D_MODEL), lambda i, *_: (i, 0))],
            out_specs=pl.BlockSpec(memory_space=pl.ANY),
            scratch_shapes=[pltpu.VMEM((TME, D_MODEL), f32), pltpu.SemaphoreType.DMA,
                            pltpu.SemaphoreType.DMA],
        ),
        out_shape=jax.ShapeDtypeStruct((R_MAX, D_MODEL), f32),
        compiler_params=_cparams(("arbitrary",)),
        name="moe_dispatch",
    )(pos_flat, fill_lo, fill_hi, n_used, h)


def _expert_body(te_ref, nu_ref, hs_ref, w1_ref, w3_ref, w2_ref, y_ref, w1b, w3b, w2b):
    t = pl.program_id(0)

    @pl.when(t < nu_ref[0])
    def _():
        changed = (t == 0) | (te_ref[t] != te_ref[jnp.maximum(t - 1, 0)])

        @pl.when(changed)
        def _():
            w1b[...] = w1_ref[...].astype(bf16)
            w3b[...] = w3_ref[...].astype(bf16)
            w2b[...] = w2_ref[...].astype(bf16)

        hb = hs_ref[...].astype(bf16)
        a = jnp.dot(hb, w1b[...], preferred_element_type=f32)
        b = jnp.dot(hb, w3b[...], preferred_element_type=f32)
        hid = (_silu(a) * b).astype(bf16)
        y_ref[...] = jnp.dot(hid, w2b[...], preferred_element_type=f32)

    @pl.when(t >= nu_ref[0])
    def _():
        y_ref[...] = jnp.zeros_like(y_ref)


def _experts(tile_expert, n_used, hs, w1, w3, w2, layer):
    def row_map(t, te, nu):
        return (jnp.minimum(t, nu[0] - 1), 0)

    def out_map(t, te, nu):
        return (t, 0)

    def w_map(t, te, nu):
        return (layer, te[t], 0, 0)

    return pl.pallas_call(
        _expert_body,
        grid_spec=pltpu.PrefetchScalarGridSpec(
            num_scalar_prefetch=2,
            grid=(NT_E,),
            in_specs=[pl.BlockSpec((TME, D_MODEL), row_map),
                      pl.BlockSpec((None, None, D_MODEL, D_FF), w_map),
                      pl.BlockSpec((None, None, D_MODEL, D_FF), w_map),
                      pl.BlockSpec((None, None, D_FF, D_MODEL), w_map)],
            out_specs=pl.BlockSpec((TME, D_MODEL), out_map),
            scratch_shapes=[pltpu.VMEM((D_MODEL, D_FF), bf16), pltpu.VMEM((D_MODEL, D_FF), bf16),
                            pltpu.VMEM((D_FF, D_MODEL), bf16)],
        ),
        out_shape=jax.ShapeDtypeStruct((R_MAX, D_MODEL), f32),
        compiler_params=_cparams(("arbitrary",)),
        name="moe_experts",
    )(tile_expert, n_used, hs, w1, w3, w2)


def _combine_body(pos_ref, x_ref, w_ref, gp_ref, gs_ref, y_hbm, *refs, split):
    if split:
        yp_ref, ys_ref, buf0, buf1, sems = refs
    else:
        ng_ref, scp_ref, shp_ref, scs_ref, shs_ref, o_ref, h_ref, buf0, buf1, sems = refs
    i = pl.program_id(0)
    n_real_tiles = N_REAL // TMC

    def copies(tile, r):
        slot = tile % 2
        base = tile * TMC
        return (pltpu.make_async_copy(y_hbm.at[pl.ds(pos_ref[base + r], 1)],
                                      buf0.at[slot, pl.ds(r, 1)], sems.at[slot]),
                pltpu.make_async_copy(y_hbm.at[pl.ds(pos_ref[M_ALL + base + r], 1)],
                                      buf1.at[slot, pl.ds(r, 1)], sems.at[slot]))

    def start_tile(tile):
        def start(r, c):
            for cp in copies(tile, r):
                cp.start()
            return c
        lax.fori_loop(0, TMC, start, 0, unroll=8)

    def wait_tile(tile):
        def wait(r, c):
            for cp in copies(tile, r):
                cp.wait()
            return c
        lax.fori_loop(0, TMC, wait, 0, unroll=8)

    @pl.when(i == 0)
    def _():
        start_tile(0)

    @pl.when(i + 1 < n_real_tiles)
    def _():
        start_tile(i + 1)

    def combined(gate):
        wait_tile(i)
        slot = i % 2
        m = w_ref[:, 0:1] * buf0[slot] + w_ref[:, 1:2] * buf1[slot]
        return x_ref[...] + gate * m

    def emit(gate, dst, sc_ref, sh_ref):
        o = combined(gate)
        dst[...] = o
        if not split:
            h_ref[...] = (_norm_rows(o, ng_ref[...]) * (1.0 + sc_ref[...]) + sh_ref[...]).astype(h_ref.dtype)

    @pl.when(i < NP // TMC)
    def _():
        emit(gp_ref[...], yp_ref if split else o_ref, None if split else scp_ref, None if split else shp_ref)

    @pl.when((i >= NP // TMC) & (i < n_real_tiles))
    def _():
        emit(gs_ref[...], ys_ref if split else o_ref, None if split else scs_ref, None if split else shs_ref)

    if not split:
        @pl.when(i >= n_real_tiles)
        def _():
            o_ref[...] = jnp.zeros_like(o_ref)
            h_ref[...] = jnp.zeros_like(h_ref)


def _combine(pos_flat, x_all, wts_t, modp, mods, layer, y, next_norm_g):
    split = next_norm_g is None
    tiles_per_seq = SEQ // TMC
    n_prompt_tiles = NP // TMC

    def mod_specs(lyr, k):
        return (pl.BlockSpec((None, None, None, 1, D_MODEL),
                             lambda i, p: (lyr, k, jnp.minimum(i // tiles_per_seq, BATCH - 1), 0, 0)),
                pl.BlockSpec((None, None, NS, D_MODEL), lambda i, p: (lyr, k, 0, 0)))

    gp, gs = mod_specs(layer, 5)
    in_specs = [pl.BlockSpec((TMC, D_MODEL), lambda i, p: (i, 0)),
                pl.BlockSpec((TMC, 2), lambda i, p: (i, 0)), gp, gs,
                pl.BlockSpec(memory_space=pl.ANY)]
    args = [x_all, wts_t, modp, mods, y]
    if split:
        grid = (N_REAL // TMC,)
        out_specs = [pl.BlockSpec((TMC, D_MODEL), lambda i, p: (jnp.minimum(i, n_prompt_tiles - 1), 0)),
                     pl.BlockSpec((NS, D_MODEL), lambda i, p: (0, 0))]
        out_shape = [jax.ShapeDtypeStruct((NP, D_MODEL), f32), jax.ShapeDtypeStruct((NS, D_MODEL), f32)]
    else:
        grid = (M_ALL // TMC,)
        scp, scs = mod_specs(layer + 1, 1)
        shp, shs = mod_specs(layer + 1, 0)
        in_specs += [pl.BlockSpec((1, D_MODEL), lambda i, p: (0, 0)), scp, shp, scs, shs]
        args += [next_norm_g.reshape(1, D_MODEL), modp, modp, mods, mods]
        out_specs = [pl.BlockSpec((TMC, D_MODEL), lambda i, p: (i, 0)),
                     pl.BlockSpec((TMC, D_MODEL), lambda i, p: (i, 0))]
        out_shape = [jax.ShapeDtypeStruct((M_ALL, D_MODEL), f32), jax.ShapeDtypeStruct((M_ALL, D_MODEL), bf16)]
    return pl.pallas_call(
        functools.partial(_combine_body, split=split),
        grid_spec=pltpu.PrefetchScalarGridSpec(
            num_scalar_prefetch=1,
            grid=grid,
            in_specs=in_specs,
            out_specs=out_specs,
            scratch_shapes=[pltpu.VMEM((2, TMC, D_MODEL), f32), pltpu.VMEM((2, TMC, D_MODEL), f32),
                            pltpu.SemaphoreType.DMA((2,))],
        ),
        out_shape=out_shape,
        compiler_params=_cparams(("arbitrary",)),
        name="moe_combine",
    )(pos_flat, *args)


def _moe(x_all, g, modp, mods, layer, router_wt, router_b, w1, w3, w2, next_norm_g):
    h, idx, wts, rank, cnt = _moe_pre(x_all, g, modp, mods, layer, router_wt, router_b)
    counts = cnt[:, 0].astype(i32)
    padded = ((counts + TME - 1) // TME) * TME
    ends = jnp.cumsum(padded)
    offs = ends - padded
    real = (jnp.arange(M_ALL) < N_REAL)[None, :]
    expert_ids = jnp.arange(N_EXPERTS, dtype=i32)[:, None, None]
    seg_start = jnp.sum(jnp.where(idx[None] == expert_ids, offs[:, None, None], 0), axis=0)
    pos = jnp.where(real, seg_start + rank, 0).astype(i32)
    pos_flat = pos.reshape(2 * M_ALL)
    n_used = (ends[-1] // TME).astype(i32).reshape(1)
    tile_start = jnp.minimum(jnp.arange(NT_E, dtype=i32), n_used[0] - 1) * TME
    tile_expert = jnp.sum((tile_start[:, None] >= ends[None, :]).astype(i32), axis=1).astype(i32)
    hs = _dispatch(pos_flat, (offs + counts).astype(i32), ends.astype(i32), n_used, h)
    y = _experts(tile_expert, n_used, hs, w1, w3, w2, layer)
    return _combine(pos_flat, x_all, wts.T, modp, mods, layer, y, next_norm_g)


def _sample_rows(a, width):
    return a.reshape(DEC_BATCH, SP, width)[:, :DEC_SEQ].reshape(NS, width)


def kernel(x_prompt, x_sample, c_prompt, c_sample, state_pool, cache_k, cache_v, state_hgrn, norm_g, ada_w, ada_b, w_in_ab, pool_w, pool_scale, q_norm, k_norm, attn_sinks, w_out_ab, w_in_c, lb_param, o_norm, w_out_c, router_w, router_b, moe_w1, moe_w3, moe_w2):
    n_c = BATCH + DEC_BATCH
    c_rows = ((n_c + 7) // 8) * 8
    c_all = jnp.concatenate([c_prompt, c_sample, jnp.zeros((c_rows - n_c, D_MODEL), f32)], axis=0)
    mod = _ada_mod(c_all, ada_w, ada_b).reshape(DEPTH, c_rows, 6, D_MODEL)
    modp = mod[:, :BATCH].transpose(0, 2, 1, 3).reshape(DEPTH, 6, BATCH, 1, D_MODEL)
    mods = jnp.repeat(mod[:, BATCH:n_c].transpose(0, 2, 1, 3), DEC_SEQ, axis=2)

    x_in = (x_prompt.reshape(NP, D_MODEL), x_sample.reshape(NS, D_MODEL))
    router_wt = router_w.T

    h = _normmod(x_in, norm_g[0, 0], modp, mods, 0, 0, 1)
    u = _proj(h, w_in_ab[0], 1280)
    tp = 256
    pool_p, qn_p, kn_p = _mix0_pre(
        u, u, lambda i: (jnp.maximum(i * (tp // 16) - 1, 0), 0), NP, tp, SEQ // tp, False,
        pool_w[0], pool_scale[0], q_norm[0], k_norm[0])
    nb = SEQ // WINDOW
    blk = lambda b, n: (b * nb + n, 0)
    prv = lambda b, n: (jnp.maximum(b * nb + n - 1, 0), 0)
    vcol = (D_POOL + D_ATTN + D_KV) // D_KV
    attn_p = _attn_call(
        attn_sinks[0], qn_p, kn_p, u,
        (pl.BlockSpec((WINDOW, D_ATTN), blk), pl.BlockSpec((WINDOW, D_KV), prv), pl.BlockSpec((WINDOW, D_KV), blk),
         pl.BlockSpec((WINDOW, D_KV), lambda b, n: (jnp.maximum(b * nb + n - 1, 0), vcol)),
         pl.BlockSpec((WINDOW, D_KV), lambda b, n: (b * nb + n, vcol)),
         pl.BlockSpec((WINDOW, D_ATTN), blk)),
        (BATCH, nb), NP, WINDOW, True)
    u_s = u[NP:N_REAL].reshape(DEC_BATCH, DEC_SEQ, D_IN_AB)
    u_s16 = jnp.pad(u_s, ((0, 0), (0, SP - DEC_SEQ), (0, 0))).reshape(NSP, D_IN_AB)
    halo_s = jnp.pad(state_pool[0], ((0, 0), (1, 0), (0, 0))).reshape(DEC_BATCH * 16, D_POOL)
    pool_s, qn_s, kn_s = _mix0_pre(
        u_s16, halo_s, lambda i: (i, 0), NSP, SP, 1, True,
        pool_w[0], pool_scale[0], q_norm[0], k_norm[0])
    kn_new = kn_s.reshape(DEC_BATCH, SP, D_KV)[:, :DEC_SEQ]
    v_new = u_s[:, :, D_POOL + D_ATTN + D_KV:]
    zpad = jnp.zeros((DEC_BATCH, WINDOW - DEC_SEQ, D_KV), f32)
    k_ext = jnp.concatenate([cache_k[0].reshape(DEC_BATCH, WINDOW, D_KV), kn_new, zpad], axis=1)
    v_ext = jnp.concatenate([cache_v[0].reshape(DEC_BATCH, WINDOW, D_KV), v_new, zpad], axis=1)
    ext_p = pl.BlockSpec((None, WINDOW, D_KV), lambda b: (b, 0, 0))
    ext_c = pl.BlockSpec((None, WINDOW, D_KV), lambda b: (b, 1, 0))
    attn_s = _attn_call(
        attn_sinks[0], qn_s, k_ext, v_ext,
        (pl.BlockSpec((SP, D_ATTN), lambda b: (b, 0)), ext_p, ext_c, ext_p, ext_c,
         pl.BlockSpec((SP, D_ATTN), lambda b: (b, 0))),
        (DEC_BATCH,), NSP, SP, False)
    x_all = _oproj([pool_p, attn_p], [_sample_rows(pool_s, D_POOL), _sample_rows(attn_s, D_ATTN)],
                   w_out_ab[0], x_in, modp, mods, 0, 2)
    x_all, h = _moe(x_all, norm_g[0, 1], modp, mods, 0, router_wt, router_b, moe_w1, moe_w3, moe_w2,
                    norm_g[1, 0])

    lb_soft = jax.nn.softmax(lb_param.astype(f32), axis=0)
    lower_bounds = jnp.cumsum(lb_soft, axis=0) - lb_soft[0:1]
    u1 = _proj(h, w_in_c[0], 1024)
    og_p, hg_p = _hgrn(u1, lower_bounds[1], o_norm[0], None,
                       nb=BATCH, nc=SEQ // 64, L=64, C=16, hpb=C_HEADS, n_valid=64, rows=NP)
    u1_s = jnp.pad(u1[NP:N_REAL].reshape(DEC_BATCH, DEC_SEQ, 4 * D_C),
                   ((0, 0), (0, SP - DEC_SEQ), (0, 0))).reshape(NSP, 4 * D_C)
    og_s, hg_s = _hgrn(u1_s, lower_bounds[1], o_norm[0], state_hgrn[0],
                       nb=DEC_BATCH, nc=1, L=SP, C=SP, hpb=C_HEADS, n_valid=DEC_SEQ, rows=NSP)
    x_all = _oproj([og_p], [_sample_rows(og_s, D_C)], w_out_c[0], (x_all,), modp, mods, 1, 2)
    y_p, y_s = _moe(x_all, norm_g[1, 1], modp, mods, 1, router_wt, router_b, moe_w1, moe_w3, moe_w2, None)

    y_prompt = y_p.reshape(BATCH, SEQ, D_MODEL)
    y_sample = y_s.reshape(DEC_BATCH, DEC_SEQ, D_MODEL)
    def seq_tail(a, n, c0, c1):
        return jnp.stack([a[(b + 1) * SEQ - n:(b + 1) * SEQ, c0:c1] for b in range(BATCH)])

    pool_prompt = seq_tail(u, POOL_STATE, 0, D_POOL)[None]
    pool_sample = jnp.concatenate([state_pool[0], u_s[:, :, :D_POOL]], axis=1)[:, -POOL_STATE:][None]
    kv_shape = (N_KV_HEADS, HEAD_DIM)
    k_prompt = seq_tail(kn_p, WINDOW, 0, D_KV).reshape(BATCH, WINDOW, *kv_shape)[None]
    v_prompt = seq_tail(u, WINDOW, D_POOL + D_ATTN + D_KV, D_IN_AB).reshape(BATCH, WINDOW, *kv_shape)[None]
    k_sample = jnp.concatenate([cache_k[0], kn_new.reshape(DEC_BATCH, DEC_SEQ, *kv_shape)], axis=1)[:, -WINDOW:][None]
    v_sample = jnp.concatenate([cache_v[0], v_new.reshape(DEC_BATCH, DEC_SEQ, *kv_shape)], axis=1)[:, -WINDOW:][None]
    return (y_prompt, y_sample, pool_prompt, pool_sample, k_prompt, k_sample, v_prompt, v_sample,
            hg_p[None], hg_s[None])
```

```python
import functools
import math

import numpy as np
import jax
import jax.numpy as jnp
from jax import lax
from jax.experimental import pallas as pl
from jax.experimental.pallas import tpu as pltpu

f32 = jnp.float32
bf16 = jnp.bfloat16
i32 = jnp.int32

D_MODEL = 2048
BATCH = 4
SEQ = 2048
DEPTH = 2
DEC_BATCH = 32
DEC_SEQ = 4
EPS = 1e-6
POOL_WINDOWS = (2, 4, 8, 16)
D_POOL = 512
POOL_GROUP = 128
POOL_STATE = 15
HEAD_DIM = 64
N_Q_HEADS = 24
N_KV_HEADS = 4
Q_PER_KV = 6
WINDOW = 128
D_ATTN = 1536
D_KV = 256
D_IN_AB = 2560
C_DK = 128
C_HEADS = 16
D_C = 2048
N_EXPERTS = 16
EXP_PER_GROUP = 4
D_FF = 512
NEG = -1e30
LOG2E = math.log2(math.e)

LANES = 128
VMEM_LIMIT = 56 * 1024 * 1024

NP = BATCH * SEQ
NS = DEC_BATCH * DEC_SEQ
N_REAL = NP + NS
TM = 512
M_ALL = ((N_REAL + TM - 1) // TM) * TM
N_TILES = M_ALL // TM
NPT = NP // TM
TPB = SEQ // TM
SP = 16
NSP = DEC_BATCH * SP

TME = 256
R_MAX = ((2 * N_REAL + N_EXPERTS * (TME - 1)) // TME + 1) * TME
NT_E = R_MAX // TME
TMC = 128
TMD = 640

FACTORED_BLOCK = 32
DECAY_CLAMP = 60.0


def _cparams(sem, vmem=VMEM_LIMIT):
    return pltpu.CompilerParams(dimension_semantics=sem, vmem_limit_bytes=vmem)


def _silu(x):
    return x * jax.nn.sigmoid(x)


def _nt_dot(a, b, **kw):
    return lax.dot_general(a, b, (((1,), (1,)), ((), ())), preferred_element_type=f32, **kw)


def _tn_dot(a, b):
    return lax.dot_general(a, b, (((0,), (0,)), ((), ())), preferred_element_type=f32)


def _ada_body(c_ref, w_ref, b_ref, o_ref):
    s = _silu(c_ref[...]).astype(bf16)
    o_ref[...] = jnp.dot(s, w_ref[...].astype(bf16), preferred_element_type=f32) + b_ref[...]


def _ada_mod(c_all, ada_w, ada_b):
    depth, d, e = ada_w.shape
    cr = c_all.shape[0]
    tn = 1024
    return pl.pallas_call(
        _ada_body,
        grid=(depth, e // tn),
        in_specs=[
            pl.BlockSpec((cr, d), lambda l, j: (0, 0)),
            pl.BlockSpec((None, d, tn), lambda l, j: (l, 0, j)),
            pl.BlockSpec((None, 1, tn), lambda l, j: (l, 0, j)),
        ],
        out_specs=pl.BlockSpec((None, cr, tn), lambda l, j: (l, 0, j)),
        out_shape=jax.ShapeDtypeStruct((depth, cr, e), f32),
        compiler_params=_cparams(("arbitrary", "arbitrary")),
        name="ada_mod",
    )(c_all, ada_w, ada_b.reshape(depth, 1, e))


def _mod_specs(layer, k):
    p = pl.BlockSpec((None, None, None, 1, D_MODEL),
                     lambda i: (layer, k, jnp.minimum(i // TPB, BATCH - 1), 0, 0))
    s = pl.BlockSpec((None, None, NS, D_MODEL), lambda i: (layer, k, 0, 0))
    return p, s


def _norm_rows(x, g):
    return x * lax.rsqrt(jnp.mean(x * x, axis=-1, keepdims=True) + EPS) * g


def _x_specs(x, width, col):
    if len(x) == 1:
        return [pl.BlockSpec((TM, width), lambda *g: (g[-1], col(*g)))]
    return [pl.BlockSpec((TM, width), lambda *g: (jnp.minimum(g[-1], NPT - 1), col(*g))),
            pl.BlockSpec((NS, width), lambda *g: (0, col(*g)))]


def _normmod_body(*refs):
    *x_refs, g_ref, scp_ref, shp_ref, scs_ref, shs_ref, h_ref = refs
    i = pl.program_id(0)

    @pl.when(i < NPT)
    def _():
        y = _norm_rows(x_refs[0][...], g_ref[...])
        h_ref[...] = (y * (1.0 + scp_ref[...]) + shp_ref[...]).astype(h_ref.dtype)

    @pl.when(i >= NPT)
    def _():
        y = _norm_rows(x_refs[-1][:NS], g_ref[...])
        h_ref[:NS] = (y * (1.0 + scs_ref[...]) + shs_ref[...]).astype(h_ref.dtype)
        h_ref[NS:] = jnp.zeros((TM - NS, D_MODEL), h_ref.dtype)


def _normmod(x, g, modp, mods, layer, k_shift, k_scale):
    scp, scs = _mod_specs(layer, k_scale)
    shp, shs = _mod_specs(layer, k_shift)
    return pl.pallas_call(
        _normmod_body,
        grid=(N_TILES,),
        in_specs=_x_specs(x, D_MODEL, lambda i: 0) + [pl.BlockSpec((1, D_MODEL), lambda i: (0, 0)),
                                                       scp, shp, scs, shs],
        out_specs=pl.BlockSpec((TM, D_MODEL), lambda i: (i, 0)),
        out_shape=jax.ShapeDtypeStruct((M_ALL, D_MODEL), bf16),
        compiler_params=_cparams(("arbitrary",)),
        name="normmod",
    )(*x, g.reshape(1, D_MODEL), modp, modp, mods, mods)


def _proj_body(h_ref, w_ref, o_ref, wb_ref):
    @pl.when(pl.program_id(1) == 0)
    def _():
        wb_ref[...] = w_ref[...].astype(bf16)

    o_ref[...] = jnp.dot(h_ref[...], wb_ref[...], preferred_element_type=f32)


def _proj(h_all, w, tn):
    k, e = w.shape
    return pl.pallas_call(
        _proj_body,
        grid=(e // tn, N_TILES),
        in_specs=[pl.BlockSpec((TM, k), lambda j, i: (i, 0)),
                  pl.BlockSpec((k, tn), lambda j, i: (0, j))],
        out_specs=pl.BlockSpec((TM, tn), lambda j, i: (i, j)),
        out_shape=jax.ShapeDtypeStruct((M_ALL, e), f32),
        scratch_shapes=[pltpu.VMEM((k, tn), bf16)],
        compiler_params=_cparams(("arbitrary", "arbitrary")),
        name="proj",
    )(h_all, w)


def _oproj_body(*refs, n_lhs):
    ap_refs = refs[:n_lhs]
    as_refs = refs[n_lhs:2 * n_lhs]
    w_ref, *x_refs, gp_ref, gs_ref, o_ref, wb_ref = refs[2 * n_lhs:]
    i = pl.program_id(1)

    @pl.when(i == 0)
    def _():
        wb_ref[...] = w_ref[...].astype(bf16)

    def matmul(a_refs, rows):
        off, acc = 0, None
        for a in a_refs:
            kk = a.shape[1]
            part = jnp.dot(a[:rows], wb_ref[off:off + kk, :], preferred_element_type=f32)
            acc = part if acc is None else acc + part
            off += kk
        return acc

    @pl.when(i < NPT)
    def _():
        o_ref[...] = x_refs[0][...] + gp_ref[...] * matmul(ap_refs, TM)

    @pl.when(i >= NPT)
    def _():
        o_ref[:NS] = x_refs[-1][:NS] + gs_ref[...] * matmul(as_refs, NS)
        o_ref[NS:] = jnp.zeros((TM - NS, o_ref.shape[1]), f32)


def _oproj(a_prompt, a_sample, w, x, modp, mods, layer, k_gate):
    n_lhs = len(a_prompt)
    kdim, e = w.shape
    tn = 1024
    in_specs = []
    for a in a_prompt:
        in_specs.append(pl.BlockSpec((TM, a.shape[1]), lambda j, i: (jnp.minimum(i, NPT - 1), 0)))
    for a in a_sample:
        in_specs.append(pl.BlockSpec((NS, a.shape[1]), lambda j, i: (0, 0)))
    in_specs += [pl.BlockSpec((kdim, tn), lambda j, i: (0, j))] + _x_specs(x, tn, lambda j, i: j) + [
        pl.BlockSpec((None, None, None, 1, tn),
                     lambda j, i: (layer, k_gate, jnp.minimum(i // TPB, BATCH - 1), 0, j)),
        pl.BlockSpec((None, None, NS, tn), lambda j, i: (layer, k_gate, 0, j)),
    ]
    return pl.pallas_call(
        functools.partial(_oproj_body, n_lhs=n_lhs),
        grid=(e // tn, N_TILES),
        in_specs=in_specs,
        out_specs=pl.BlockSpec((TM, tn), lambda j, i: (i, j)),
        out_shape=jax.ShapeDtypeStruct((M_ALL, e), f32),
        scratch_shapes=[pltpu.VMEM((kdim, tn), bf16)],
        compiler_params=_cparams(("arbitrary", "arbitrary")),
        name="oproj",
    )(*a_prompt, *a_sample, w, *x, modp, mods)


def _head_norm(x, w_row, scale):
    lane = lax.broadcasted_iota(i32, x.shape, 1)
    lo = lane < HEAD_DIM
    ss = x * x
    m_lo = jnp.sum(jnp.where(lo, ss, 0.0), axis=-1, keepdims=True)
    m_hi = jnp.sum(jnp.where(lo, 0.0, ss), axis=-1, keepdims=True)
    r = jnp.where(lo, lax.rsqrt(m_lo * (1.0 / HEAD_DIM) + EPS), lax.rsqrt(m_hi * (1.0 / HEAD_DIM) + EPS))
    return x * r * (w_row * scale)


def _mix0_pre_body(cur_ref, halo_ref, q0_ref, q1_ref, q2_ref, k_ref, pw_ref, ps_ref, qw_ref, kw_ref,
                   pool_ref, qn_ref, kn_ref, ext_ref, *, tp, tiles_per_seq, full_count):
    i = pl.program_id(0)
    halo = halo_ref[...]
    if not full_count:
        halo = jnp.where(i % tiles_per_seq == 0, 0.0, halo)
    ext_ref[0:16, :] = halo
    ext_ref[16:16 + tp, :] = cur_ref[...]
    t_pos = (i % tiles_per_seq) * tp + lax.broadcasted_iota(i32, (tp, 1), 0)
    for g, w in enumerate(POOL_WINDOWS):
        ch = slice(g * POOL_GROUP, (g + 1) * POOL_GROUP)
        acc = ext_ref[16:16 + tp, ch]
        for dlt in range(1, w):
            acc = acc + ext_ref[16 - dlt:16 - dlt + tp, ch]
        if full_count:
            mean = acc * (1.0 / w)
        else:
            cnt = jnp.minimum(t_pos + 1, w).astype(f32)
            mean = acc / cnt
        d = mean - ext_ref[16:16 + tp, ch]
        y = jnp.dot(d.astype(bf16), pw_ref[g].astype(bf16), preferred_element_type=f32)
        pool_ref[:, ch] = (y * ps_ref[:, ch]).astype(pool_ref.dtype)
    for c3, q_ref in enumerate((q0_ref, q1_ref, q2_ref)):
        for c in range(4):
            sl = slice(c * LANES, (c + 1) * LANES)
            dst = slice((c3 * 4 + c) * LANES, (c3 * 4 + c + 1) * LANES)
            qn_ref[:, dst] = _head_norm(q_ref[:, sl], qw_ref[...], LOG2E * HEAD_DIM ** -0.5).astype(qn_ref.dtype)
    for c in range(2):
        sl = slice(c * LANES, (c + 1) * LANES)
        kn_ref[:, sl] = _head_norm(k_ref[:, sl], kw_ref[...], 1.0)


def _mix0_pre(u, halo_src, halo_map, rows, tp, tiles_per_seq, full_count, pool_w, pool_scale, q_norm, k_norm):
    qw = jnp.tile(q_norm.reshape(1, HEAD_DIM), (1, 2))
    kw = jnp.tile(k_norm.reshape(1, HEAD_DIM), (1, 2))
    body = functools.partial(_mix0_pre_body, tp=tp, tiles_per_seq=tiles_per_seq, full_count=full_count)
    return pl.pallas_call(
        body,
        grid=(rows // tp,),
        in_specs=[
            pl.BlockSpec((tp, D_POOL), lambda i: (i, 0)),
            pl.BlockSpec((16, D_POOL), halo_map),
            pl.BlockSpec((tp, 512), lambda i: (i, 1)),
            pl.BlockSpec((tp, 512), lambda i: (i, 2)),
            pl.BlockSpec((tp, 512), lambda i: (i, 3)),
            pl.BlockSpec((tp, D_KV), lambda i: (i, (D_POOL + D_ATTN) // D_KV)),
            pl.BlockSpec((4, POOL_GROUP, POOL_GROUP), lambda i: (0, 0, 0)),
            pl.BlockSpec((1, D_POOL), lambda i: (0, 0)),
            pl.BlockSpec((1, LANES), lambda i: (0, 0)),
            pl.BlockSpec((1, LANES), lambda i: (0, 0)),
        ],
        out_specs=[
            pl.BlockSpec((tp, D_POOL), lambda i: (i, 0)),
            pl.BlockSpec((tp, D_ATTN), lambda i: (i, 0)),
            pl.BlockSpec((tp, D_KV), lambda i: (i, 0)),
        ],
        out_shape=[
            jax.ShapeDtypeStruct((rows, D_POOL), bf16),
            jax.ShapeDtypeStruct((rows, D_ATTN), bf16),
            jax.ShapeDtypeStruct((rows, D_KV), f32),
        ],
        scratch_shapes=[pltpu.VMEM((16 + tp, D_POOL), f32)],
        compiler_params=_cparams(("arbitrary",)),
        name="mix0_pre",
    )(u, halo_src, u, u, u, u, pool_w, pool_scale.reshape(1, D_POOL), qw, kw)


def _alibi_slopes(n):
    def pow2_slopes(m):
        start = 2.0 ** (-8.0 / m)
        return [start ** (i + 1) for i in range(m)]
    if n & (n - 1) == 0:
        s = pow2_slopes(n)
    else:
        c = 2 ** int(math.floor(math.log2(n)))
        s = pow2_slopes(c) + pow2_slopes(2 * c)[0::2][: n - c]
    return [float(v) for v in np.asarray(s, dtype=np.float32)]


_SLOPES = _alibi_slopes(N_Q_HEADS)


def _attn_body(sink_ref, q_ref, kp_ref, kc_ref, vp_ref, vc_ref, o_ref, bias_ref, *, tq, prompt, group):
    first = pl.program_id(0) == 0
    if prompt:
        first = first & (pl.program_id(1) == 0)

    @pl.when(first)
    def _():
        qi = lax.broadcasted_iota(i32, (tq, WINDOW), 0)
        kj = lax.broadcasted_iota(i32, (tq, WINDOW), 1)
        dist_p = (qi + WINDOW - kj).astype(f32)
        dist_c = (qi - kj).astype(f32)
        neg_p = jnp.where(qi + WINDOW - kj < WINDOW, 0.0, NEG)
        neg_c = jnp.where(qi - kj >= 0, 0.0, NEG)
        for head in range(N_Q_HEADS):
            bias_ref[head, 0] = neg_p - (_SLOPES[head] * LOG2E) * dist_p
            bias_ref[head, 1] = neg_c - (_SLOPES[head] * LOG2E) * dist_c

    lane = lax.broadcasted_iota(i32, (WINDOW, LANES), 1)
    lo = lane < HEAD_DIM
    qlane = lax.broadcasted_iota(i32, (tq, LANES), 1) < HEAD_DIM

    for g in range(group):
        q_rows_g = slice(g * tq, (g + 1) * tq)

        def kv(ref, cols):
            return ref[:, cols] if group == 1 else ref[g, :, cols]

        for j in range(N_KV_HEADS):
            pair = slice((j // 2) * LANES, (j // 2 + 1) * LANES)
            own_lo = (j % 2 == 0)

            def dup(kpair):
                sw = pltpu.roll(kpair, HEAD_DIM, 1)
                return jnp.where(lo, kpair, sw) if own_lo else jnp.where(lo, sw, kpair)

            def halves(vpair):
                own = jnp.where(lo, vpair, 0.0) if own_lo else jnp.where(lo, 0.0, vpair)
                sw = pltpu.roll(own, HEAD_DIM, 1)
                return (own, sw) if own_lo else (sw, own)

            kk_p = dup(kv(kp_ref, pair)).astype(bf16)
            kk_c = dup(kv(kc_ref, pair)).astype(bf16)
            v_p = [h.astype(bf16) for h in halves(kv(vp_ref, pair))]
            v_c = [h.astype(bf16) for h in halves(kv(vc_ref, pair))]

            heads = [(hf, c) for hf in (0, 1) for c in range(3)]
            q_rows = []
            for hf, c in heads:
                ch = slice((3 * j + c) * LANES, (3 * j + c + 1) * LANES)
                qc = q_ref[q_rows_g, ch].astype(f32)
                q_rows.append(jnp.where(qlane, qc, 0.0) if hf == 0 else jnp.where(qlane, 0.0, qc))
            q_st = jnp.concatenate(q_rows, axis=0).astype(bf16)
            s_p = _nt_dot(q_st, kk_p)
            s_c = _nt_dot(q_st, kk_c)
            for c in range(3):
                out = None
                for hf in (0, 1):
                    head = 6 * j + 2 * c + hf
                    rows = slice((hf * 3 + c) * tq, (hf * 3 + c + 1) * tq)
                    bias_p = bias_ref[head, 0]
                    if prompt:
                        bias_p = jnp.where(pl.program_id(1) > 0, bias_p, NEG)
                    sp = s_p[rows] + bias_p
                    sc = s_c[rows] + bias_ref[head, 1]
                    sink = sink_ref[head] * LOG2E
                    m = jnp.maximum(jnp.max(jnp.maximum(sp, sc), axis=-1, keepdims=True), sink)
                    ep = jnp.exp2(sp - m)
                    ec = jnp.exp2(sc - m)
                    den = jnp.sum(ep + ec, axis=-1, keepdims=True) + jnp.exp2(sink - m)
                    o = (jnp.dot(ep.astype(bf16), v_p[hf], preferred_element_type=f32)
                         + jnp.dot(ec.astype(bf16), v_c[hf], preferred_element_type=f32)) * (1.0 / den)
                    out = o if out is None else out + o
                ch = slice((3 * j + c) * LANES, (3 * j + c + 1) * LANES)
                o_ref[q_rows_g, ch] = out.astype(o_ref.dtype)


def _attn_call(sinks, q, k_arr, v_arr, specs, grid, rows, tq, prompt, group):
    body = functools.partial(_attn_body, tq=tq, prompt=prompt, group=group)
    q_spec, kp_spec, kc_spec, vp_spec, vc_spec, o_spec = specs
    return pl.pallas_call(
        body,
        grid=grid,
        in_specs=[pl.BlockSpec(memory_space=pltpu.SMEM), q_spec, kp_spec, kc_spec, vp_spec, vc_spec],
        out_specs=o_spec,
        out_shape=jax.ShapeDtypeStruct((rows, D_ATTN), bf16),
        scratch_shapes=[pltpu.VMEM((N_Q_HEADS, 2, tq, WINDOW), f32)],
        compiler_params=_cparams(("arbitrary",) * len(grid)),
        name="attn",
    )(sinks, q, k_arr, k_arr, v_arr, v_arr)


def _hgrn_inputs(q_ref, f_ref, i_ref, lb_ref, sl, *, L, n_valid):
    lb = lb_ref[:, sl]
    fp = f_ref[:, sl]
    q = _silu(q_ref[:, sl])
    t = jnp.exp(-jnp.abs(fp))
    inv = 1.0 / (1.0 + t)
    pos = fp >= 0.0
    logf = jnp.log(lb + (1.0 - lb) * jnp.where(pos, inv, t * inv))
    kk = (1.0 - lb) * jnp.where(pos, t * inv, inv)
    if n_valid < L:
        valid = lax.broadcasted_iota(i32, (L, 1), 0) < n_valid
        logf = jnp.where(valid, logf, 0.0)
        kk = jnp.where(valid, kk, 0.0)
    return q, kk, i_ref[:, sl], logf


def _cum_decay(logf, L):
    row = lax.broadcasted_iota(i32, (L, L), 0)
    col = lax.broadcasted_iota(i32, (L, L), 1)
    tri = (row >= col).astype(bf16)
    h1 = logf.astype(bf16)
    r1 = logf - h1.astype(f32)
    h2 = r1.astype(bf16)
    h3 = (r1 - h2.astype(f32)).astype(bf16)
    return (jnp.dot(tri, h1, preferred_element_type=f32) + jnp.dot(tri, h2, preferred_element_type=f32)
            + jnp.dot(tri, h3, preferred_element_type=f32))


def _sub_ref(b, lo):
    return b[lo - 1:lo] if lo > 0 else jnp.zeros((1, b.shape[1]), f32)


def _factored_operands(q, kk, b, *, L, CF):
    rowid = lax.broadcasted_iota(i32, (L, 1), 0)
    qts, kts = [], []
    for lo in range(0, L, CF):
        r = _sub_ref(b, lo)
        qts.append((q[lo:lo + CF] * jnp.exp(b[lo:lo + CF] - r)).astype(bf16))
        kts.append(jnp.where(rowid < lo + CF, kk * jnp.exp(jnp.minimum(r - b, DECAY_CLAMP)), 0.0).astype(bf16))
    return qts, kts


def _decay_span(b, *, L, CF):
    span = None
    for lo in range(0, L, CF):
        d = _sub_ref(b, lo) - b[lo + CF - 1:lo + CF]
        span = d if span is None else jnp.maximum(span, d)
    return span


def _intra_pairwise(q, kk, v, b, *, L, C):
    vb = v.astype(bf16)
    rowid = lax.broadcasted_iota(i32, (L, 1), 0)
    tloc = lax.broadcasted_iota(i32, (C, 1), 0)
    outs = []
    for lo in range(0, L, C):
        bi, qi, ki, vi = b[lo:lo + C], q[lo:lo + C], kk[lo:lo + C], v[lo:lo + C]
        if lo > 0:
            r = _sub_ref(b, lo)
            qt = (qi * jnp.exp(bi - r)).astype(bf16)
            kt = jnp.where(rowid < lo, kk * jnp.exp(jnp.minimum(r - b, 0.0)), 0.0).astype(bf16)
            oi = jnp.dot(_nt_dot(qt, kt).astype(bf16), vb, preferred_element_type=f32)
        else:
            oi = jnp.zeros((C, LANES), f32)
        for s in range(C):
            e = jnp.exp(jnp.minimum(bi - bi[s:s + 1], 0.0))
            a = jnp.sum(qi * ki[s:s + 1] * e, axis=-1, keepdims=True)
            oi = oi + jnp.where(tloc >= s, a, 0.0) * vi[s:s + 1]
        outs.append(oi)
    return outs[0] if len(outs) == 1 else jnp.concatenate(outs, axis=0)


def _hgrn_body(*refs, L, C, hpb, n_valid, has_state):
    if has_state:
        q_ref, f_ref, i_ref, g_ref, lb_ref, on_ref, s0_ref, og_ref, so_ref, st_ref, inter_ref = refs
    else:
        q_ref, f_ref, i_ref, g_ref, lb_ref, on_ref, og_ref, so_ref, st_ref, inter_ref = refs
    c = pl.program_id(2)
    nc = pl.num_programs(2)
    CF = min(L, FACTORED_BLOCK)

    @pl.when(c == 0)
    def _():
        for hh in range(hpb):
            st_ref[hh] = s0_ref[hh].T if has_state else jnp.zeros((C_DK, C_DK), f32)

    def finish(o, sl):
        o = o * lax.rsqrt(jnp.mean(o * o, axis=-1, keepdims=True) + EPS) * on_ref[...]
        og_ref[:, sl] = (o * _silu(g_ref[:, sl])).astype(og_ref.dtype)

    q, kk, v, logf = _hgrn_inputs(q_ref, f_ref, i_ref, lb_ref, slice(0, hpb * LANES), L=L, n_valid=n_valid)
    b = _cum_decay(logf, L)
    vb = v.astype(bf16)
    q_hat = (q * jnp.exp(b)).astype(bf16)
    bl = b[L - 1:L]
    k_end = (kk * jnp.exp(bl - b)).astype(bf16)
    st_decay = jnp.exp(bl)
    qts, kts = _factored_operands(q, kk, b, L=L, CF=CF)
    span = _decay_span(b, L=L, CF=CF)
    causal = lax.broadcasted_iota(i32, (L, L), 0) >= lax.broadcasted_iota(i32, (L, L), 1)
    for hh in range(hpb):
        sl = slice(hh * LANES, (hh + 1) * LANES)
        st = st_ref[hh]
        o_inter = _nt_dot(q_hat[:, sl], st.astype(bf16))
        st_ref[hh] = st * st_decay[:, sl] + _tn_dot(vb[:, sl], k_end[:, sl])
        inter_ref[hh] = o_inter
        blocks = [_nt_dot(qt[:, sl], kt[:, sl]) for qt, kt in zip(qts, kts)]
        a = blocks[0] if len(blocks) == 1 else jnp.concatenate(blocks, axis=0)
        a = jnp.where(causal, a, 0.0).astype(bf16)
        finish(jnp.dot(a, vb[:, sl], preferred_element_type=f32) + o_inter, sl)

    @pl.when(jnp.max(span) >= DECAY_CLAMP)
    def _():
        for hh in range(hpb):
            sl = slice(hh * LANES, (hh + 1) * LANES)
            q, kk, v, logf = _hgrn_inputs(q_ref, f_ref, i_ref, lb_ref, sl, L=L, n_valid=n_valid)
            b = _cum_decay(logf, L)
            finish(_intra_pairwise(q, kk, v, b, L=L, C=C) + inter_ref[hh], sl)

    @pl.when(c == nc - 1)
    def _():
        for hh in range(hpb):
            so_ref[hh] = st_ref[hh].T


def _hgrn(u, lb, o_norm, s0, *, nb, nc, L, C, hpb, n_valid, rows):
    nh = C_HEADS // hpb
    w = hpb * LANES
    has_state = s0 is not None
    body = functools.partial(_hgrn_body, L=L, C=C, hpb=hpb, n_valid=n_valid, has_state=has_state)

    def uspec(part):
        return pl.BlockSpec((L, w), lambda b, h, c: (b * nc + c, part * nh + h))

    in_specs = [uspec(0), uspec(1), uspec(2), uspec(3),
                pl.BlockSpec((1, w), lambda b, h, c: (0, h)),
                pl.BlockSpec((1, LANES), lambda b, h, c: (0, 0))]
    args = [u, u, u, u, lb.reshape(1, D_C), o_norm.reshape(1, LANES)]
    if has_state:
        in_specs.append(pl.BlockSpec((None, hpb, C_DK, C_DK), lambda b, h, c: (b, h, 0, 0)))
        args.append(s0)
    return pl.pallas_call(
        body,
        grid=(nb, nh, nc),
        in_specs=in_specs,
        out_specs=[pl.BlockSpec((L, w), lambda b, h, c: (b * nc + c, h)),
                   pl.BlockSpec((None, hpb, C_DK, C_DK), lambda b, h, c: (b, h, 0, 0))],
        out_shape=[jax.ShapeDtypeStruct((rows, D_C), bf16),
                   jax.ShapeDtypeStruct((nb, C_HEADS, C_DK, C_DK), f32)],
        scratch_shapes=[pltpu.VMEM((hpb, C_DK, C_DK), f32), pltpu.VMEM((hpb, L, LANES), f32)],
        compiler_params=_cparams(("arbitrary", "arbitrary", "arbitrary")),
        name="hgrn",
    )(*args)


def _moe_pre_body(x_ref, g_ref, scp_ref, shp_ref, scs_ref, shs_ref, wrt_ref, rb_ref,
                  h_ref, idx_ref, wts_ref, rank_ref, cnt_ref, carry):
    i = pl.program_id(0)

    @pl.when(i == 0)
    def _():
        carry[...] = jnp.zeros_like(carry)

    y = _norm_rows(x_ref[...], g_ref[...])

    @pl.when(i < NPT)
    def _():
        h_ref[...] = y * (1.0 + scp_ref[...]) + shp_ref[...]

    @pl.when(i >= NPT)
    def _():
        h_ref[:NS] = y[:NS] * (1.0 + scs_ref[...]) + shs_ref[...]
        h_ref[NS:] = jnp.zeros((TM - NS, D_MODEL), f32)

    h = h_ref[...]

    logits = _nt_dot(wrt_ref[...], h, precision=lax.Precision.HIGHEST)
    scores = jax.nn.sigmoid(logits)
    sel = scores + rb_ref[...]
    best = jnp.zeros((1, TM), i32)
    best_v = None
    for g in range(N_EXPERTS // EXP_PER_GROUP):
        r = [sel[EXP_PER_GROUP * g + t:EXP_PER_GROUP * g + t + 1] for t in range(EXP_PER_GROUP)]
        top2 = None
        for a in range(EXP_PER_GROUP):
            for bb in range(a + 1, EXP_PER_GROUP):
                s2 = r[a] + r[bb]
                top2 = s2 if top2 is None else jnp.maximum(top2, s2)
        if g == 0:
            best_v = top2
        else:
            upd = top2 > best_v
            best = jnp.where(upd, g, best)
            best_v = jnp.where(upd, top2, best_v)
    eidx = lax.broadcasted_iota(i32, (N_EXPERTS, TM), 0)
    masked = jnp.where(jnp.right_shift(eidx, 2) == best, sel, NEG)
    m1 = jnp.max(masked, axis=0, keepdims=True)
    i1 = jnp.min(jnp.where(masked == m1, eidx, N_EXPERTS), axis=0, keepdims=True)
    masked2 = jnp.where(eidx == i1, -jnp.inf, masked)
    m2 = jnp.max(masked2, axis=0, keepdims=True)
    i2 = jnp.min(jnp.where(masked2 == m2, eidx, N_EXPERTS), axis=0, keepdims=True)
    w1 = jnp.sum(jnp.where(eidx == i1, scores, 0.0), axis=0, keepdims=True)
    w2 = jnp.sum(jnp.where(eidx == i2, scores, 0.0), axis=0, keepdims=True)
    den = w1 + w2
    idx_ref[0:1, :] = i1
    idx_ref[1:2, :] = i2
    wts_ref[0:1, :] = w1 / den
    wts_ref[1:2, :] = w2 / den

    colg = i * TM + lax.broadcasted_iota(i32, (1, TM), 1)
    oh = jnp.where(((eidx == i1) | (eidx == i2)) & (colg < N_REAL), 1.0, 0.0)
    rr = lax.broadcasted_iota(i32, (TM, TM), 0)
    cc = lax.broadcasted_iota(i32, (TM, TM), 1)
    before = (rr < cc).astype(bf16)
    rank = jnp.dot(oh.astype(bf16), before, preferred_element_type=f32) + carry[:, 0:1]
    rank_ref[0:1, :] = jnp.sum(jnp.where(eidx == i1, rank, 0.0), axis=0, keepdims=True).astype(i32)
    rank_ref[1:2, :] = jnp.sum(jnp.where(eidx == i2, rank, 0.0), axis=0, keepdims=True).astype(i32)
    carry[...] = carry[...] + jnp.sum(oh, axis=1, keepdims=True)
    cnt_ref[...] = carry[...]


def _moe_pre(x_all, g, modp, mods, layer, router_wt, router_b):
    scp, scs = _mod_specs(layer, 4)
    shp, shs = _mod_specs(layer, 3)
    return pl.pallas_call(
        _moe_pre_body,
        grid=(N_TILES,),
        in_specs=[pl.BlockSpec((TM, D_MODEL), lambda i: (i, 0)),
                  pl.BlockSpec((1, D_MODEL), lambda i: (0, 0)),
                  scp, shp, scs, shs,
                  pl.BlockSpec((N_EXPERTS, D_MODEL), lambda i: (0, 0)),
                  pl.BlockSpec((N_EXPERTS, 1), lambda i: (0, 0))],
        out_specs=[pl.BlockSpec((TM, D_MODEL), lambda i: (i, 0)),
                   pl.BlockSpec((2, TM), lambda i: (0, i)),
                   pl.BlockSpec((2, TM), lambda i: (0, i)),
                   pl.BlockSpec((2, TM), lambda i: (0, i)),
                   pl.BlockSpec((N_EXPERTS, LANES), lambda i: (0, 0))],
        out_shape=[jax.ShapeDtypeStruct((M_ALL, D_MODEL), f32),
                   jax.ShapeDtypeStruct((2, M_ALL), i32),
                   jax.ShapeDtypeStruct((2, M_ALL), f32),
                   jax.ShapeDtypeStruct((2, M_ALL), i32),
                   jax.ShapeDtypeStruct((N_EXPERTS, LANES), f32)],
        scratch_shapes=[pltpu.VMEM((N_EXPERTS, LANES), f32)],
        compiler_params=_cparams(("arbitrary",)),
        name="moe_pre",
    )(x_all, g.reshape(1, D_MODEL), modp, modp, mods, mods, router_wt, router_b.reshape(N_EXPERTS, 1))


def _dispatch_body(pos_ref, fill_lo_ref, fill_hi_ref, nu_ref, h_ref, hs_hbm, zero_buf, sem, tile_sem):
    i = pl.program_id(0)
    base = i * TMD

    def row_copy(src, dst_row):
        return pltpu.make_async_copy(src, hs_hbm.at[pl.ds(dst_row, 1)], sem)

    def wait_row(n, carry):
        row_copy(h_ref.at[pl.ds(0, 1)], 0).wait()
        return carry

    @pl.when(i == 0)
    def _():
        zero_buf[...] = jnp.zeros_like(zero_buf)
        zero_row = zero_buf.at[pl.ds(0, 1)]

        def tile_copy(t):
            return pltpu.make_async_copy(zero_buf, hs_hbm.at[pl.ds(t * TME, TME)], tile_sem)

        def tail_start(t, carry):
            tile_copy(t).start()
            return carry

        def tail_wait(t, carry):
            tile_copy(t).wait()
            return carry

        def fill_expert(e, carry):
            def fill(r, c2):
                row_copy(zero_row, r).start()
                return c2
            lax.fori_loop(fill_lo_ref[e], fill_hi_ref[e], fill, 0)
            lax.fori_loop(fill_lo_ref[e], fill_hi_ref[e], wait_row, 0)
            return carry

        lax.fori_loop(nu_ref[0], NT_E, tail_start, 0)
        lax.fori_loop(0, N_EXPERTS, fill_expert, 0)
        lax.fori_loop(nu_ref[0], NT_E, tail_wait, 0)

    def issue(r, carry):
        src = h_ref.at[pl.ds(r, 1)]
        row_copy(src, pos_ref[base + r]).start()
        row_copy(src, pos_ref[M_ALL + base + r]).start()
        return carry

    lax.fori_loop(0, TMD, issue, 0, unroll=8)
    lax.fori_loop(0, 2 * TMD, wait_row, 0, unroll=8)


def _dispatch(pos_flat, fill_lo, fill_hi, n_used, h):
    return pl.pallas_call(
        _dispatch_body,
        grid_spec=pltpu.PrefetchScalarGridSpec(
            num_scalar_prefetch=4,
            grid=(N_REAL // TMD,),
            in_specs=[pl.BlockSpec((TMD, D_MODEL), lambda i, *_: (i, 0))],
            out_specs=pl.BlockSpec(memory_space=pl.ANY),
            scratch_shapes=[pltpu.VMEM((TME, D_MODEL), f32), pltpu.SemaphoreType.DMA,
                            pltpu.SemaphoreType.DMA],
        ),
        out_shape=jax.ShapeDtypeStruct((R_MAX, D_MODEL), f32),
        compiler_params=_cparams(("arbitrary",)),
        name="moe_dispatch",
    )(pos_flat, fill_lo, fill_hi, n_used, h)


def _expert_body(te_ref, nu_ref, hs_ref, w1_ref, w3_ref, w2_ref, y_ref, w1b, w3b, w2b):
    t = pl.program_id(0)

    @pl.when(t < nu_ref[0])
    def _():
        changed = (t == 0) | (te_ref[t] != te_ref[jnp.maximum(t - 1, 0)])

        @pl.when(changed)
        def _():
            w1b[...] = w1_ref[...].astype(bf16)
            w3b[...] = w3_ref[...].astype(bf16)
            w2b[...] = w2_ref[...].astype(bf16)

        hb = hs_ref[...].astype(bf16)
        a = jnp.dot(hb, w1b[...], preferred_element_type=f32)
        b = jnp.dot(hb, w3b[...], preferred_element_type=f32)
        hid = (_silu(a) * b).astype(bf16)
        y_ref[...] = jnp.dot(hid, w2b[...], preferred_element_type=f32)

    @pl.when(t >= nu_ref[0])
    def _():
        y_ref[...] = jnp.zeros_like(y_ref)


def _experts(tile_expert, n_used, hs, w1, w3, w2, layer):
    def row_map(t, te, nu):
        return (jnp.minimum(t, nu[0] - 1), 0)

    def out_map(t, te, nu):
        return (t, 0)

    def w_map(t, te, nu):
        return (layer, te[t], 0, 0)

    return pl.pallas_call(
        _expert_body,
        grid_spec=pltpu.PrefetchScalarGridSpec(
            num_scalar_prefetch=2,
            grid=(NT_E,),
            in_specs=[pl.BlockSpec((TME, D_MODEL), row_map),
                      pl.BlockSpec((None, None, D_MODEL, D_FF), w_map),
                      pl.BlockSpec((None, None, D_MODEL, D_FF), w_map),
                      pl.BlockSpec((None, None, D_FF, D_MODEL), w_map)],
            out_specs=pl.BlockSpec((TME, D_MODEL), out_map),
            scratch_shapes=[pltpu.VMEM((D_MODEL, D_FF), bf16), pltpu.VMEM((D_MODEL, D_FF), bf16),
                            pltpu.VMEM((D_FF, D_MODEL), bf16)],
        ),
        out_shape=jax.ShapeDtypeStruct((R_MAX, D_MODEL), f32),
        compiler_params=_cparams(("arbitrary",)),
        name="moe_experts",
    )(tile_expert, n_used, hs, w1, w3, w2)


def _combine_body(pos_ref, x_ref, w_ref, gp_ref, gs_ref, y_hbm, *refs, split):
    if split:
        yp_ref, ys_ref, buf0, buf1, sems = refs
    else:
        ng_ref, scp_ref, shp_ref, scs_ref, shs_ref, o_ref, h_ref, buf0, buf1, sems = refs
    i = pl.program_id(0)
    n_real_tiles = N_REAL // TMC

    def copies(tile, r):
        slot = tile % 2
        base = tile * TMC
        return (pltpu.make_async_copy(y_hbm.at[pl.ds(pos_ref[base + r], 1)],
                                      buf0.at[slot, pl.ds(r, 1)], sems.at[slot]),
                pltpu.make_async_copy(y_hbm.at[pl.ds(pos_ref[M_ALL + base + r], 1)],
                                      buf1.at[slot, pl.ds(r, 1)], sems.at[slot]))

    def start_tile(tile):
        def start(r, c):
            for cp in copies(tile, r):
                cp.start()
            return c
        lax.fori_loop(0, TMC, start, 0, unroll=8)

    def wait_tile(tile):
        def wait(r, c):
            for cp in copies(tile, r):
                cp.wait()
            return c
        lax.fori_loop(0, TMC, wait, 0, unroll=8)

    @pl.when(i == 0)
    def _():
        start_tile(0)

    @pl.when(i + 1 < n_real_tiles)
    def _():
        start_tile(i + 1)

    def combined(gate):
        wait_tile(i)
        slot = i % 2
        m = w_ref[:, 0:1] * buf0[slot] + w_ref[:, 1:2] * buf1[slot]
        return x_ref[...] + gate * m

    def emit(gate, dst, sc_ref, sh_ref):
        o = combined(gate)
        dst[...] = o
        if not split:
            h_ref[...] = (_norm_rows(o, ng_ref[...]) * (1.0 + sc_ref[...]) + sh_ref[...]).astype(h_ref.dtype)

    @pl.when(i < NP // TMC)
    def _():
        emit(gp_ref[...], yp_ref if split else o_ref, None if split else scp_ref, None if split else shp_ref)

    @pl.when((i >= NP // TMC) & (i < n_real_tiles))
    def _():
        emit(gs_ref[...], ys_ref if split else o_ref, None if split else scs_ref, None if split else shs_ref)

    if not split:
        @pl.when(i >= n_real_tiles)
        def _():
            o_ref[...] = jnp.zeros_like(o_ref)
            h_ref[...] = jnp.zeros_like(h_ref)


def _combine(pos_flat, x_all, wts_t, modp, mods, layer, y, next_norm_g):
    split = next_norm_g is None
    tiles_per_seq = SEQ // TMC
    n_prompt_tiles = NP // TMC

    def mod_specs(lyr, k):
        return (pl.BlockSpec((None, None, None, 1, D_MODEL),
                             lambda i, p: (lyr, k, jnp.minimum(i // tiles_per_seq, BATCH - 1), 0, 0)),
                pl.BlockSpec((None, None, NS, D_MODEL), lambda i, p: (lyr, k, 0, 0)))

    gp, gs = mod_specs(layer, 5)
    in_specs = [pl.BlockSpec((TMC, D_MODEL), lambda i, p: (i, 0)),
                pl.BlockSpec((TMC, 2), lambda i, p: (i, 0)), gp, gs,
                pl.BlockSpec(memory_space=pl.ANY)]
    args = [x_all, wts_t, modp, mods, y]
    if split:
        grid = (N_REAL // TMC,)
        out_specs = [pl.BlockSpec((TMC, D_MODEL), lambda i, p: (jnp.minimum(i, n_prompt_tiles - 1), 0)),
                     pl.BlockSpec((NS, D_MODEL), lambda i, p: (0, 0))]
        out_shape = [jax.ShapeDtypeStruct((NP, D_MODEL), f32), jax.ShapeDtypeStruct((NS, D_MODEL), f32)]
    else:
        grid = (M_ALL // TMC,)
        scp, scs = mod_specs(layer + 1, 1)
        shp, shs = mod_specs(layer + 1, 0)
        in_specs += [pl.BlockSpec((1, D_MODEL), lambda i, p: (0, 0)), scp, shp, scs, shs]
        args += [next_norm_g.reshape(1, D_MODEL), modp, modp, mods, mods]
        out_specs = [pl.BlockSpec((TMC, D_MODEL), lambda i, p: (i, 0)),
                     pl.BlockSpec((TMC, D_MODEL), lambda i, p: (i, 0))]
        out_shape = [jax.ShapeDtypeStruct((M_ALL, D_MODEL), f32), jax.ShapeDtypeStruct((M_ALL, D_MODEL), bf16)]
    return pl.pallas_call(
        functools.partial(_combine_body, split=split),
        grid_spec=pltpu.PrefetchScalarGridSpec(
            num_scalar_prefetch=1,
            grid=grid,
            in_specs=in_specs,
            out_specs=out_specs,
            scratch_shapes=[pltpu.VMEM((2, TMC, D_MODEL), f32), pltpu.VMEM((2, TMC, D_MODEL), f32),
                            pltpu.SemaphoreType.DMA((2,))],
        ),
        out_shape=out_shape,
        compiler_params=_cparams(("arbitrary",)),
        name="moe_combine",
    )(pos_flat, *args)


def _moe(x_all, g, modp, mods, layer, router_wt, router_b, w1, w3, w2, next_norm_g):
    h, idx, wts, rank, cnt = _moe_pre(x_all, g, modp, mods, layer, router_wt, router_b)
    counts = cnt[:, 0].astype(i32)
    padded = ((counts + TME - 1) // TME) * TME
    ends = jnp.cumsum(padded)
    offs = ends - padded
    real = (jnp.arange(M_ALL) < N_REAL)[None, :]
    expert_ids = jnp.arange(N_EXPERTS, dtype=i32)[:, None, None]
    seg_start = jnp.sum(jnp.where(idx[None] == expert_ids, offs[:, None, None], 0), axis=0)
    pos = jnp.where(real, seg_start + rank, 0).astype(i32)
    pos_flat = pos.reshape(2 * M_ALL)
    n_used = (ends[-1] // TME).astype(i32).reshape(1)
    tile_start = jnp.minimum(jnp.arange(NT_E, dtype=i32), n_used[0] - 1) * TME
    tile_expert = jnp.sum((tile_start[:, None] >= ends[None, :]).astype(i32), axis=1).astype(i32)
    hs = _dispatch(pos_flat, (offs + counts).astype(i32), ends.astype(i32), n_used, h)
    y = _experts(tile_expert, n_used, hs, w1, w3, w2, layer)
    return _combine(pos_flat, x_all, wts.T, modp, mods, layer, y, next_norm_g)


def _sample_rows(a, width):
    return a.reshape(DEC_BATCH, SP, width)[:, :DEC_SEQ].reshape(NS, width)


def kernel(x_prompt, x_sample, c_prompt, c_sample, state_pool, cache_k, cache_v, state_hgrn, norm_g, ada_w, ada_b, w_in_ab, pool_w, pool_scale, q_norm, k_norm, attn_sinks, w_out_ab, w_in_c, lb_param, o_norm, w_out_c, router_w, router_b, moe_w1, moe_w3, moe_w2):
    n_c = BATCH + DEC_BATCH
    c_rows = ((n_c + 7) // 8) * 8
    c_all = jnp.concatenate([c_prompt, c_sample, jnp.zeros((c_rows - n_c, D_MODEL), f32)], axis=0)
    mod = _ada_mod(c_all, ada_w, ada_b).reshape(DEPTH, c_rows, 6, D_MODEL)
    modp = mod[:, :BATCH].transpose(0, 2, 1, 3).reshape(DEPTH, 6, BATCH, 1, D_MODEL)
    mods = jnp.repeat(mod[:, BATCH:n_c].transpose(0, 2, 1, 3), DEC_SEQ, axis=2)

    x_in = (x_prompt.reshape(NP, D_MODEL), x_sample.reshape(NS, D_MODEL))
    router_wt = router_w.T

    h = _normmod(x_in, norm_g[0, 0], modp, mods, 0, 0, 1)
    u = _proj(h, w_in_ab[0], 1280)
    tp = 256
    pool_p, qn_p, kn_p = _mix0_pre(
        u, u, lambda i: (jnp.maximum(i * (tp // 16) - 1, 0), 0), NP, tp, SEQ // tp, False,
        pool_w[0], pool_scale[0], q_norm[0], k_norm[0])
    nb = SEQ // WINDOW
    blk = lambda b, n: (b * nb + n, 0)
    prv = lambda b, n: (jnp.maximum(b * nb + n - 1, 0), 0)
    vcol = (D_POOL + D_ATTN + D_KV) // D_KV
    attn_p = _attn_call(
        attn_sinks[0], qn_p, kn_p, u,
        (pl.BlockSpec((WINDOW, D_ATTN), blk), pl.BlockSpec((WINDOW, D_KV), prv), pl.BlockSpec((WINDOW, D_KV), blk),
         pl.BlockSpec((WINDOW, D_KV), lambda b, n: (jnp.maximum(b * nb + n - 1, 0), vcol)),
         pl.BlockSpec((WINDOW, D_KV), lambda b, n: (b * nb + n, vcol)),
         pl.BlockSpec((WINDOW, D_ATTN), blk)),
        (BATCH, nb), NP, WINDOW, True, 1)
    u_s = u[NP:N_REAL].reshape(DEC_BATCH, DEC_SEQ, D_IN_AB)
    u_s16 = jnp.pad(u_s, ((0, 0), (0, SP - DEC_SEQ), (0, 0))).reshape(NSP, D_IN_AB)
    halo_s = jnp.pad(state_pool[0], ((0, 0), (1, 0), (0, 0))).reshape(DEC_BATCH * 16, D_POOL)
    pool_s, qn_s, kn_s = _mix0_pre(
        u_s16, halo_s, lambda i: (i, 0), NSP, SP, 1, True,
        pool_w[0], pool_scale[0], q_norm[0], k_norm[0])
    kn_new = kn_s.reshape(DEC_BATCH, SP, D_KV)[:, :DEC_SEQ]
    v_new = u_s[:, :, D_POOL + D_ATTN + D_KV:]
    zpad = jnp.zeros((DEC_BATCH, WINDOW - DEC_SEQ, D_KV), f32)
    k_ext = jnp.concatenate([cache_k[0].reshape(DEC_BATCH, WINDOW, D_KV), kn_new, zpad], axis=1)
    v_ext = jnp.concatenate([cache_v[0].reshape(DEC_BATCH, WINDOW, D_KV), v_new, zpad], axis=1)
    sg = 4
    ext_p = pl.BlockSpec((sg, WINDOW, D_KV), lambda b: (b, 0, 0))
    ext_c = pl.BlockSpec((sg, WINDOW, D_KV), lambda b: (b, 1, 0))
    attn_s = _attn_call(
        attn_sinks[0], qn_s, k_ext, v_ext,
        (pl.BlockSpec((sg * SP, D_ATTN), lambda b: (b, 0)), ext_p, ext_c, ext_p, ext_c,
         pl.BlockSpec((sg * SP, D_ATTN), lambda b: (b, 0))),
        (DEC_BATCH // sg,), NSP, SP, False, sg)
    x_all = _oproj([pool_p, attn_p], [_sample_rows(pool_s, D_POOL), _sample_rows(attn_s, D_ATTN)],
                   w_out_ab[0], x_in, modp, mods, 0, 2)
    x_all, h = _moe(x_all, norm_g[0, 1], modp, mods, 0, router_wt, router_b, moe_w1, moe_w3, moe_w2,
                    norm_g[1, 0])

    lb_soft = jax.nn.softmax(lb_param.astype(f32), axis=0)
    lower_bounds = jnp.cumsum(lb_soft, axis=0) - lb_soft[0:1]
    u1 = _proj(h, w_in_c[0], 2048)
    og_p, hg_p = _hgrn(u1, lower_bounds[1], o_norm[0], None,
                       nb=BATCH, nc=SEQ // 64, L=64, C=16, hpb=C_HEADS, n_valid=64, rows=NP)
    u1_s = jnp.pad(u1[NP:N_REAL].reshape(DEC_BATCH, DEC_SEQ, 4 * D_C),
                   ((0, 0), (0, SP - DEC_SEQ), (0, 0))).reshape(NSP, 4 * D_C)
    og_s, hg_s = _hgrn(u1_s, lower_bounds[1], o_norm[0], state_hgrn[0],
                       nb=DEC_BATCH, nc=1, L=SP, C=SP, hpb=C_HEADS, n_valid=DEC_SEQ, rows=NSP)
    x_all = _oproj([og_p], [_sample_rows(og_s, D_C)], w_out_c[0], (x_all,), modp, mods, 1, 2)
    y_p, y_s = _moe(x_all, norm_g[1, 1], modp, mods, 1, router_wt, router_b, moe_w1, moe_w3, moe_w2, None)

    y_prompt = y_p.reshape(BATCH, SEQ, D_MODEL)
    y_sample = y_s.reshape(DEC_BATCH, DEC_SEQ, D_MODEL)
    def seq_tail(a, n, c0, c1):
        return jnp.stack([a[(b + 1) * SEQ - n:(b + 1) * SEQ, c0:c1] for b in range(BATCH)])

    pool_prompt = seq_tail(u, POOL_STATE, 0, D_POOL)[None]
    pool_sample = jnp.concatenate([state_pool[0], u_s[:, :, :D_POOL]], axis=1)[:, -POOL_STATE:][None]
    kv_shape = (N_KV_HEADS, HEAD_DIM)
    k_prompt = seq_tail(kn_p, WINDOW, 0, D_KV).reshape(BATCH, WINDOW, *kv_shape)[None]
    v_prompt = seq_tail(u, WINDOW, D_POOL + D_ATTN + D_KV, D_IN_AB).reshape(BATCH, WINDOW, *kv_shape)[None]
    k_sample = jnp.concatenate([cache_k[0], kn_new.reshape(DEC_BATCH, DEC_SEQ, *kv_shape)], axis=1)[:, -WINDOW:][None]
    v_sample = jnp.concatenate([cache_v[0], v_new.reshape(DEC_BATCH, DEC_SEQ, *kv_shape)], axis=1)[:, -WINDOW:][None]
    return (y_prompt, y_sample, pool_prompt, pool_sample, k_prompt, k_sample, v_prompt, v_sample,
            hg_p[None], hg_s[None])
```

```python
import functools
import math

import numpy as np
import jax
import jax.numpy as jnp
from jax import lax
from jax.experimental import pallas as pl
from jax.experimental.pallas import tpu as pltpu

f32 = jnp.float32
bf16 = jnp.bfloat16
i32 = jnp.int32

D_MODEL = 2048
BATCH = 4
SEQ = 2048
DEPTH = 2
DEC_BATCH = 32
DEC_SEQ = 4
EPS = 1e-6
POOL_WINDOWS = (2, 4, 8, 16)
D_POOL = 512
POOL_GROUP = 128
POOL_STATE = 15
HEAD_DIM = 64
N_Q_HEADS = 24
N_KV_HEADS = 4
Q_PER_KV = 6
WINDOW = 128
D_ATTN = 1536
D_KV = 256
D_IN_AB = 2560
C_DK = 128
C_HEADS = 16
D_C = 2048
N_EXPERTS = 16
EXP_PER_GROUP = 4
D_FF = 512
NEG = -1e30
LOG2E = math.log2(math.e)

LANES = 128
VMEM_LIMIT = 56 * 1024 * 1024

NP = BATCH * SEQ
NS = DEC_BATCH * DEC_SEQ
N_REAL = NP + NS
TM = 512
M_ALL = ((N_REAL + TM - 1) // TM) * TM
N_TILES = M_ALL // TM
NPT = NP // TM
TPB = SEQ // TM
SP = 16
NSP = DEC_BATCH * SP

TME = 256
R_MAX = ((2 * N_REAL + N_EXPERTS * (TME - 1)) // TME + 1) * TME
NT_E = R_MAX // TME
TMC = 128
TMD = 640

FACTORED_BLOCK = 32
DECAY_CLAMP = 60.0


def _cparams(sem, vmem=VMEM_LIMIT):
    return pltpu.CompilerParams(dimension_semantics=sem, vmem_limit_bytes=vmem)


def _silu(x):
    return x * jax.nn.sigmoid(x)


def _nt_dot(a, b, **kw):
    return lax.dot_general(a, b, (((1,), (1,)), ((), ())), preferred_element_type=f32, **kw)


def _tn_dot(a, b):
    return lax.dot_general(a, b, (((0,), (0,)), ((), ())), preferred_element_type=f32)


def _ada_body(c_ref, w_ref, b_ref, o_ref):
    s = _silu(c_ref[...]).astype(bf16)
    o_ref[...] = jnp.dot(s, w_ref[...].astype(bf16), preferred_element_type=f32) + b_ref[...]


def _ada_mod(c_all, ada_w, ada_b):
    depth, d, e = ada_w.shape
    cr = c_all.shape[0]
    tn = 1024
    return pl.pallas_call(
        _ada_body,
        grid=(depth, e // tn),
        in_specs=[
            pl.BlockSpec((cr, d), lambda l, j: (0, 0)),
            pl.BlockSpec((None, d, tn), lambda l, j: (l, 0, j)),
            pl.BlockSpec((None, 1, tn), lambda l, j: (l, 0, j)),
        ],
        out_specs=pl.BlockSpec((None, cr, tn), lambda l, j: (l, 0, j)),
        out_shape=jax.ShapeDtypeStruct((depth, cr, e), f32),
        compiler_params=_cparams(("arbitrary", "arbitrary")),
        name="ada_mod",
    )(c_all, ada_w, ada_b.reshape(depth, 1, e))


def _mod_specs(layer, k):
    p = pl.BlockSpec((None, None, None, 1, D_MODEL),
                     lambda i: (layer, k, jnp.minimum(i // TPB, BATCH - 1), 0, 0))
    s = pl.BlockSpec((None, None, NS, D_MODEL), lambda i: (layer, k, 0, 0))
    return p, s


def _norm_rows(x, g):
    return x * lax.rsqrt(jnp.mean(x * x, axis=-1, keepdims=True) + EPS) * g


def _x_specs(x, width, col):
    if len(x) == 1:
        return [pl.BlockSpec((TM, width), lambda *g: (g[-1], col(*g)))]
    return [pl.BlockSpec((TM, width), lambda *g: (jnp.minimum(g[-1], NPT - 1), col(*g))),
            pl.BlockSpec((NS, width), lambda *g: (0, col(*g)))]


def _normmod_body(*refs):
    *x_refs, g_ref, scp_ref, shp_ref, scs_ref, shs_ref, h_ref = refs
    i = pl.program_id(0)

    @pl.when(i < NPT)
    def _():
        y = _norm_rows(x_refs[0][...], g_ref[...])
        h_ref[...] = (y * (1.0 + scp_ref[...]) + shp_ref[...]).astype(h_ref.dtype)

    @pl.when(i >= NPT)
    def _():
        y = _norm_rows(x_refs[-1][:NS], g_ref[...])
        h_ref[:NS] = (y * (1.0 + scs_ref[...]) + shs_ref[...]).astype(h_ref.dtype)
        h_ref[NS:] = jnp.zeros((TM - NS, D_MODEL), h_ref.dtype)


def _normmod(x, g, modp, mods, layer, k_shift, k_scale):
    scp, scs = _mod_specs(layer, k_scale)
    shp, shs = _mod_specs(layer, k_shift)
    return pl.pallas_call(
        _normmod_body,
        grid=(N_TILES,),
        in_specs=_x_specs(x, D_MODEL, lambda i: 0) + [pl.BlockSpec((1, D_MODEL), lambda i: (0, 0)),
                                                       scp, shp, scs, shs],
        out_specs=pl.BlockSpec((TM, D_MODEL), lambda i: (i, 0)),
        out_shape=jax.ShapeDtypeStruct((M_ALL, D_MODEL), bf16),
        compiler_params=_cparams(("arbitrary",)),
        name="normmod",
    )(*x, g.reshape(1, D_MODEL), modp, modp, mods, mods)


def _proj_body(h_ref, w_ref, o_ref, wb_ref):
    @pl.when(pl.program_id(1) == 0)
    def _():
        wb_ref[...] = w_ref[...].astype(bf16)

    o_ref[...] = jnp.dot(h_ref[...], wb_ref[...], preferred_element_type=f32)


def _proj(h_all, w, tn):
    k, e = w.shape
    return pl.pallas_call(
        _proj_body,
        grid=(e // tn, N_TILES),
        in_specs=[pl.BlockSpec((TM, k), lambda j, i: (i, 0)),
                  pl.BlockSpec((k, tn), lambda j, i: (0, j))],
        out_specs=pl.BlockSpec((TM, tn), lambda j, i: (i, j)),
        out_shape=jax.ShapeDtypeStruct((M_ALL, e), f32),
        scratch_shapes=[pltpu.VMEM((k, tn), bf16)],
        compiler_params=_cparams(("arbitrary", "arbitrary")),
        name="proj",
    )(h_all, w)


def _oproj_body(*refs, n_lhs):
    ap_refs = refs[:n_lhs]
    as_refs = refs[n_lhs:2 * n_lhs]
    w_ref, *x_refs, gp_ref, gs_ref, o_ref, wb_ref = refs[2 * n_lhs:]
    i = pl.program_id(1)

    @pl.when(i == 0)
    def _():
        wb_ref[...] = w_ref[...].astype(bf16)

    def matmul(a_refs, rows):
        off, acc = 0, None
        for a in a_refs:
            kk = a.shape[1]
            part = jnp.dot(a[:rows], wb_ref[off:off + kk, :], preferred_element_type=f32)
            acc = part if acc is None else acc + part
            off += kk
        return acc

    @pl.when(i < NPT)
    def _():
        o_ref[...] = x_refs[0][...] + gp_ref[...] * matmul(ap_refs, TM)

    @pl.when(i >= NPT)
    def _():
        o_ref[:NS] = x_refs[-1][:NS] + gs_ref[...] * matmul(as_refs, NS)
        o_ref[NS:] = jnp.zeros((TM - NS, o_ref.shape[1]), f32)


def _oproj(a_prompt, a_sample, w, x, modp, mods, layer, k_gate):
    n_lhs = len(a_prompt)
    kdim, e = w.shape
    tn = 1024
    in_specs = []
    for a in a_prompt:
        in_specs.append(pl.BlockSpec((TM, a.shape[1]), lambda j, i: (jnp.minimum(i, NPT - 1), 0)))
    for a in a_sample:
        in_specs.append(pl.BlockSpec((NS, a.shape[1]), lambda j, i: (0, 0)))
    in_specs += [pl.BlockSpec((kdim, tn), lambda j, i: (0, j))] + _x_specs(x, tn, lambda j, i: j) + [
        pl.BlockSpec((None, None, None, 1, tn),
                     lambda j, i: (layer, k_gate, jnp.minimum(i // TPB, BATCH - 1), 0, j)),
        pl.BlockSpec((None, None, NS, tn), lambda j, i: (layer, k_gate, 0, j)),
    ]
    return pl.pallas_call(
        functools.partial(_oproj_body, n_lhs=n_lhs),
        grid=(e // tn, N_TILES),
        in_specs=in_specs,
        out_specs=pl.BlockSpec((TM, tn), lambda j, i: (i, j)),
        out_shape=jax.ShapeDtypeStruct((M_ALL, e), f32),
        scratch_shapes=[pltpu.VMEM((kdim, tn), bf16)],
        compiler_params=_cparams(("arbitrary", "arbitrary")),
        name="oproj",
    )(*a_prompt, *a_sample, w, *x, modp, mods)


def _head_norm(x, w_row, scale):
    lane = lax.broadcasted_iota(i32, x.shape, 1)
    lo = lane < HEAD_DIM
    ss = x * x
    m_lo = jnp.sum(jnp.where(lo, ss, 0.0), axis=-1, keepdims=True)
    m_hi = jnp.sum(jnp.where(lo, 0.0, ss), axis=-1, keepdims=True)
    r = jnp.where(lo, lax.rsqrt(m_lo * (1.0 / HEAD_DIM) + EPS), lax.rsqrt(m_hi * (1.0 / HEAD_DIM) + EPS))
    return x * r * (w_row * scale)


def _mix0_pre_body(cur_ref, halo_ref, q0_ref, q1_ref, q2_ref, k_ref, pw_ref, ps_ref, qw_ref, kw_ref,
                   pool_ref, qn_ref, kn_ref, ext_ref, *, tp, tiles_per_seq, full_count):
    i = pl.program_id(0)
    halo = halo_ref[...]
    if not full_count:
        halo = jnp.where(i % tiles_per_seq == 0, 0.0, halo)
    ext_ref[0:16, :] = halo
    ext_ref[16:16 + tp, :] = cur_ref[...]
    t_pos = (i % tiles_per_seq) * tp + lax.broadcasted_iota(i32, (tp, 1), 0)
    for g, w in enumerate(POOL_WINDOWS):
        ch = slice(g * POOL_GROUP, (g + 1) * POOL_GROUP)
        acc = ext_ref[16:16 + tp, ch]
        for dlt in range(1, w):
            acc = acc + ext_ref[16 - dlt:16 - dlt + tp, ch]
        if full_count:
            mean = acc * (1.0 / w)
        else:
            cnt = jnp.minimum(t_pos + 1, w).astype(f32)
            mean = acc / cnt
        d = mean - ext_ref[16:16 + tp, ch]
        y = jnp.dot(d.astype(bf16), pw_ref[g].astype(bf16), preferred_element_type=f32)
        pool_ref[:, ch] = (y * ps_ref[:, ch]).astype(pool_ref.dtype)
    for c3, q_ref in enumerate((q0_ref, q1_ref, q2_ref)):
        for c in range(4):
            sl = slice(c * LANES, (c + 1) * LANES)
            dst = slice((c3 * 4 + c) * LANES, (c3 * 4 + c + 1) * LANES)
            qn_ref[:, dst] = _head_norm(q_ref[:, sl], qw_ref[...], LOG2E * HEAD_DIM ** -0.5).astype(qn_ref.dtype)
    for c in range(2):
        sl = slice(c * LANES, (c + 1) * LANES)
        kn_ref[:, sl] = _head_norm(k_ref[:, sl], kw_ref[...], 1.0)


def _mix0_pre(u, halo_src, halo_map, rows, tp, tiles_per_seq, full_count, pool_w, pool_scale, q_norm, k_norm):
    qw = jnp.tile(q_norm.reshape(1, HEAD_DIM), (1, 2))
    kw = jnp.tile(k_norm.reshape(1, HEAD_DIM), (1, 2))
    body = functools.partial(_mix0_pre_body, tp=tp, tiles_per_seq=tiles_per_seq, full_count=full_count)
    return pl.pallas_call(
        body,
        grid=(rows // tp,),
        in_specs=[
            pl.BlockSpec((tp, D_POOL), lambda i: (i, 0)),
            pl.BlockSpec((16, D_POOL), halo_map),
            pl.BlockSpec((tp, 512), lambda i: (i, 1)),
            pl.BlockSpec((tp, 512), lambda i: (i, 2)),
            pl.BlockSpec((tp, 512), lambda i: (i, 3)),
            pl.BlockSpec((tp, D_KV), lambda i: (i, (D_POOL + D_ATTN) // D_KV)),
            pl.BlockSpec((4, POOL_GROUP, POOL_GROUP), lambda i: (0, 0, 0)),
            pl.BlockSpec((1, D_POOL), lambda i: (0, 0)),
            pl.BlockSpec((1, LANES), lambda i: (0, 0)),
            pl.BlockSpec((1, LANES), lambda i: (0, 0)),
        ],
        out_specs=[
            pl.BlockSpec((tp, D_POOL), lambda i: (i, 0)),
            pl.BlockSpec((tp, D_ATTN), lambda i: (i, 0)),
            pl.BlockSpec((tp, D_KV), lambda i: (i, 0)),
        ],
        out_shape=[
            jax.ShapeDtypeStruct((rows, D_POOL), bf16),
            jax.ShapeDtypeStruct((rows, D_ATTN), bf16),
            jax.ShapeDtypeStruct((rows, D_KV), f32),
        ],
        scratch_shapes=[pltpu.VMEM((16 + tp, D_POOL), f32)],
        compiler_params=_cparams(("arbitrary",)),
        name="mix0_pre",
    )(u, halo_src, u, u, u, u, pool_w, pool_scale.reshape(1, D_POOL), qw, kw)


def _alibi_slopes(n):
    def pow2_slopes(m):
        start = 2.0 ** (-8.0 / m)
        return [start ** (i + 1) for i in range(m)]
    if n & (n - 1) == 0:
        s = pow2_slopes(n)
    else:
        c = 2 ** int(math.floor(math.log2(n)))
        s = pow2_slopes(c) + pow2_slopes(2 * c)[0::2][: n - c]
    return [float(v) for v in np.asarray(s, dtype=np.float32)]


_SLOPES = _alibi_slopes(N_Q_HEADS)


def _attn_body(sink_ref, q_ref, kp_ref, kc_ref, vp_ref, vc_ref, o_ref, bias_ref, *, tq, prompt, group):
    first = pl.program_id(0) == 0
    if prompt:
        first = first & (pl.program_id(1) == 0)

    @pl.when(first)
    def _():
        qi = lax.broadcasted_iota(i32, (tq, WINDOW), 0)
        kj = lax.broadcasted_iota(i32, (tq, WINDOW), 1)
        dist_p = (qi + WINDOW - kj).astype(f32)
        dist_c = (qi - kj).astype(f32)
        neg_p = jnp.where(qi + WINDOW - kj < WINDOW, 0.0, NEG)
        neg_c = jnp.where(qi - kj >= 0, 0.0, NEG)
        for head in range(N_Q_HEADS):
            bias_ref[head, 0] = neg_p - (_SLOPES[head] * LOG2E) * dist_p
            bias_ref[head, 1] = neg_c - (_SLOPES[head] * LOG2E) * dist_c

    lane = lax.broadcasted_iota(i32, (WINDOW, LANES), 1)
    lo = lane < HEAD_DIM
    qlane = lax.broadcasted_iota(i32, (tq, LANES), 1) < HEAD_DIM

    for g in range(group):
        q_rows_g = slice(g * tq, (g + 1) * tq)

        def kv(ref, cols):
            return ref[:, cols] if group == 1 else ref[g, :, cols]

        for j in range(N_KV_HEADS):
            pair = slice((j // 2) * LANES, (j // 2 + 1) * LANES)
            own_lo = (j % 2 == 0)

            def dup(kpair):
                sw = pltpu.roll(kpair, HEAD_DIM, 1)
                return jnp.where(lo, kpair, sw) if own_lo else jnp.where(lo, sw, kpair)

            def halves(vpair):
                own = jnp.where(lo, vpair, 0.0) if own_lo else jnp.where(lo, 0.0, vpair)
                sw = pltpu.roll(own, HEAD_DIM, 1)
                return (own, sw) if own_lo else (sw, own)

            kk_p = dup(kv(kp_ref, pair)).astype(bf16)
            kk_c = dup(kv(kc_ref, pair)).astype(bf16)
            v_p = [h.astype(bf16) for h in halves(kv(vp_ref, pair))]
            v_c = [h.astype(bf16) for h in halves(kv(vc_ref, pair))]

            heads = [(hf, c) for hf in (0, 1) for c in range(3)]
            q_rows = []
            for hf, c in heads:
                ch = slice((3 * j + c) * LANES, (3 * j + c + 1) * LANES)
                qc = q_ref[q_rows_g, ch].astype(f32)
                q_rows.append(jnp.where(qlane, qc, 0.0) if hf == 0 else jnp.where(qlane, 0.0, qc))
            q_st = jnp.concatenate(q_rows, axis=0).astype(bf16)
            s_p = _nt_dot(q_st, kk_p)
            s_c = _nt_dot(q_st, kk_c)
            for c in range(3):
                out = None
                for hf in (0, 1):
                    head = 6 * j + 2 * c + hf
                    rows = slice((hf * 3 + c) * tq, (hf * 3 + c + 1) * tq)
                    bias_p = bias_ref[head, 0]
                    if prompt:
                        bias_p = jnp.where(pl.program_id(1) > 0, bias_p, NEG)
                    sp = s_p[rows] + bias_p
                    sc = s_c[rows] + bias_ref[head, 1]
                    sink = sink_ref[head] * LOG2E
                    m = jnp.maximum(jnp.max(jnp.maximum(sp, sc), axis=-1, keepdims=True), sink)
                    ep = jnp.exp2(sp - m)
                    ec = jnp.exp2(sc - m)
                    den = jnp.sum(ep + ec, axis=-1, keepdims=True) + jnp.exp2(sink - m)
                    o = (jnp.dot(ep.astype(bf16), v_p[hf], preferred_element_type=f32)
                         + jnp.dot(ec.astype(bf16), v_c[hf], preferred_element_type=f32)) * (1.0 / den)
                    out = o if out is None else out + o
                ch = slice((3 * j + c) * LANES, (3 * j + c + 1) * LANES)
                o_ref[q_rows_g, ch] = out.astype(o_ref.dtype)


def _attn_call(sinks, q, k_arr, v_arr, specs, grid, rows, tq, prompt, group):
    body = functools.partial(_attn_body, tq=tq, prompt=prompt, group=group)
    q_spec, kp_spec, kc_spec, vp_spec, vc_spec, o_spec = specs
    return pl.pallas_call(
        body,
        grid=grid,
        in_specs=[pl.BlockSpec(memory_space=pltpu.SMEM), q_spec, kp_spec, kc_spec, vp_spec, vc_spec],
        out_specs=o_spec,
        out_shape=jax.ShapeDtypeStruct((rows, D_ATTN), bf16),
        scratch_shapes=[pltpu.VMEM((N_Q_HEADS, 2, tq, WINDOW), f32)],
        compiler_params=_cparams(("arbitrary",) * len(grid)),
        name="attn",
    )(sinks, q, k_arr, k_arr, v_arr, v_arr)


def _hgrn_inputs(q_ref, f_ref, i_ref, lb_ref, sl, *, L, n_valid):
    lb = lb_ref[:, sl]
    fp = f_ref[:, sl]
    q = _silu(q_ref[:, sl])
    t = jnp.exp(-jnp.abs(fp))
    inv = 1.0 / (1.0 + t)
    pos = fp >= 0.0
    logf = jnp.log(lb + (1.0 - lb) * jnp.where(pos, inv, t * inv))
    kk = (1.0 - lb) * jnp.where(pos, t * inv, inv)
    if n_valid < L:
        valid = lax.broadcasted_iota(i32, (L, 1), 0) < n_valid
        logf = jnp.where(valid, logf, 0.0)
        kk = jnp.where(valid, kk, 0.0)
    return q, kk, i_ref[:, sl], logf


def _cum_decay(logf, L):
    row = lax.broadcasted_iota(i32, (L, L), 0)
    col = lax.broadcasted_iota(i32, (L, L), 1)
    tri = (row >= col).astype(bf16)
    h1 = logf.astype(bf16)
    r1 = logf - h1.astype(f32)
    h2 = r1.astype(bf16)
    h3 = (r1 - h2.astype(f32)).astype(bf16)
    return (jnp.dot(tri, h1, preferred_element_type=f32) + jnp.dot(tri, h2, preferred_element_type=f32)
            + jnp.dot(tri, h3, preferred_element_type=f32))


def _sub_ref(b, lo):
    return b[lo - 1:lo] if lo > 0 else jnp.zeros((1, b.shape[1]), f32)


def _factored_operands(q, kk, b, *, L, CF):
    rowid = lax.broadcasted_iota(i32, (L, 1), 0)
    qts, kts = [], []
    for lo in range(0, L, CF):
        r = _sub_ref(b, lo)
        qts.append((q[lo:lo + CF] * jnp.exp(b[lo:lo + CF] - r)).astype(bf16))
        kts.append(jnp.where(rowid < lo + CF, kk * jnp.exp(jnp.minimum(r - b, DECAY_CLAMP)), 0.0).astype(bf16))
    return qts, kts


def _decay_span(b, *, L, CF):
    span = None
    for lo in range(0, L, CF):
        d = _sub_ref(b, lo) - b[lo + CF - 1:lo + CF]
        span = d if span is None else jnp.maximum(span, d)
    return span


def _intra_pairwise(q, kk, v, b, *, L, C):
    vb = v.astype(bf16)
    rowid = lax.broadcasted_iota(i32, (L, 1), 0)
    tloc = lax.broadcasted_iota(i32, (C, 1), 0)
    outs = []
    for lo in range(0, L, C):
        bi, qi, ki, vi = b[lo:lo + C], q[lo:lo + C], kk[lo:lo + C], v[lo:lo + C]
        if lo > 0:
            r = _sub_ref(b, lo)
            qt = (qi * jnp.exp(bi - r)).astype(bf16)
            kt = jnp.where(rowid < lo, kk * jnp.exp(jnp.minimum(r - b, 0.0)), 0.0).astype(bf16)
            oi = jnp.dot(_nt_dot(qt, kt).astype(bf16), vb, preferred_element_type=f32)
        else:
            oi = jnp.zeros((C, LANES), f32)
        for s in range(C):
            e = jnp.exp(jnp.minimum(bi - bi[s:s + 1], 0.0))
            a = jnp.sum(qi * ki[s:s + 1] * e, axis=-1, keepdims=True)
            oi = oi + jnp.where(tloc >= s, a, 0.0) * vi[s:s + 1]
        outs.append(oi)
    return outs[0] if len(outs) == 1 else jnp.concatenate(outs, axis=0)


def _hgrn_body(*refs, L, C, hpb, n_valid, has_state):
    if has_state:
        q_ref, f_ref, i_ref, g_ref, lb_ref, on_ref, s0_ref, og_ref, so_ref, st_ref, inter_ref = refs
    else:
        q_ref, f_ref, i_ref, g_ref, lb_ref, on_ref, og_ref, so_ref, st_ref, inter_ref = refs
    c = pl.program_id(2)
    nc = pl.num_programs(2)
    CF = min(L, FACTORED_BLOCK)

    @pl.when(c == 0)
    def _():
        for hh in range(hpb):
            st_ref[hh] = s0_ref[hh].T if has_state else jnp.zeros((C_DK, C_DK), f32)

    def finish(o, sl):
        o = o * lax.rsqrt(jnp.mean(o * o, axis=-1, keepdims=True) + EPS) * on_ref[...]
        og_ref[:, sl] = (o * _silu(g_ref[:, sl])).astype(og_ref.dtype)

    q, kk, v, logf = _hgrn_inputs(q_ref, f_ref, i_ref, lb_ref, slice(0, hpb * LANES), L=L, n_valid=n_valid)
    b = _cum_decay(logf, L)
    vb = v.astype(bf16)
    q_hat = (q * jnp.exp(b)).astype(bf16)
    bl = b[L - 1:L]
    k_end = (kk * jnp.exp(bl - b)).astype(bf16)
    st_decay = jnp.exp(bl)
    qts, kts = _factored_operands(q, kk, b, L=L, CF=CF)
    span = _decay_span(b, L=L, CF=CF)
    causal = lax.broadcasted_iota(i32, (L, L), 0) >= lax.broadcasted_iota(i32, (L, L), 1)
    for hh in range(hpb):
        sl = slice(hh * LANES, (hh + 1) * LANES)
        st = st_ref[hh]
        o_inter = _nt_dot(q_hat[:, sl], st.astype(bf16))
        st_ref[hh] = st * st_decay[:, sl] + _tn_dot(vb[:, sl], k_end[:, sl])
        inter_ref[hh] = o_inter
        blocks = [_nt_dot(qt[:, sl], kt[:, sl]) for qt, kt in zip(qts, kts)]
        a = blocks[0] if len(blocks) == 1 else jnp.concatenate(blocks, axis=0)
        a = jnp.where(causal, a, 0.0).astype(bf16)
        finish(jnp.dot(a, vb[:, sl], preferred_element_type=f32) + o_inter, sl)

    @pl.when(jnp.max(span) >= DECAY_CLAMP)
    def _():
        for hh in range(hpb):
            sl = slice(hh * LANES, (hh + 1) * LANES)
            q, kk, v, logf = _hgrn_inputs(q_ref, f_ref, i_ref, lb_ref, sl, L=L, n_valid=n_valid)
            b = _cum_decay(logf, L)
            finish(_intra_pairwise(q, kk, v, b, L=L, C=C) + inter_ref[hh], sl)

    @pl.when(c == nc - 1)
    def _():
        for hh in range(hpb):
            so_ref[hh] = st_ref[hh].T


def _hgrn(u, lb, o_norm, s0, *, nb, nc, L, C, hpb, n_valid, rows):
    nh = C_HEADS // hpb
    w = hpb * LANES
    has_state = s0 is not None
    body = functools.partial(_hgrn_body, L=L, C=C, hpb=hpb, n_valid=n_valid, has_state=has_state)

    def uspec(part):
        return pl.BlockSpec((L, w), lambda b, h, c: (b * nc + c, part * nh + h))

    in_specs = [uspec(0), uspec(1), uspec(2), uspec(3),
                pl.BlockSpec((1, w), lambda b, h, c: (0, h)),
                pl.BlockSpec((1, LANES), lambda b, h, c: (0, 0))]
    args = [u, u, u, u, lb.reshape(1, D_C), o_norm.reshape(1, LANES)]
    if has_state:
        in_specs.append(pl.BlockSpec((None, hpb, C_DK, C_DK), lambda b, h, c: (b, h, 0, 0)))
        args.append(s0)
    return pl.pallas_call(
        body,
        grid=(nb, nh, nc),
        in_specs=in_specs,
        out_specs=[pl.BlockSpec((L, w), lambda b, h, c: (b * nc + c, h)),
                   pl.BlockSpec((None, hpb, C_DK, C_DK), lambda b, h, c: (b, h, 0, 0))],
        out_shape=[jax.ShapeDtypeStruct((rows, D_C), bf16),
                   jax.ShapeDtypeStruct((nb, C_HEADS, C_DK, C_DK), f32)],
        scratch_shapes=[pltpu.VMEM((hpb, C_DK, C_DK), f32), pltpu.VMEM((hpb, L, LANES), f32)],
        compiler_params=_cparams(("arbitrary", "arbitrary", "arbitrary")),
        name="hgrn",
    )(*args)


def _moe_pre_body(x_ref, g_ref, scp_ref, shp_ref, scs_ref, shs_ref, wrt_ref, rb_ref,
                  h_ref, idx_ref, wts_ref, rank_ref, cnt_ref, carry):
    i = pl.program_id(0)

    @pl.when(i == 0)
    def _():
        carry[...] = jnp.zeros_like(carry)

    y = _norm_rows(x_ref[...], g_ref[...])

    @pl.when(i < NPT)
    def _():
        h_ref[...] = y * (1.0 + scp_ref[...]) + shp_ref[...]

    @pl.when(i >= NPT)
    def _():
        h_ref[:NS] = y[:NS] * (1.0 + scs_ref[...]) + shs_ref[...]
        h_ref[NS:] = jnp.zeros((TM - NS, D_MODEL), f32)

    h = h_ref[...]

    w_r = wrt_ref[...]
    w_hi = w_r.astype(bf16)
    w_lo = (w_r - w_hi.astype(f32)).astype(bf16)
    h_hi = h.astype(bf16)
    h_lo = (h - h_hi.astype(f32)).astype(bf16)
    logits = _nt_dot(w_hi, h_hi) + _nt_dot(w_hi, h_lo) + _nt_dot(w_lo, h_hi)
    scores = jax.nn.sigmoid(logits)
    sel = scores + rb_ref[...]
    best = jnp.zeros((1, TM), i32)
    best_v = None
    for g in range(N_EXPERTS // EXP_PER_GROUP):
        r = [sel[EXP_PER_GROUP * g + t:EXP_PER_GROUP * g + t + 1] for t in range(EXP_PER_GROUP)]
        top2 = None
        for a in range(EXP_PER_GROUP):
            for bb in range(a + 1, EXP_PER_GROUP):
                s2 = r[a] + r[bb]
                top2 = s2 if top2 is None else jnp.maximum(top2, s2)
        if g == 0:
            best_v = top2
        else:
            upd = top2 > best_v
            best = jnp.where(upd, g, best)
            best_v = jnp.where(upd, top2, best_v)
    eidx = lax.broadcasted_iota(i32, (N_EXPERTS, TM), 0)
    masked = jnp.where(jnp.right_shift(eidx, 2) == best, sel, NEG)
    m1 = jnp.max(masked, axis=0, keepdims=True)
    i1 = jnp.min(jnp.where(masked == m1, eidx, N_EXPERTS), axis=0, keepdims=True)
    masked2 = jnp.where(eidx == i1, -jnp.inf, masked)
    m2 = jnp.max(masked2, axis=0, keepdims=True)
    i2 = jnp.min(jnp.where(masked2 == m2, eidx, N_EXPERTS), axis=0, keepdims=True)
    w1 = jnp.sum(jnp.where(eidx == i1, scores, 0.0), axis=0, keepdims=True)
    w2 = jnp.sum(jnp.where(eidx == i2, scores, 0.0), axis=0, keepdims=True)
    den = w1 + w2
    idx_ref[0:1, :] = i1
    idx_ref[1:2, :] = i2
    wts_ref[0:1, :] = w1 / den
    wts_ref[1:2, :] = w2 / den

    colg = i * TM + lax.broadcasted_iota(i32, (1, TM), 1)
    oh = jnp.where(((eidx == i1) | (eidx == i2)) & (colg < N_REAL), 1.0, 0.0)
    rr = lax.broadcasted_iota(i32, (TM, TM), 0)
    cc = lax.broadcasted_iota(i32, (TM, TM), 1)
    before = (rr < cc).astype(bf16)
    rank = jnp.dot(oh.astype(bf16), before, preferred_element_type=f32) + carry[:, 0:1]
    rank_ref[0:1, :] = jnp.sum(jnp.where(eidx == i1, rank, 0.0), axis=0, keepdims=True).astype(i32)
    rank_ref[1:2, :] = jnp.sum(jnp.where(eidx == i2, rank, 0.0), axis=0, keepdims=True).astype(i32)
    carry[...] = carry[...] + jnp.sum(oh, axis=1, keepdims=True)
    cnt_ref[...] = carry[...]


def _moe_pre(x_all, g, modp, mods, layer, router_wt, router_b):
    scp, scs = _mod_specs(layer, 4)
    shp, shs = _mod_specs(layer, 3)
    return pl.pallas_call(
        _moe_pre_body,
        grid=(N_TILES,),
        in_specs=[pl.BlockSpec((TM, D_MODEL), lambda i: (i, 0)),
                  pl.BlockSpec((1, D_MODEL), lambda i: (0, 0)),
                  scp, shp, scs, shs,
                  pl.BlockSpec((N_EXPERTS, D_MODEL), lambda i: (0, 0)),
                  pl.BlockSpec((N_EXPERTS, 1), lambda i: (0, 0))],
        out_specs=[pl.BlockSpec((TM, D_MODEL), lambda i: (i, 0)),
                   pl.BlockSpec((2, TM), lambda i: (0, i)),
                   pl.BlockSpec((2, TM), lambda i: (0, i)),
                   pl.BlockSpec((2, TM), lambda i: (0, i)),
                   pl.BlockSpec((N_EXPERTS, LANES), lambda i: (0, 0))],
        out_shape=[jax.ShapeDtypeStruct((M_ALL, D_MODEL), f32),
                   jax.ShapeDtypeStruct((2, M_ALL), i32),
                   jax.ShapeDtypeStruct((2, M_ALL), f32),
                   jax.ShapeDtypeStruct((2, M_ALL), i32),
                   jax.ShapeDtypeStruct((N_EXPERTS, LANES), f32)],
        scratch_shapes=[pltpu.VMEM((N_EXPERTS, LANES), f32)],
        compiler_params=_cparams(("arbitrary",)),
        name="moe_pre",
    )(x_all, g.reshape(1, D_MODEL), modp, modp, mods, mods, router_wt, router_b.reshape(N_EXPERTS, 1))


def _dispatch_body(pos_ref, fill_lo_ref, fill_hi_ref, nu_ref, h_ref, hs_hbm, zero_buf, sem, tile_sem):
    i = pl.program_id(0)
    base = i * TMD

    def row_copy(src, dst_row):
        return pltpu.make_async_copy(src, hs_hbm.at[pl.ds(dst_row, 1)], sem)

    def wait_row(n, carry):
        row_copy(h_ref.at[pl.ds(0, 1)], 0).wait()
        return carry

    @pl.when(i == 0)
    def _():
        zero_buf[...] = jnp.zeros_like(zero_buf)
        zero_row = zero_buf.at[pl.ds(0, 1)]

        def tile_copy(t):
            return pltpu.make_async_copy(zero_buf, hs_hbm.at[pl.ds(t * TME, TME)], tile_sem)

        def tail_start(t, carry):
            tile_copy(t).start()
            return carry

        def tail_wait(t, carry):
            tile_copy(t).wait()
            return carry

        def fill_expert(e, carry):
            def fill(r, c2):
                row_copy(zero_row, r).start()
                return c2
            lax.fori_loop(fill_lo_ref[e], fill_hi_ref[e], fill, 0)
            lax.fori_loop(fill_lo_ref[e], fill_hi_ref[e], wait_row, 0)
            return carry

        lax.fori_loop(nu_ref[0], NT_E, tail_start, 0)
        lax.fori_loop(0, N_EXPERTS, fill_expert, 0)
        lax.fori_loop(nu_ref[0], NT_E, tail_wait, 0)

    def issue(r, carry):
        src = h_ref.at[pl.ds(r, 1)]
        row_copy(src, pos_ref[base + r]).start()
        row_copy(src, pos_ref[M_ALL + base + r]).start()
        return carry

    lax.fori_loop(0, TMD, issue, 0, unroll=8)
    lax.fori_loop(0, 2 * TMD, wait_row, 0, unroll=8)


def _dispatch(pos_flat, fill_lo, fill_hi, n_used, h):
    return pl.pallas_call(
        _dispatch_body,
        grid_spec=pltpu.PrefetchScalarGridSpec(
            num_scalar_prefetch=4,
            grid=(N_REAL // TMD,),
            in_specs=[pl.BlockSpec((TMD, D_MODEL), lambda i, *_: (i, 0))],
            out_specs=pl.BlockSpec(memory_space=pl.ANY),
            scratch_shapes=[pltpu.VMEM((TME, D_MODEL), f32), pltpu.SemaphoreType.DMA,
                            pltpu.SemaphoreType.DMA],
        ),
        out_shape=jax.ShapeDtypeStruct((R_MAX, D_MODEL), f32),
        compiler_params=_cparams(("arbitrary",)),
        name="moe_dispatch",
    )(pos_flat, fill_lo, fill_hi, n_used, h)


def _expert_body(te_ref, nu_ref, hs_ref, w1_ref, w3_ref, w2_ref, y_ref, w1b, w3b, w2b):
    t = pl.program_id(0)

    @pl.when(t < nu_ref[0])
    def _():
        changed = (t == 0) | (te_ref[t] != te_ref[jnp.maximum(t - 1, 0)])

        @pl.when(changed)
        def _():
            w1b[...] = w1_ref[...].astype(bf16)
            w3b[...] = w3_ref[...].astype(bf16)
            w2b[...] = w2_ref[...].astype(bf16)

        hb = hs_ref[...].astype(bf16)
        a = jnp.dot(hb, w1b[...], preferred_element_type=f32)
        b = jnp.dot(hb, w3b[...], preferred_element_type=f32)
        hid = (_silu(a) * b).astype(bf16)
        y_ref[...] = jnp.dot(hid, w2b[...], preferred_element_type=f32)

    @pl.when(t >= nu_ref[0])
    def _():
        y_ref[...] = jnp.zeros_like(y_ref)


def _experts(tile_expert, n_used, hs, w1, w3, w2, layer):
    def row_map(t, te, nu):
        return (jnp.minimum(t, nu[0] - 1), 0)

    def out_map(t, te, nu):
        return (t, 0)

    def w_map(t, te, nu):
        return (layer, te[t], 0, 0)

    return pl.pallas_call(
        _expert_body,
        grid_spec=pltpu.PrefetchScalarGridSpec(
            num_scalar_prefetch=2,
            grid=(NT_E,),
            in_specs=[pl.BlockSpec((TME, D_MODEL), row_map),
                      pl.BlockSpec((None, None, D_MODEL, D_FF), w_map),
                      pl.BlockSpec((None, None, D_MODEL, D_FF), w_map),
                      pl.BlockSpec((None, None, D_FF, D_MODEL), w_map)],
            out_specs=pl.BlockSpec((TME, D_MODEL), out_map),
            scratch_shapes=[pltpu.VMEM((D_MODEL, D_FF), bf16), pltpu.VMEM((D_MODEL, D_FF), bf16),
                            pltpu.VMEM((D_FF, D_MODEL), bf16)],
        ),
        out_shape=jax.ShapeDtypeStruct((R_MAX, D_MODEL), f32),
        compiler_params=_cparams(("arbitrary",)),
        name="moe_experts",
    )(tile_expert, n_used, hs, w1, w3, w2)


def _combine_body(pos_ref, x_ref, w_ref, gp_ref, gs_ref, y_hbm, *refs, split):
    if split:
        yp_ref, ys_ref, buf0, buf1, sems = refs
    else:
        ng_ref, scp_ref, shp_ref, scs_ref, shs_ref, o_ref, h_ref, buf0, buf1, sems = refs
    i = pl.program_id(0)
    n_real_tiles = N_REAL // TMC

    def copies(tile, r):
        slot = tile % 2
        base = tile * TMC
        return (pltpu.make_async_copy(y_hbm.at[pl.ds(pos_ref[base + r], 1)],
                                      buf0.at[slot, pl.ds(r, 1)], sems.at[slot]),
                pltpu.make_async_copy(y_hbm.at[pl.ds(pos_ref[M_ALL + base + r], 1)],
                                      buf1.at[slot, pl.ds(r, 1)], sems.at[slot]))

    def start_tile(tile):
        def start(r, c):
            for cp in copies(tile, r):
                cp.start()
            return c
        lax.fori_loop(0, TMC, start, 0, unroll=8)

    def wait_tile(tile):
        def wait(r, c):
            for cp in copies(tile, r):
                cp.wait()
            return c
        lax.fori_loop(0, TMC, wait, 0, unroll=8)

    @pl.when(i == 0)
    def _():
        start_tile(0)

    @pl.when(i + 1 < n_real_tiles)
    def _():
        start_tile(i + 1)

    def combined(gate):
        wait_tile(i)
        slot = i % 2
        m = w_ref[:, 0:1] * buf0[slot] + w_ref[:, 1:2] * buf1[slot]
        return x_ref[...] + gate * m

    def emit(gate, dst, sc_ref, sh_ref):
        o = combined(gate)
        dst[...] = o
        if not split:
            h_ref[...] = (_norm_rows(o, ng_ref[...]) * (1.0 + sc_ref[...]) + sh_ref[...]).astype(h_ref.dtype)

    @pl.when(i < NP // TMC)
    def _():
        emit(gp_ref[...], yp_ref if split else o_ref, None if split else scp_ref, None if split else shp_ref)

    @pl.when((i >= NP // TMC) & (i < n_real_tiles))
    def _():
        emit(gs_ref[...], ys_ref if split else o_ref, None if split else scs_ref, None if split else shs_ref)

    if not split:
        @pl.when(i >= n_real_tiles)
        def _():
            o_ref[...] = jnp.zeros_like(o_ref)
            h_ref[...] = jnp.zeros_like(h_ref)


def _combine(pos_flat, x_all, wts_t, modp, mods, layer, y, next_norm_g):
    split = next_norm_g is None
    tiles_per_seq = SEQ // TMC
    n_prompt_tiles = NP // TMC

    def mod_specs(lyr, k):
        return (pl.BlockSpec((None, None, None, 1, D_MODEL),
                             lambda i, p: (lyr, k, jnp.minimum(i // tiles_per_seq, BATCH - 1), 0, 0)),
                pl.BlockSpec((None, None, NS, D_MODEL), lambda i, p: (lyr, k, 0, 0)))

    gp, gs = mod_specs(layer, 5)
    in_specs = [pl.BlockSpec((TMC, D_MODEL), lambda i, p: (i, 0)),
                pl.BlockSpec((TMC, 2), lambda i, p: (i, 0)), gp, gs,
                pl.BlockSpec(memory_space=pl.ANY)]
    args = [x_all, wts_t, modp, mods, y]
    if split:
        grid = (N_REAL // TMC,)
        out_specs = [pl.BlockSpec((TMC, D_MODEL), lambda i, p: (jnp.minimum(i, n_prompt_tiles - 1), 0)),
                     pl.BlockSpec((NS, D_MODEL), lambda i, p: (0, 0))]
        out_shape = [jax.ShapeDtypeStruct((NP, D_MODEL), f32), jax.ShapeDtypeStruct((NS, D_MODEL), f32)]
    else:
        grid = (M_ALL // TMC,)
        scp, scs = mod_specs(layer + 1, 1)
        shp, shs = mod_specs(layer + 1, 0)
        in_specs += [pl.BlockSpec((1, D_MODEL), lambda i, p: (0, 0)), scp, shp, scs, shs]
        args += [next_norm_g.reshape(1, D_MODEL), modp, modp, mods, mods]
        out_specs = [pl.BlockSpec((TMC, D_MODEL), lambda i, p: (i, 0)),
                     pl.BlockSpec((TMC, D_MODEL), lambda i, p: (i, 0))]
        out_shape = [jax.ShapeDtypeStruct((M_ALL, D_MODEL), f32), jax.ShapeDtypeStruct((M_ALL, D_MODEL), bf16)]
    return pl.pallas_call(
        functools.partial(_combine_body, split=split),
        grid_spec=pltpu.PrefetchScalarGridSpec(
            num_scalar_prefetch=1,
            grid=grid,
            in_specs=in_specs,
            out_specs=out_specs,
            scratch_shapes=[pltpu.VMEM((2, TMC, D_MODEL), f32), pltpu.VMEM((2, TMC, D_MODEL), f32),
                            pltpu.SemaphoreType.DMA((2,))],
        ),
        out_shape=out_shape,
        compiler_params=_cparams(("arbitrary",)),
        name="moe_combine",
    )(pos_flat, *args)


def _moe(x_all, g, modp, mods, layer, router_wt, router_b, w1, w3, w2, next_norm_g):
    h, idx, wts, rank, cnt = _moe_pre(x_all, g, modp, mods, layer, router_wt, router_b)
    counts = cnt[:, 0].astype(i32)
    padded = ((counts + TME - 1) // TME) * TME
    ends = jnp.cumsum(padded)
    offs = ends - padded
    real = (jnp.arange(M_ALL) < N_REAL)[None, :]
    expert_ids = jnp.arange(N_EXPERTS, dtype=i32)[:, None, None]
    seg_start = jnp.sum(jnp.where(idx[None] == expert_ids, offs[:, None, None], 0), axis=0)
    pos = jnp.where(real, seg_start + rank, 0).astype(i32)
    pos_flat = pos.reshape(2 * M_ALL)
    n_used = (ends[-1] // TME).astype(i32).reshape(1)
    tile_start = jnp.minimum(jnp.arange(NT_E, dtype=i32), n_used[0] - 1) * TME
    tile_expert = jnp.sum((tile_start[:, None] >= ends[None, :]).astype(i32), axis=1).astype(i32)
    hs = _dispatch(pos_flat, (offs + counts).astype(i32), ends.astype(i32), n_used, h)
    y = _experts(tile_expert, n_used, hs, w1, w3, w2, layer)
    return _combine(pos_flat, x_all, wts.T, modp, mods, layer, y, next_norm_g)


def _sample_rows(a, width):
    return a.reshape(DEC_BATCH, SP, width)[:, :DEC_SEQ].reshape(NS, width)


def kernel(x_prompt, x_sample, c_prompt, c_sample, state_pool, cache_k, cache_v, state_hgrn, norm_g, ada_w, ada_b, w_in_ab, pool_w, pool_scale, q_norm, k_norm, attn_sinks, w_out_ab, w_in_c, lb_param, o_norm, w_out_c, router_w, router_b, moe_w1, moe_w3, moe_w2):
    n_c = BATCH + DEC_BATCH
    c_rows = ((n_c + 7) // 8) * 8
    c_all = jnp.concatenate([c_prompt, c_sample, jnp.zeros((c_rows - n_c, D_MODEL), f32)], axis=0)
    mod = _ada_mod(c_all, ada_w, ada_b).reshape(DEPTH, c_rows, 6, D_MODEL)
    modp = mod[:, :BATCH].transpose(0, 2, 1, 3).reshape(DEPTH, 6, BATCH, 1, D_MODEL)
    mods = jnp.repeat(mod[:, BATCH:n_c].transpose(0, 2, 1, 3), DEC_SEQ, axis=2)

    x_in = (x_prompt.reshape(NP, D_MODEL), x_sample.reshape(NS, D_MODEL))
    router_wt = router_w.T

    h = _normmod(x_in, norm_g[0, 0], modp, mods, 0, 0, 1)
    u = _proj(h, w_in_ab[0], 1280)
    tp = 256
    pool_p, qn_p, kn_p = _mix0_pre(
        u, u, lambda i: (jnp.maximum(i * (tp // 16) - 1, 0), 0), NP, tp, SEQ // tp, False,
        pool_w[0], pool_scale[0], q_norm[0], k_norm[0])
    nb = SEQ // WINDOW
    blk = lambda b, n: (b * nb + n, 0)
    prv = lambda b, n: (jnp.maximum(b * nb + n - 1, 0), 0)
    vcol = (D_POOL + D_ATTN + D_KV) // D_KV
    attn_p = _attn_call(
        attn_sinks[0], qn_p, kn_p, u,
        (pl.BlockSpec((WINDOW, D_ATTN), blk), pl.BlockSpec((WINDOW, D_KV), prv), pl.BlockSpec((WINDOW, D_KV), blk),
         pl.BlockSpec((WINDOW, D_KV), lambda b, n: (jnp.maximum(b * nb + n - 1, 0), vcol)),
         pl.BlockSpec((WINDOW, D_KV), lambda b, n: (b * nb + n, vcol)),
         pl.BlockSpec((WINDOW, D_ATTN), blk)),
        (BATCH, nb), NP, WINDOW, True, 1)
    u_s = u[NP:N_REAL].reshape(DEC_BATCH, DEC_SEQ, D_IN_AB)
    u_s16 = jnp.pad(u_s, ((0, 0), (0, SP - DEC_SEQ), (0, 0))).reshape(NSP, D_IN_AB)
    halo_s = jnp.pad(state_pool[0], ((0, 0), (1, 0), (0, 0))).reshape(DEC_BATCH * 16, D_POOL)
    pool_s, qn_s, kn_s = _mix0_pre(
        u_s16, halo_s, lambda i: (i, 0), NSP, SP, 1, True,
        pool_w[0], pool_scale[0], q_norm[0], k_norm[0])
    kn_new = kn_s.reshape(DEC_BATCH, SP, D_KV)[:, :DEC_SEQ]
    v_new = u_s[:, :, D_POOL + D_ATTN + D_KV:]
    zpad = jnp.zeros((DEC_BATCH, WINDOW - DEC_SEQ, D_KV), f32)
    k_ext = jnp.concatenate([cache_k[0].reshape(DEC_BATCH, WINDOW, D_KV), kn_new, zpad], axis=1)
    v_ext = jnp.concatenate([cache_v[0].reshape(DEC_BATCH, WINDOW, D_KV), v_new, zpad], axis=1)
    sg = 4
    ext_p = pl.BlockSpec((sg, WINDOW, D_KV), lambda b: (b, 0, 0))
    ext_c = pl.BlockSpec((sg, WINDOW, D_KV), lambda b: (b, 1, 0))
    attn_s = _attn_call(
        attn_sinks[0], qn_s, k_ext, v_ext,
        (pl.BlockSpec((sg * SP, D_ATTN), lambda b: (b, 0)), ext_p, ext_c, ext_p, ext_c,
         pl.BlockSpec((sg * SP, D_ATTN), lambda b: (b, 0))),
        (DEC_BATCH // sg,), NSP, SP, False, sg)
    x_all = _oproj([pool_p, attn_p], [_sample_rows(pool_s, D_POOL), _sample_rows(attn_s, D_ATTN)],
                   w_out_ab[0], x_in, modp, mods, 0, 2)
    x_all, h = _moe(x_all, norm_g[0, 1], modp, mods, 0, router_wt, router_b, moe_w1, moe_w3, moe_w2,
                    norm_g[1, 0])

    lb_soft = jax.nn.softmax(lb_param.astype(f32), axis=0)
    lower_bounds = jnp.cumsum(lb_soft, axis=0) - lb_soft[0:1]
    u1 = _proj(h, w_in_c[0], 2048)
    og_p, hg_p = _hgrn(u1, lower_bounds[1], o_norm[0], None,
                       nb=BATCH, nc=SEQ // 64, L=64, C=16, hpb=C_HEADS, n_valid=64, rows=NP)
    u1_s = jnp.pad(u1[NP:N_REAL].reshape(DEC_BATCH, DEC_SEQ, 4 * D_C),
                   ((0, 0), (0, SP - DEC_SEQ), (0, 0))).reshape(NSP, 4 * D_C)
    og_s, hg_s = _hgrn(u1_s, lower_bounds[1], o_norm[0], state_hgrn[0],
                       nb=DEC_BATCH, nc=1, L=SP, C=SP, hpb=C_HEADS, n_valid=DEC_SEQ, rows=NSP)
    x_all = _oproj([og_p], [_sample_rows(og_s, D_C)], w_out_c[0], (x_all,), modp, mods, 1, 2)
    y_p, y_s = _moe(x_all, norm_g[1, 1], modp, mods, 1, router_wt, router_b, moe_w1, moe_w3, moe_w2, None)

    y_prompt = y_p.reshape(BATCH, SEQ, D_MODEL)
    y_sample = y_s.reshape(DEC_BATCH, DEC_SEQ, D_MODEL)
    def seq_tail(a, n, c0, c1):
        return jnp.stack([a[(b + 1) * SEQ - n:(b + 1) * SEQ, c0:c1] for b in range(BATCH)])

    pool_prompt = seq_tail(u, POOL_STATE, 0, D_POOL)[None]
    pool_sample = jnp.concatenate([state_pool[0], u_s[:, :, :D_POOL]], axis=1)[:, -POOL_STATE:][None]
    kv_shape = (N_KV_HEADS, HEAD_DIM)
    k_prompt = seq_tail(kn_p, WINDOW, 0, D_KV).reshape(BATCH, WINDOW, *kv_shape)[None]
    v_prompt = seq_tail(u, WINDOW, D_POOL + D_ATTN + D_KV, D_IN_AB).reshape(BATCH, WINDOW, *kv_shape)[None]
    k_sample = jnp.concatenate([cache_k[0], kn_new.reshape(DEC_BATCH, DEC_SEQ, *kv_shape)], axis=1)[:, -WINDOW:][None]
    v_sample = jnp.concatenate([cache_v[0], v_new.reshape(DEC_BATCH, DEC_SEQ, *kv_shape)], axis=1)[:, -WINDOW:][None]
    return (y_prompt, y_sample, pool_prompt, pool_sample, k_prompt, k_sample, v_prompt, v_sample,
            hg_p[None], hg_s[None])
```

```python
import functools
import math

import numpy as np
import jax
import jax.numpy as jnp
from jax import lax
from jax.experimental import pallas as pl
from jax.experimental.pallas import tpu as pltpu

f32 = jnp.float32
bf16 = jnp.bfloat16
i32 = jnp.int32

D_MODEL = 2048
BATCH = 4
SEQ = 2048
DEPTH = 2
DEC_BATCH = 32
DEC_SEQ = 4
EPS = 1e-6
POOL_WINDOWS = (2, 4, 8, 16)
D_POOL = 512
POOL_GROUP = 128
POOL_STATE = 15
HEAD_DIM = 64
N_Q_HEADS = 24
N_KV_HEADS = 4
Q_PER_KV = 6
WINDOW = 128
D_ATTN = 1536
D_KV = 256
D_IN_AB = 2560
C_DK = 128
C_HEADS = 16
D_C = 2048
N_EXPERTS = 16
EXP_PER_GROUP = 4
D_FF = 512
NEG = -1e30
LOG2E = math.log2(math.e)

LANES = 128
VMEM_LIMIT = 56 * 1024 * 1024

NP = BATCH * SEQ
NS = DEC_BATCH * DEC_SEQ
N_REAL = NP + NS
TM = 512
M_ALL = ((N_REAL + TM - 1) // TM) * TM
N_TILES = M_ALL // TM
NPT = NP // TM
TPB = SEQ // TM
SP = 16
NSP = DEC_BATCH * SP

TME = 256
R_MAX = ((2 * N_REAL + N_EXPERTS * (TME - 1)) // TME + 1) * TME
NT_E = R_MAX // TME
TMC = 128
TMD = 640

FACTORED_BLOCK = 32
DECAY_CLAMP = 60.0


def _cparams(sem, vmem=VMEM_LIMIT):
    return pltpu.CompilerParams(dimension_semantics=sem, vmem_limit_bytes=vmem)


def _silu(x):
    return x * jax.nn.sigmoid(x)


def _nt_dot(a, b, **kw):
    return lax.dot_general(a, b, (((1,), (1,)), ((), ())), preferred_element_type=f32, **kw)


def _tn_dot(a, b):
    return lax.dot_general(a, b, (((0,), (0,)), ((), ())), preferred_element_type=f32)


def _ada_body(c_ref, w_ref, b_ref, o_ref):
    s = _silu(c_ref[...]).astype(bf16)
    o_ref[...] = jnp.dot(s, w_ref[...].astype(bf16), preferred_element_type=f32) + b_ref[...]


def _ada_mod(c_all, ada_w, ada_b):
    depth, d, e = ada_w.shape
    cr = c_all.shape[0]
    tn = 1024
    return pl.pallas_call(
        _ada_body,
        grid=(depth, e // tn),
        in_specs=[
            pl.BlockSpec((cr, d), lambda l, j: (0, 0)),
            pl.BlockSpec((None, d, tn), lambda l, j: (l, 0, j)),
            pl.BlockSpec((None, 1, tn), lambda l, j: (l, 0, j)),
        ],
        out_specs=pl.BlockSpec((None, cr, tn), lambda l, j: (l, 0, j)),
        out_shape=jax.ShapeDtypeStruct((depth, cr, e), f32),
        compiler_params=_cparams(("arbitrary", "arbitrary")),
        name="ada_mod",
    )(c_all, ada_w, ada_b.reshape(depth, 1, e))


def _mod_specs(layer, k):
    p = pl.BlockSpec((None, None, None, 1, D_MODEL),
                     lambda i: (layer, k, jnp.minimum(i // TPB, BATCH - 1), 0, 0))
    s = pl.BlockSpec((None, None, NS, D_MODEL), lambda i: (layer, k, 0, 0))
    return p, s


def _norm_rows(x, g):
    return x * lax.rsqrt(jnp.mean(x * x, axis=-1, keepdims=True) + EPS) * g


def _x_specs(x, width, col):
    if len(x) == 1:
        return [pl.BlockSpec((TM, width), lambda *g: (g[-1], col(*g)))]
    return [pl.BlockSpec((TM, width), lambda *g: (jnp.minimum(g[-1], NPT - 1), col(*g))),
            pl.BlockSpec((NS, width), lambda *g: (0, col(*g)))]


def _normmod_body(*refs):
    *x_refs, g_ref, scp_ref, shp_ref, scs_ref, shs_ref, h_ref = refs
    i = pl.program_id(0)

    @pl.when(i < NPT)
    def _():
        y = _norm_rows(x_refs[0][...], g_ref[...])
        h_ref[...] = (y * (1.0 + scp_ref[...]) + shp_ref[...]).astype(h_ref.dtype)

    @pl.when(i >= NPT)
    def _():
        y = _norm_rows(x_refs[-1][:NS], g_ref[...])
        h_ref[:NS] = (y * (1.0 + scs_ref[...]) + shs_ref[...]).astype(h_ref.dtype)
        h_ref[NS:] = jnp.zeros((TM - NS, D_MODEL), h_ref.dtype)


def _normmod(x, g, modp, mods, layer, k_shift, k_scale):
    scp, scs = _mod_specs(layer, k_scale)
    shp, shs = _mod_specs(layer, k_shift)
    return pl.pallas_call(
        _normmod_body,
        grid=(N_TILES,),
        in_specs=_x_specs(x, D_MODEL, lambda i: 0) + [pl.BlockSpec((1, D_MODEL), lambda i: (0, 0)),
                                                       scp, shp, scs, shs],
        out_specs=pl.BlockSpec((TM, D_MODEL), lambda i: (i, 0)),
        out_shape=jax.ShapeDtypeStruct((M_ALL, D_MODEL), bf16),
        compiler_params=_cparams(("arbitrary",)),
        name="normmod",
    )(*x, g.reshape(1, D_MODEL), modp, modp, mods, mods)


def _proj_body(h_ref, w_ref, o_ref, wb_ref):
    @pl.when(pl.program_id(1) == 0)
    def _():
        wb_ref[...] = w_ref[...].astype(bf16)

    o_ref[...] = jnp.dot(h_ref[...], wb_ref[...], preferred_element_type=f32)


def _proj(h_all, w, tn):
    k, e = w.shape
    return pl.pallas_call(
        _proj_body,
        grid=(e // tn, N_TILES),
        in_specs=[pl.BlockSpec((TM, k), lambda j, i: (i, 0)),
                  pl.BlockSpec((k, tn), lambda j, i: (0, j))],
        out_specs=pl.BlockSpec((TM, tn), lambda j, i: (i, j)),
        out_shape=jax.ShapeDtypeStruct((M_ALL, e), f32),
        scratch_shapes=[pltpu.VMEM((k, tn), bf16)],
        compiler_params=_cparams(("arbitrary", "arbitrary")),
        name="proj",
    )(h_all, w)


def _oproj_body(*refs, n_lhs):
    ap_refs = refs[:n_lhs]
    as_refs = refs[n_lhs:2 * n_lhs]
    w_ref, *x_refs, gp_ref, gs_ref, o_ref, wb_ref = refs[2 * n_lhs:]
    i = pl.program_id(1)

    @pl.when(i == 0)
    def _():
        wb_ref[...] = w_ref[...].astype(bf16)

    def matmul(a_refs, rows):
        off, acc = 0, None
        for a in a_refs:
            kk = a.shape[1]
            part = jnp.dot(a[:rows], wb_ref[off:off + kk, :], preferred_element_type=f32)
            acc = part if acc is None else acc + part
            off += kk
        return acc

    @pl.when(i < NPT)
    def _():
        o_ref[...] = x_refs[0][...] + gp_ref[...] * matmul(ap_refs, TM)

    @pl.when(i >= NPT)
    def _():
        o_ref[:NS] = x_refs[-1][:NS] + gs_ref[...] * matmul(as_refs, NS)
        o_ref[NS:] = jnp.zeros((TM - NS, o_ref.shape[1]), f32)


def _oproj(a_prompt, a_sample, w, x, modp, mods, layer, k_gate):
    n_lhs = len(a_prompt)
    kdim, e = w.shape
    tn = 1024
    in_specs = []
    for a in a_prompt:
        in_specs.append(pl.BlockSpec((TM, a.shape[1]), lambda j, i: (jnp.minimum(i, NPT - 1), 0)))
    for a in a_sample:
        in_specs.append(pl.BlockSpec((NS, a.shape[1]), lambda j, i: (0, 0)))
    in_specs += [pl.BlockSpec((kdim, tn), lambda j, i: (0, j))] + _x_specs(x, tn, lambda j, i: j) + [
        pl.BlockSpec((None, None, None, 1, tn),
                     lambda j, i: (layer, k_gate, jnp.minimum(i // TPB, BATCH - 1), 0, j)),
        pl.BlockSpec((None, None, NS, tn), lambda j, i: (layer, k_gate, 0, j)),
    ]
    return pl.pallas_call(
        functools.partial(_oproj_body, n_lhs=n_lhs),
        grid=(e // tn, N_TILES),
        in_specs=in_specs,
        out_specs=pl.BlockSpec((TM, tn), lambda j, i: (i, j)),
        out_shape=jax.ShapeDtypeStruct((M_ALL, e), f32),
        scratch_shapes=[pltpu.VMEM((kdim, tn), bf16)],
        compiler_params=_cparams(("arbitrary", "arbitrary")),
        name="oproj",
    )(*a_prompt, *a_sample, w, *x, modp, mods)


def _head_norm(x, w_row, scale):
    lane = lax.broadcasted_iota(i32, x.shape, 1)
    lo = lane < HEAD_DIM
    ss = x * x
    m_lo = jnp.sum(jnp.where(lo, ss, 0.0), axis=-1, keepdims=True)
    m_hi = jnp.sum(jnp.where(lo, 0.0, ss), axis=-1, keepdims=True)
    r = jnp.where(lo, lax.rsqrt(m_lo * (1.0 / HEAD_DIM) + EPS), lax.rsqrt(m_hi * (1.0 / HEAD_DIM) + EPS))
    return x * r * (w_row * scale)


def _mix0_pre_body(cur_ref, halo_ref, q0_ref, q1_ref, q2_ref, k_ref, pw_ref, ps_ref, qw_ref, kw_ref,
                   pool_ref, qn_ref, kn_ref, ext_ref, *, tp, tiles_per_seq, full_count):
    i = pl.program_id(0)
    halo = halo_ref[...]
    if not full_count:
        halo = jnp.where(i % tiles_per_seq == 0, 0.0, halo)
    ext_ref[0:16, :] = halo
    ext_ref[16:16 + tp, :] = cur_ref[...]
    t_pos = (i % tiles_per_seq) * tp + lax.broadcasted_iota(i32, (tp, 1), 0)
    for g, w in enumerate(POOL_WINDOWS):
        ch = slice(g * POOL_GROUP, (g + 1) * POOL_GROUP)
        acc = ext_ref[16:16 + tp, ch]
        for dlt in range(1, w):
            acc = acc + ext_ref[16 - dlt:16 - dlt + tp, ch]
        if full_count:
            mean = acc * (1.0 / w)
        else:
            cnt = jnp.minimum(t_pos + 1, w).astype(f32)
            mean = acc / cnt
        d = mean - ext_ref[16:16 + tp, ch]
        y = jnp.dot(d.astype(bf16), pw_ref[g].astype(bf16), preferred_element_type=f32)
        pool_ref[:, ch] = (y * ps_ref[:, ch]).astype(pool_ref.dtype)
    for c3, q_ref in enumerate((q0_ref, q1_ref, q2_ref)):
        for c in range(4):
            sl = slice(c * LANES, (c + 1) * LANES)
            dst = slice((c3 * 4 + c) * LANES, (c3 * 4 + c + 1) * LANES)
            qn_ref[:, dst] = _head_norm(q_ref[:, sl], qw_ref[...], LOG2E * HEAD_DIM ** -0.5).astype(qn_ref.dtype)
    for c in range(2):
        sl = slice(c * LANES, (c + 1) * LANES)
        kn_ref[:, sl] = _head_norm(k_ref[:, sl], kw_ref[...], 1.0)


def _mix0_pre(u, halo_src, halo_map, rows, tp, tiles_per_seq, full_count, pool_w, pool_scale, q_norm, k_norm):
    qw = jnp.tile(q_norm.reshape(1, HEAD_DIM), (1, 2))
    kw = jnp.tile(k_norm.reshape(1, HEAD_DIM), (1, 2))
    body = functools.partial(_mix0_pre_body, tp=tp, tiles_per_seq=tiles_per_seq, full_count=full_count)
    return pl.pallas_call(
        body,
        grid=(rows // tp,),
        in_specs=[
            pl.BlockSpec((tp, D_POOL), lambda i: (i, 0)),
            pl.BlockSpec((16, D_POOL), halo_map),
            pl.BlockSpec((tp, 512), lambda i: (i, 1)),
            pl.BlockSpec((tp, 512), lambda i: (i, 2)),
            pl.BlockSpec((tp, 512), lambda i: (i, 3)),
            pl.BlockSpec((tp, D_KV), lambda i: (i, (D_POOL + D_ATTN) // D_KV)),
            pl.BlockSpec((4, POOL_GROUP, POOL_GROUP), lambda i: (0, 0, 0)),
            pl.BlockSpec((1, D_POOL), lambda i: (0, 0)),
            pl.BlockSpec((1, LANES), lambda i: (0, 0)),
            pl.BlockSpec((1, LANES), lambda i: (0, 0)),
        ],
        out_specs=[
            pl.BlockSpec((tp, D_POOL), lambda i: (i, 0)),
            pl.BlockSpec((tp, D_ATTN), lambda i: (i, 0)),
            pl.BlockSpec((tp, D_KV), lambda i: (i, 0)),
        ],
        out_shape=[
            jax.ShapeDtypeStruct((rows, D_POOL), bf16),
            jax.ShapeDtypeStruct((rows, D_ATTN), bf16),
            jax.ShapeDtypeStruct((rows, D_KV), f32),
        ],
        scratch_shapes=[pltpu.VMEM((16 + tp, D_POOL), f32)],
        compiler_params=_cparams(("arbitrary",)),
        name="mix0_pre",
    )(u, halo_src, u, u, u, u, pool_w, pool_scale.reshape(1, D_POOL), qw, kw)


def _alibi_slopes(n):
    def pow2_slopes(m):
        start = 2.0 ** (-8.0 / m)
        return [start ** (i + 1) for i in range(m)]
    if n & (n - 1) == 0:
        s = pow2_slopes(n)
    else:
        c = 2 ** int(math.floor(math.log2(n)))
        s = pow2_slopes(c) + pow2_slopes(2 * c)[0::2][: n - c]
    return [float(v) for v in np.asarray(s, dtype=np.float32)]


_SLOPES = _alibi_slopes(N_Q_HEADS)


def _attn_body(sink_ref, q_ref, kp_ref, kc_ref, vp_ref, vc_ref, o_ref, bias_ref, *, tq, prompt, group):
    first = pl.program_id(0) == 0
    if prompt:
        first = first & (pl.program_id(1) == 0)

    @pl.when(first)
    def _():
        qi = lax.broadcasted_iota(i32, (tq, WINDOW), 0)
        kj = lax.broadcasted_iota(i32, (tq, WINDOW), 1)
        dist_p = (qi + WINDOW - kj).astype(f32)
        dist_c = (qi - kj).astype(f32)
        neg_p = jnp.where(qi + WINDOW - kj < WINDOW, 0.0, NEG)
        neg_c = jnp.where(qi - kj >= 0, 0.0, NEG)
        for head in range(N_Q_HEADS):
            bias_ref[head, 0] = neg_p - (_SLOPES[head] * LOG2E) * dist_p
            bias_ref[head, 1] = neg_c - (_SLOPES[head] * LOG2E) * dist_c

    lane = lax.broadcasted_iota(i32, (WINDOW, LANES), 1)
    lo = lane < HEAD_DIM
    qlane = lax.broadcasted_iota(i32, (tq, LANES), 1) < HEAD_DIM

    for g in range(group):
        q_rows_g = slice(g * tq, (g + 1) * tq)

        def kv(ref, cols):
            return ref[:, cols] if group == 1 else ref[g, :, cols]

        for j in range(N_KV_HEADS):
            pair = slice((j // 2) * LANES, (j // 2 + 1) * LANES)
            own_lo = (j % 2 == 0)

            def dup(kpair):
                sw = pltpu.roll(kpair, HEAD_DIM, 1)
                return jnp.where(lo, kpair, sw) if own_lo else jnp.where(lo, sw, kpair)

            def halves(vpair):
                own = jnp.where(lo, vpair, 0.0) if own_lo else jnp.where(lo, 0.0, vpair)
                sw = pltpu.roll(own, HEAD_DIM, 1)
                return (own, sw) if own_lo else (sw, own)

            kk_p = dup(kv(kp_ref, pair)).astype(bf16)
            kk_c = dup(kv(kc_ref, pair)).astype(bf16)
            v_p = [h.astype(bf16) for h in halves(kv(vp_ref, pair))]
            v_c = [h.astype(bf16) for h in halves(kv(vc_ref, pair))]

            heads = [(hf, c) for hf in (0, 1) for c in range(3)]
            q_rows = []
            for hf, c in heads:
                ch = slice((3 * j + c) * LANES, (3 * j + c + 1) * LANES)
                qc = q_ref[q_rows_g, ch].astype(f32)
                q_rows.append(jnp.where(qlane, qc, 0.0) if hf == 0 else jnp.where(qlane, 0.0, qc))
            q_st = jnp.concatenate(q_rows, axis=0).astype(bf16)
            s_p = _nt_dot(q_st, kk_p)
            s_c = _nt_dot(q_st, kk_c)
            for c in range(3):
                out = None
                for hf in (0, 1):
                    head = 6 * j + 2 * c + hf
                    rows = slice((hf * 3 + c) * tq, (hf * 3 + c + 1) * tq)
                    bias_p = bias_ref[head, 0]
                    if prompt:
                        bias_p = jnp.where(pl.program_id(1) > 0, bias_p, NEG)
                    sp = s_p[rows] + bias_p
                    sc = s_c[rows] + bias_ref[head, 1]
                    sink = sink_ref[head] * LOG2E
                    m = jnp.maximum(jnp.max(jnp.maximum(sp, sc), axis=-1, keepdims=True), sink)
                    ep = jnp.exp2(sp - m)
                    ec = jnp.exp2(sc - m)
                    den = jnp.sum(ep + ec, axis=-1, keepdims=True) + jnp.exp2(sink - m)
                    o = (jnp.dot(ep.astype(bf16), v_p[hf], preferred_element_type=f32)
                         + jnp.dot(ec.astype(bf16), v_c[hf], preferred_element_type=f32)) * (1.0 / den)
                    out = o if out is None else out + o
                ch = slice((3 * j + c) * LANES, (3 * j + c + 1) * LANES)
                o_ref[q_rows_g, ch] = out.astype(o_ref.dtype)


def _attn_call(sinks, q, k_arr, v_arr, specs, grid, rows, tq, prompt, group):
    body = functools.partial(_attn_body, tq=tq, prompt=prompt, group=group)
    q_spec, kp_spec, kc_spec, vp_spec, vc_spec, o_spec = specs
    return pl.pallas_call(
        body,
        grid=grid,
        in_specs=[pl.BlockSpec(memory_space=pltpu.SMEM), q_spec, kp_spec, kc_spec, vp_spec, vc_spec],
        out_specs=o_spec,
        out_shape=jax.ShapeDtypeStruct((rows, D_ATTN), bf16),
        scratch_shapes=[pltpu.VMEM((N_Q_HEADS, 2, tq, WINDOW), f32)],
        compiler_params=_cparams(("arbitrary",) * len(grid)),
        name="attn",
    )(sinks, q, k_arr, k_arr, v_arr, v_arr)


def _hgrn_inputs(q_ref, f_ref, i_ref, lb_ref, sl, *, L, n_valid):
    lb = lb_ref[:, sl]
    fp = f_ref[:, sl]
    q = _silu(q_ref[:, sl])
    t = jnp.exp(-jnp.abs(fp))
    inv = 1.0 / (1.0 + t)
    pos = fp >= 0.0
    logf = jnp.log(lb + (1.0 - lb) * jnp.where(pos, inv, t * inv))
    kk = (1.0 - lb) * jnp.where(pos, t * inv, inv)
    if n_valid < L:
        valid = lax.broadcasted_iota(i32, (L, 1), 0) < n_valid
        logf = jnp.where(valid, logf, 0.0)
        kk = jnp.where(valid, kk, 0.0)
    return q, kk, i_ref[:, sl], logf


def _cum_decay(logf, L):
    row = lax.broadcasted_iota(i32, (L, L), 0)
    col = lax.broadcasted_iota(i32, (L, L), 1)
    tri = (row >= col).astype(bf16)
    h1 = logf.astype(bf16)
    r1 = logf - h1.astype(f32)
    h2 = r1.astype(bf16)
    h3 = (r1 - h2.astype(f32)).astype(bf16)
    return (jnp.dot(tri, h1, preferred_element_type=f32) + jnp.dot(tri, h2, preferred_element_type=f32)
            + jnp.dot(tri, h3, preferred_element_type=f32))


def _sub_ref(b, lo):
    return b[lo - 1:lo] if lo > 0 else jnp.zeros((1, b.shape[1]), f32)


def _factored_operands(q, kk, b, *, L, CF):
    rowid = lax.broadcasted_iota(i32, (L, 1), 0)
    qts, kts = [], []
    for lo in range(0, L, CF):
        r = _sub_ref(b, lo)
        qts.append((q[lo:lo + CF] * jnp.exp(b[lo:lo + CF] - r)).astype(bf16))
        kts.append(jnp.where(rowid < lo + CF, kk * jnp.exp(jnp.minimum(r - b, DECAY_CLAMP)), 0.0).astype(bf16))
    return qts, kts


def _decay_span(b, *, L, CF):
    span = None
    for lo in range(0, L, CF):
        d = _sub_ref(b, lo) - b[lo + CF - 1:lo + CF]
        span = d if span is None else jnp.maximum(span, d)
    return span


def _intra_pairwise(q, kk, v, b, *, L, C):
    vb = v.astype(bf16)
    rowid = lax.broadcasted_iota(i32, (L, 1), 0)
    tloc = lax.broadcasted_iota(i32, (C, 1), 0)
    outs = []
    for lo in range(0, L, C):
        bi, qi, ki, vi = b[lo:lo + C], q[lo:lo + C], kk[lo:lo + C], v[lo:lo + C]
        if lo > 0:
            r = _sub_ref(b, lo)
            qt = (qi * jnp.exp(bi - r)).astype(bf16)
            kt = jnp.where(rowid < lo, kk * jnp.exp(jnp.minimum(r - b, 0.0)), 0.0).astype(bf16)
            oi = jnp.dot(_nt_dot(qt, kt).astype(bf16), vb, preferred_element_type=f32)
        else:
            oi = jnp.zeros((C, LANES), f32)
        for s in range(C):
            e = jnp.exp(jnp.minimum(bi - bi[s:s + 1], 0.0))
            a = jnp.sum(qi * ki[s:s + 1] * e, axis=-1, keepdims=True)
            oi = oi + jnp.where(tloc >= s, a, 0.0) * vi[s:s + 1]
        outs.append(oi)
    return outs[0] if len(outs) == 1 else jnp.concatenate(outs, axis=0)


def _hgrn_body(*refs, L, C, hpb, n_valid, has_state):
    if has_state:
        q_ref, f_ref, i_ref, g_ref, lb_ref, on_ref, s0_ref, og_ref, so_ref, st_ref, inter_ref = refs
    else:
        q_ref, f_ref, i_ref, g_ref, lb_ref, on_ref, og_ref, so_ref, st_ref, inter_ref = refs
    c = pl.program_id(2)
    nc = pl.num_programs(2)
    CF = min(L, FACTORED_BLOCK)

    @pl.when(c == 0)
    def _():
        for hh in range(hpb):
            st_ref[hh] = s0_ref[hh].T if has_state else jnp.zeros((C_DK, C_DK), f32)

    def finish(o, sl):
        o = o * lax.rsqrt(jnp.mean(o * o, axis=-1, keepdims=True) + EPS) * on_ref[...]
        og_ref[:, sl] = (o * _silu(g_ref[:, sl])).astype(og_ref.dtype)

    q, kk, v, logf = _hgrn_inputs(q_ref, f_ref, i_ref, lb_ref, slice(0, hpb * LANES), L=L, n_valid=n_valid)
    b = _cum_decay(logf, L)
    vb = v.astype(bf16)
    q_hat = (q * jnp.exp(b)).astype(bf16)
    bl = b[L - 1:L]
    k_end = (kk * jnp.exp(bl - b)).astype(bf16)
    st_decay = jnp.exp(bl)
    qts, kts = _factored_operands(q, kk, b, L=L, CF=CF)
    span = _decay_span(b, L=L, CF=CF)
    causal = lax.broadcasted_iota(i32, (L, L), 0) >= lax.broadcasted_iota(i32, (L, L), 1)
    for hh in range(hpb):
        sl = slice(hh * LANES, (hh + 1) * LANES)
        st = st_ref[hh]
        o_inter = _nt_dot(q_hat[:, sl], st.astype(bf16))
        st_ref[hh] = st * st_decay[:, sl] + _tn_dot(vb[:, sl], k_end[:, sl])
        inter_ref[hh] = o_inter
        blocks = [_nt_dot(qt[:, sl], kt[:, sl]) for qt, kt in zip(qts, kts)]
        a = blocks[0] if len(blocks) == 1 else jnp.concatenate(blocks, axis=0)
        a = jnp.where(causal, a, 0.0).astype(bf16)
        finish(jnp.dot(a, vb[:, sl], preferred_element_type=f32) + o_inter, sl)

    @pl.when(jnp.max(span) >= DECAY_CLAMP)
    def _():
        for hh in range(hpb):
            sl = slice(hh * LANES, (hh + 1) * LANES)
            q, kk, v, logf = _hgrn_inputs(q_ref, f_ref, i_ref, lb_ref, sl, L=L, n_valid=n_valid)
            b = _cum_decay(logf, L)
            finish(_intra_pairwise(q, kk, v, b, L=L, C=C) + inter_ref[hh], sl)

    @pl.when(c == nc - 1)
    def _():
        for hh in range(hpb):
            so_ref[hh] = st_ref[hh].T


def _hgrn(u, lb, o_norm, s0, *, nb, nc, L, C, hpb, n_valid, rows):
    nh = C_HEADS // hpb
    w = hpb * LANES
    has_state = s0 is not None
    body = functools.partial(_hgrn_body, L=L, C=C, hpb=hpb, n_valid=n_valid, has_state=has_state)

    def uspec(part):
        return pl.BlockSpec((L, w), lambda b, h, c: (b * nc + c, part * nh + h))

    in_specs = [uspec(0), uspec(1), uspec(2), uspec(3),
                pl.BlockSpec((1, w), lambda b, h, c: (0, h)),
                pl.BlockSpec((1, LANES), lambda b, h, c: (0, 0))]
    args = [u, u, u, u, lb.reshape(1, D_C), o_norm.reshape(1, LANES)]
    if has_state:
        in_specs.append(pl.BlockSpec((None, hpb, C_DK, C_DK), lambda b, h, c: (b, h, 0, 0)))
        args.append(s0)
    return pl.pallas_call(
        body,
        grid=(nb, nh, nc),
        in_specs=in_specs,
        out_specs=[pl.BlockSpec((L, w), lambda b, h, c: (b * nc + c, h)),
                   pl.BlockSpec((None, hpb, C_DK, C_DK), lambda b, h, c: (b, h, 0, 0))],
        out_shape=[jax.ShapeDtypeStruct((rows, D_C), bf16),
                   jax.ShapeDtypeStruct((nb, C_HEADS, C_DK, C_DK), f32)],
        scratch_shapes=[pltpu.VMEM((hpb, C_DK, C_DK), f32), pltpu.VMEM((hpb, L, LANES), f32)],
        compiler_params=_cparams(("arbitrary", "arbitrary", "arbitrary")),
        name="hgrn",
    )(*args)


def _moe_pre_body(x_ref, g_ref, scp_ref, shp_ref, scs_ref, shs_ref, wrt_ref, rb_ref,
                  h_ref, idx_ref, wts_ref, rank_ref, cnt_ref, carry):
    i = pl.program_id(0)

    @pl.when(i == 0)
    def _():
        carry[...] = jnp.zeros_like(carry)

    y = _norm_rows(x_ref[...], g_ref[...])

    @pl.when(i < NPT)
    def _():
        h_ref[...] = y * (1.0 + scp_ref[...]) + shp_ref[...]

    @pl.when(i >= NPT)
    def _():
        h_ref[:NS] = y[:NS] * (1.0 + scs_ref[...]) + shs_ref[...]
        h_ref[NS:] = jnp.zeros((TM - NS, D_MODEL), f32)

    h = h_ref[...]

    w_r = wrt_ref[...]
    w_hi = w_r.astype(bf16)
    w_lo = (w_r - w_hi.astype(f32)).astype(bf16)
    h_hi = h.astype(bf16)
    h_lo = (h - h_hi.astype(f32)).astype(bf16)
    logits = _nt_dot(w_hi, h_hi) + _nt_dot(w_hi, h_lo) + _nt_dot(w_lo, h_hi)
    scores = jax.nn.sigmoid(logits)
    sel = scores + rb_ref[...]
    best = jnp.zeros((1, TM), i32)
    best_v = None
    for g in range(N_EXPERTS // EXP_PER_GROUP):
        r = [sel[EXP_PER_GROUP * g + t:EXP_PER_GROUP * g + t + 1] for t in range(EXP_PER_GROUP)]
        top2 = None
        for a in range(EXP_PER_GROUP):
            for bb in range(a + 1, EXP_PER_GROUP):
                s2 = r[a] + r[bb]
                top2 = s2 if top2 is None else jnp.maximum(top2, s2)
        if g == 0:
            best_v = top2
        else:
            upd = top2 > best_v
            best = jnp.where(upd, g, best)
            best_v = jnp.where(upd, top2, best_v)
    eidx = lax.broadcasted_iota(i32, (N_EXPERTS, TM), 0)
    masked = jnp.where(jnp.right_shift(eidx, 2) == best, sel, NEG)
    m1 = jnp.max(masked, axis=0, keepdims=True)
    i1 = jnp.min(jnp.where(masked == m1, eidx, N_EXPERTS), axis=0, keepdims=True)
    masked2 = jnp.where(eidx == i1, -jnp.inf, masked)
    m2 = jnp.max(masked2, axis=0, keepdims=True)
    i2 = jnp.min(jnp.where(masked2 == m2, eidx, N_EXPERTS), axis=0, keepdims=True)
    w1 = jnp.sum(jnp.where(eidx == i1, scores, 0.0), axis=0, keepdims=True)
    w2 = jnp.sum(jnp.where(eidx == i2, scores, 0.0), axis=0, keepdims=True)
    den = w1 + w2
    idx_ref[0:1, :] = i1
    idx_ref[1:2, :] = i2
    wts_ref[0:1, :] = w1 / den
    wts_ref[1:2, :] = w2 / den

    colg = i * TM + lax.broadcasted_iota(i32, (1, TM), 1)
    oh = jnp.where(((eidx == i1) | (eidx == i2)) & (colg < N_REAL), 1.0, 0.0)
    rr = lax.broadcasted_iota(i32, (TM, TM), 0)
    cc = lax.broadcasted_iota(i32, (TM, TM), 1)
    before = (rr < cc).astype(bf16)
    rank = jnp.dot(oh.astype(bf16), before, preferred_element_type=f32) + carry[:, 0:1]
    rank_ref[0:1, :] = jnp.sum(jnp.where(eidx == i1, rank, 0.0), axis=0, keepdims=True).astype(i32)
    rank_ref[1:2, :] = jnp.sum(jnp.where(eidx == i2, rank, 0.0), axis=0, keepdims=True).astype(i32)
    carry[...] = carry[...] + jnp.sum(oh, axis=1, keepdims=True)
    cnt_ref[...] = carry[...]


def _moe_pre(x_all, g, modp, mods, layer, router_wt, router_b):
    scp, scs = _mod_specs(layer, 4)
    shp, shs = _mod_specs(layer, 3)
    return pl.pallas_call(
        _moe_pre_body,
        grid=(N_TILES,),
        in_specs=[pl.BlockSpec((TM, D_MODEL), lambda i: (i, 0)),
                  pl.BlockSpec((1, D_MODEL), lambda i: (0, 0)),
                  scp, shp, scs, shs,
                  pl.BlockSpec((N_EXPERTS, D_MODEL), lambda i: (0, 0)),
                  pl.BlockSpec((N_EXPERTS, 1), lambda i: (0, 0))],
        out_specs=[pl.BlockSpec((TM, D_MODEL), lambda i: (i, 0)),
                   pl.BlockSpec((2, TM), lambda i: (0, i)),
                   pl.BlockSpec((2, TM), lambda i: (0, i)),
                   pl.BlockSpec((2, TM), lambda i: (0, i)),
                   pl.BlockSpec((N_EXPERTS, LANES), lambda i: (0, 0))],
        out_shape=[jax.ShapeDtypeStruct((M_ALL, D_MODEL), f32),
                   jax.ShapeDtypeStruct((2, M_ALL), i32),
                   jax.ShapeDtypeStruct((2, M_ALL), f32),
                   jax.ShapeDtypeStruct((2, M_ALL), i32),
                   jax.ShapeDtypeStruct((N_EXPERTS, LANES), f32)],
        scratch_shapes=[pltpu.VMEM((N_EXPERTS, LANES), f32)],
        compiler_params=_cparams(("arbitrary",)),
        name="moe_pre",
    )(x_all, g.reshape(1, D_MODEL), modp, modp, mods, mods, router_wt, router_b.reshape(N_EXPERTS, 1))


def _dispatch_body(pos_ref, fill_lo_ref, fill_hi_ref, nu_ref, h_ref, hs_hbm, zero_buf, sem, tile_sem):
    i = pl.program_id(0)
    base = i * TMD

    def row_copy(src, dst_row):
        return pltpu.make_async_copy(src, hs_hbm.at[pl.ds(dst_row, 1)], sem)

    def wait_row(n, carry):
        row_copy(h_ref.at[pl.ds(0, 1)], 0).wait()
        return carry

    @pl.when(i == 0)
    def _():
        zero_buf[...] = jnp.zeros_like(zero_buf)
        zero_row = zero_buf.at[pl.ds(0, 1)]

        def tile_copy(t):
            return pltpu.make_async_copy(zero_buf, hs_hbm.at[pl.ds(t * TME, TME)], tile_sem)

        def tail_start(t, carry):
            tile_copy(t).start()
            return carry

        def tail_wait(t, carry):
            tile_copy(t).wait()
            return carry

        def fill_expert(e, carry):
            def fill(r, c2):
                row_copy(zero_row, r).start()
                return c2
            lax.fori_loop(fill_lo_ref[e], fill_hi_ref[e], fill, 0)
            lax.fori_loop(fill_lo_ref[e], fill_hi_ref[e], wait_row, 0)
            return carry

        lax.fori_loop(nu_ref[0], NT_E, tail_start, 0)
        lax.fori_loop(0, N_EXPERTS, fill_expert, 0)
        lax.fori_loop(nu_ref[0], NT_E, tail_wait, 0)

    def issue(r, carry):
        src = h_ref.at[pl.ds(r, 1)]
        row_copy(src, pos_ref[base + r]).start(priority=0)
        row_copy(src, pos_ref[M_ALL + base + r]).start(priority=1)
        return carry

    lax.fori_loop(0, TMD, issue, 0, unroll=8)
    lax.fori_loop(0, 2 * TMD, wait_row, 0, unroll=8)


def _dispatch(pos_flat, fill_lo, fill_hi, n_used, h):
    return pl.pallas_call(
        _dispatch_body,
        grid_spec=pltpu.PrefetchScalarGridSpec(
            num_scalar_prefetch=4,
            grid=(N_REAL // TMD,),
            in_specs=[pl.BlockSpec((TMD, D_MODEL), lambda i, *_: (i, 0))],
            out_specs=pl.BlockSpec(memory_space=pl.ANY),
            scratch_shapes=[pltpu.VMEM((TME, D_MODEL), f32), pltpu.SemaphoreType.DMA,
                            pltpu.SemaphoreType.DMA],
        ),
        out_shape=jax.ShapeDtypeStruct((R_MAX, D_MODEL), f32),
        compiler_params=_cparams(("arbitrary",)),
        name="moe_dispatch",
    )(pos_flat, fill_lo, fill_hi, n_used, h)


def _expert_body(te_ref, nu_ref, hs_ref, w1_ref, w3_ref, w2_ref, y_ref, w1b, w3b, w2b):
    t = pl.program_id(0)

    @pl.when(t < nu_ref[0])
    def _():
        changed = (t == 0) | (te_ref[t] != te_ref[jnp.maximum(t - 1, 0)])

        @pl.when(changed)
        def _():
            w1b[...] = w1_ref[...].astype(bf16)
            w3b[...] = w3_ref[...].astype(bf16)
            w2b[...] = w2_ref[...].astype(bf16)

        hb = hs_ref[...].astype(bf16)
        a = jnp.dot(hb, w1b[...], preferred_element_type=f32)
        b = jnp.dot(hb, w3b[...], preferred_element_type=f32)
        hid = (_silu(a) * b).astype(bf16)
        y_ref[...] = jnp.dot(hid, w2b[...], preferred_element_type=f32)

    @pl.when(t >= nu_ref[0])
    def _():
        y_ref[...] = jnp.zeros_like(y_ref)


def _experts(tile_expert, n_used, hs, w1, w3, w2, layer):
    def row_map(t, te, nu):
        return (jnp.minimum(t, nu[0] - 1), 0)

    def out_map(t, te, nu):
        return (t, 0)

    def w_map(t, te, nu):
        return (layer, te[t], 0, 0)

    return pl.pallas_call(
        _expert_body,
        grid_spec=pltpu.PrefetchScalarGridSpec(
            num_scalar_prefetch=2,
            grid=(NT_E,),
            in_specs=[pl.BlockSpec((TME, D_MODEL), row_map),
                      pl.BlockSpec((None, None, D_MODEL, D_FF), w_map),
                      pl.BlockSpec((None, None, D_MODEL, D_FF), w_map),
                      pl.BlockSpec((None, None, D_FF, D_MODEL), w_map)],
            out_specs=pl.BlockSpec((TME, D_MODEL), out_map),
            scratch_shapes=[pltpu.VMEM((D_MODEL, D_FF), bf16), pltpu.VMEM((D_MODEL, D_FF), bf16),
                            pltpu.VMEM((D_FF, D_MODEL), bf16)],
        ),
        out_shape=jax.ShapeDtypeStruct((R_MAX, D_MODEL), f32),
        compiler_params=_cparams(("arbitrary",)),
        name="moe_experts",
    )(tile_expert, n_used, hs, w1, w3, w2)


def _combine_body(pos_ref, x_ref, w_ref, gp_ref, gs_ref, y_hbm, *refs, split):
    if split:
        yp_ref, ys_ref, buf0, buf1, sems = refs
    else:
        ng_ref, scp_ref, shp_ref, scs_ref, shs_ref, o_ref, h_ref, buf0, buf1, sems = refs
    i = pl.program_id(0)
    n_real_tiles = N_REAL // TMC

    def copies(tile, r):
        slot = tile % 2
        base = tile * TMC
        return (pltpu.make_async_copy(y_hbm.at[pl.ds(pos_ref[base + r], 1)],
                                      buf0.at[slot, pl.ds(r, 1)], sems.at[slot]),
                pltpu.make_async_copy(y_hbm.at[pl.ds(pos_ref[M_ALL + base + r], 1)],
                                      buf1.at[slot, pl.ds(r, 1)], sems.at[slot]))

    def start_tile(tile):
        def start(r, c):
            for queue, cp in enumerate(copies(tile, r)):
                cp.start(priority=queue)
            return c
        lax.fori_loop(0, TMC, start, 0, unroll=8)

    def wait_tile(tile):
        def wait(r, c):
            for cp in copies(tile, r):
                cp.wait()
            return c
        lax.fori_loop(0, TMC, wait, 0, unroll=8)

    @pl.when(i == 0)
    def _():
        start_tile(0)

    @pl.when(i + 1 < n_real_tiles)
    def _():
        start_tile(i + 1)

    def combined(gate):
        wait_tile(i)
        slot = i % 2
        m = w_ref[:, 0:1] * buf0[slot] + w_ref[:, 1:2] * buf1[slot]
        return x_ref[...] + gate * m

    def emit(gate, dst, sc_ref, sh_ref):
        o = combined(gate)
        dst[...] = o
        if not split:
            h_ref[...] = (_norm_rows(o, ng_ref[...]) * (1.0 + sc_ref[...]) + sh_ref[...]).astype(h_ref.dtype)

    @pl.when(i < NP // TMC)
    def _():
        emit(gp_ref[...], yp_ref if split else o_ref, None if split else scp_ref, None if split else shp_ref)

    @pl.when((i >= NP // TMC) & (i < n_real_tiles))
    def _():
        emit(gs_ref[...], ys_ref if split else o_ref, None if split else scs_ref, None if split else shs_ref)

    if not split:
        @pl.when(i >= n_real_tiles)
        def _():
            o_ref[...] = jnp.zeros_like(o_ref)
            h_ref[...] = jnp.zeros_like(h_ref)


def _combine(pos_flat, x_all, wts_t, modp, mods, layer, y, next_norm_g):
    split = next_norm_g is None
    tiles_per_seq = SEQ // TMC
    n_prompt_tiles = NP // TMC

    def mod_specs(lyr, k):
        return (pl.BlockSpec((None, None, None, 1, D_MODEL),
                             lambda i, p: (lyr, k, jnp.minimum(i // tiles_per_seq, BATCH - 1), 0, 0)),
                pl.BlockSpec((None, None, NS, D_MODEL), lambda i, p: (lyr, k, 0, 0)))

    gp, gs = mod_specs(layer, 5)
    in_specs = [pl.BlockSpec((TMC, D_MODEL), lambda i, p: (i, 0)),
                pl.BlockSpec((TMC, 2), lambda i, p: (i, 0)), gp, gs,
                pl.BlockSpec(memory_space=pl.ANY)]
    args = [x_all, wts_t, modp, mods, y]
    if split:
        grid = (N_REAL // TMC,)
        out_specs = [pl.BlockSpec((TMC, D_MODEL), lambda i, p: (jnp.minimum(i, n_prompt_tiles - 1), 0)),
                     pl.BlockSpec((NS, D_MODEL), lambda i, p: (0, 0))]
        out_shape = [jax.ShapeDtypeStruct((NP, D_MODEL), f32), jax.ShapeDtypeStruct((NS, D_MODEL), f32)]
    else:
        grid = (M_ALL // TMC,)
        scp, scs = mod_specs(layer + 1, 1)
        shp, shs = mod_specs(layer + 1, 0)
        in_specs += [pl.BlockSpec((1, D_MODEL), lambda i, p: (0, 0)), scp, shp, scs, shs]
        args += [next_norm_g.reshape(1, D_MODEL), modp, modp, mods, mods]
        out_specs = [pl.BlockSpec((TMC, D_MODEL), lambda i, p: (i, 0)),
                     pl.BlockSpec((TMC, D_MODEL), lambda i, p: (i, 0))]
        out_shape = [jax.ShapeDtypeStruct((M_ALL, D_MODEL), f32), jax.ShapeDtypeStruct((M_ALL, D_MODEL), bf16)]
    return pl.pallas_call(
        functools.partial(_combine_body, split=split),
        grid_spec=pltpu.PrefetchScalarGridSpec(
            num_scalar_prefetch=1,
            grid=grid,
            in_specs=in_specs,
            out_specs=out_specs,
            scratch_shapes=[pltpu.VMEM((2, TMC, D_MODEL), f32), pltpu.VMEM((2, TMC, D_MODEL), f32),
                            pltpu.SemaphoreType.DMA((2,))],
        ),
        out_shape=out_shape,
        compiler_params=_cparams(("arbitrary",)),
        name="moe_combine",
    )(pos_flat, *args)


def _moe(x_all, g, modp, mods, layer, router_wt, router_b, w1, w3, w2, next_norm_g):
    h, idx, wts, rank, cnt = _moe_pre(x_all, g, modp, mods, layer, router_wt, router_b)
    counts = cnt[:, 0].astype(i32)
    padded = ((counts + TME - 1) // TME) * TME
    ends = jnp.cumsum(padded)
    offs = ends - padded
    real = (jnp.arange(M_ALL) < N_REAL)[None, :]
    expert_ids = jnp.arange(N_EXPERTS, dtype=i32)[:, None, None]
    seg_start = jnp.sum(jnp.where(idx[None] == expert_ids, offs[:, None, None], 0), axis=0)
    pos = jnp.where(real, seg_start + rank, 0).astype(i32)
    pos_flat = pos.reshape(2 * M_ALL)
    n_used = (ends[-1] // TME).astype(i32).reshape(1)
    tile_start = jnp.minimum(jnp.arange(NT_E, dtype=i32), n_used[0] - 1) * TME
    tile_expert = jnp.sum((tile_start[:, None] >= ends[None, :]).astype(i32), axis=1).astype(i32)
    hs = _dispatch(pos_flat, (offs + counts).astype(i32), ends.astype(i32), n_used, h)
    y = _experts(tile_expert, n_used, hs, w1, w3, w2, layer)
    return _combine(pos_flat, x_all, wts.T, modp, mods, layer, y, next_norm_g)


def _sample_rows(a, width):
    return a.reshape(DEC_BATCH, SP, width)[:, :DEC_SEQ].reshape(NS, width)


def kernel(x_prompt, x_sample, c_prompt, c_sample, state_pool, cache_k, cache_v, state_hgrn, norm_g, ada_w, ada_b, w_in_ab, pool_w, pool_scale, q_norm, k_norm, attn_sinks, w_out_ab, w_in_c, lb_param, o_norm, w_out_c, router_w, router_b, moe_w1, moe_w3, moe_w2):
    n_c = BATCH + DEC_BATCH
    c_rows = ((n_c + 7) // 8) * 8
    c_all = jnp.concatenate([c_prompt, c_sample, jnp.zeros((c_rows - n_c, D_MODEL), f32)], axis=0)
    mod = _ada_mod(c_all, ada_w, ada_b).reshape(DEPTH, c_rows, 6, D_MODEL)
    modp = mod[:, :BATCH].transpose(0, 2, 1, 3).reshape(DEPTH, 6, BATCH, 1, D_MODEL)
    mods = jnp.repeat(mod[:, BATCH:n_c].transpose(0, 2, 1, 3), DEC_SEQ, axis=2)

    x_in = (x_prompt.reshape(NP, D_MODEL), x_sample.reshape(NS, D_MODEL))
    router_wt = router_w.T

    h = _normmod(x_in, norm_g[0, 0], modp, mods, 0, 0, 1)
    u = _proj(h, w_in_ab[0], 1280)
    tp = 256
    pool_p, qn_p, kn_p = _mix0_pre(
        u, u, lambda i: (jnp.maximum(i * (tp // 16) - 1, 0), 0), NP, tp, SEQ // tp, False,
        pool_w[0], pool_scale[0], q_norm[0], k_norm[0])
    nb = SEQ // WINDOW
    blk = lambda b, n: (b * nb + n, 0)
    prv = lambda b, n: (jnp.maximum(b * nb + n - 1, 0), 0)
    vcol = (D_POOL + D_ATTN + D_KV) // D_KV
    attn_p = _attn_call(
        attn_sinks[0], qn_p, kn_p, u,
        (pl.BlockSpec((WINDOW, D_ATTN), blk), pl.BlockSpec((WINDOW, D_KV), prv), pl.BlockSpec((WINDOW, D_KV), blk),
         pl.BlockSpec((WINDOW, D_KV), lambda b, n: (jnp.maximum(b * nb + n - 1, 0), vcol)),
         pl.BlockSpec((WINDOW, D_KV), lambda b, n: (b * nb + n, vcol)),
         pl.BlockSpec((WINDOW, D_ATTN), blk)),
        (BATCH, nb), NP, WINDOW, True, 1)
    u_s = u[NP:N_REAL].reshape(DEC_BATCH, DEC_SEQ, D_IN_AB)
    u_s16 = jnp.pad(u_s, ((0, 0), (0, SP - DEC_SEQ), (0, 0))).reshape(NSP, D_IN_AB)
    halo_s = jnp.pad(state_pool[0], ((0, 0), (1, 0), (0, 0))).reshape(DEC_BATCH * 16, D_POOL)
    pool_s, qn_s, kn_s = _mix0_pre(
        u_s16, halo_s, lambda i: (i, 0), NSP, SP, 1, True,
        pool_w[0], pool_scale[0], q_norm[0], k_norm[0])
    kn_new = kn_s.reshape(DEC_BATCH, SP, D_KV)[:, :DEC_SEQ]
    v_new = u_s[:, :, D_POOL + D_ATTN + D_KV:]
    zpad = jnp.zeros((DEC_BATCH, WINDOW - DEC_SEQ, D_KV), f32)
    k_ext = jnp.concatenate([cache_k[0].reshape(DEC_BATCH, WINDOW, D_KV), kn_new, zpad], axis=1)
    v_ext = jnp.concatenate([cache_v[0].reshape(DEC_BATCH, WINDOW, D_KV), v_new, zpad], axis=1)
    sg = 4
    ext_p = pl.BlockSpec((sg, WINDOW, D_KV), lambda b: (b, 0, 0))
    ext_c = pl.BlockSpec((sg, WINDOW, D_KV), lambda b: (b, 1, 0))
    attn_s = _attn_call(
        attn_sinks[0], qn_s, k_ext, v_ext,
        (pl.BlockSpec((sg * SP, D_ATTN), lambda b: (b, 0)), ext_p, ext_c, ext_p, ext_c,
         pl.BlockSpec((sg * SP, D_ATTN), lambda b: (b, 0))),
        (DEC_BATCH // sg,), NSP, SP, False, sg)
    x_all = _oproj([pool_p, attn_p], [_sample_rows(pool_s, D_POOL), _sample_rows(attn_s, D_ATTN)],
                   w_out_ab[0], x_in, modp, mods, 0, 2)
    x_all, h = _moe(x_all, norm_g[0, 1], modp, mods, 0, router_wt, router_b, moe_w1, moe_w3, moe_w2,
                    norm_g[1, 0])

    lb_soft = jax.nn.softmax(lb_param.astype(f32), axis=0)
    lower_bounds = jnp.cumsum(lb_soft, axis=0) - lb_soft[0:1]
    u1 = _proj(h, w_in_c[0], 2048)
    og_p, hg_p = _hgrn(u1, lower_bounds[1], o_norm[0], None,
                       nb=BATCH, nc=SEQ // 64, L=64, C=16, hpb=C_HEADS, n_valid=64, rows=NP)
    u1_s = jnp.pad(u1[NP:N_REAL].reshape(DEC_BATCH, DEC_SEQ, 4 * D_C),
                   ((0, 0), (0, SP - DEC_SEQ), (0, 0))).reshape(NSP, 4 * D_C)
    og_s, hg_s = _hgrn(u1_s, lower_bounds[1], o_norm[0], state_hgrn[0],
                       nb=DEC_BATCH, nc=1, L=SP, C=SP, hpb=C_HEADS, n_valid=DEC_SEQ, rows=NSP)
    x_all = _oproj([og_p], [_sample_rows(og_s, D_C)], w_out_c[0], (x_all,), modp, mods, 1, 2)
    y_p, y_s = _moe(x_all, norm_g[1, 1], modp, mods, 1, router_wt, router_b, moe_w1, moe_w3, moe_w2, None)

    y_prompt = y_p.reshape(BATCH, SEQ, D_MODEL)
    y_sample = y_s.reshape(DEC_BATCH, DEC_SEQ, D_MODEL)
    def seq_tail(a, n, c0, c1):
        return jnp.stack([a[(b + 1) * SEQ - n:(b + 1) * SEQ, c0:c1] for b in range(BATCH)])

    pool_prompt = seq_tail(u, POOL_STATE, 0, D_POOL)[None]
    pool_sample = jnp.concatenate([state_pool[0], u_s[:, :, :D_POOL]], axis=1)[:, -POOL_STATE:][None]
    kv_shape = (N_KV_HEADS, HEAD_DIM)
    k_prompt = seq_tail(kn_p, WINDOW, 0, D_KV).reshape(BATCH, WINDOW, *kv_shape)[None]
    v_prompt = seq_tail(u, WINDOW, D_POOL + D_ATTN + D_KV, D_IN_AB).reshape(BATCH, WINDOW, *kv_shape)[None]
    k_sample = jnp.concatenate([cache_k[0], kn_new.reshape(DEC_BATCH, DEC_SEQ, *kv_shape)], axis=1)[:, -WINDOW:][None]
    v_sample = jnp.concatenate([cache_v[0], v_new.reshape(DEC_BATCH, DEC_SEQ, *kv_shape)], axis=1)[:, -WINDOW:][None]
    return (y_prompt, y_sample, pool_prompt, pool_sample, k_prompt, k_sample, v_prompt, v_sample,
            hg_p[None], hg_s[None])
```

```python
import functools
import math

import numpy as np
import jax
import jax.numpy as jnp
from jax import lax
from jax.experimental import pallas as pl
from jax.experimental.pallas import tpu as pltpu

f32 = jnp.float32
bf16 = jnp.bfloat16
i32 = jnp.int32

D_MODEL = 2048
BATCH = 4
SEQ = 2048
DEPTH = 2
DEC_BATCH = 32
DEC_SEQ = 4
EPS = 1e-6
POOL_WINDOWS = (2, 4, 8, 16)
D_POOL = 512
POOL_GROUP = 128
POOL_STATE = 15
HEAD_DIM = 64
N_Q_HEADS = 24
N_KV_HEADS = 4
Q_PER_KV = 6
WINDOW = 128
D_ATTN = 1536
D_KV = 256
D_IN_AB = 2560
C_DK = 128
C_HEADS = 16
D_C = 2048
N_EXPERTS = 16
EXP_PER_GROUP = 4
D_FF = 512
NEG = -1e30
LOG2E = math.log2(math.e)

LANES = 128
VMEM_LIMIT = 56 * 1024 * 1024

NP = BATCH * SEQ
NS = DEC_BATCH * DEC_SEQ
N_REAL = NP + NS
TM = 512
M_ALL = ((N_REAL + TM - 1) // TM) * TM
N_TILES = M_ALL // TM
NPT = NP // TM
TPB = SEQ // TM
SP = 16
NSP = DEC_BATCH * SP

TME = 256
R_MAX = ((2 * N_REAL + N_EXPERTS * (TME - 1)) // TME + 1) * TME
NT_E = R_MAX // TME
TMC = 128
TMD = 640

FACTORED_BLOCK = 32
DECAY_CLAMP = 60.0


def _cparams(sem, vmem=VMEM_LIMIT):
    return pltpu.CompilerParams(dimension_semantics=sem, vmem_limit_bytes=vmem)


def _silu(x):
    h = 0.5 * x
    return h * jnp.tanh(h) + h


def _nt_dot(a, b, **kw):
    return lax.dot_general(a, b, (((1,), (1,)), ((), ())), preferred_element_type=f32, **kw)


def _tn_dot(a, b):
    return lax.dot_general(a, b, (((0,), (0,)), ((), ())), preferred_element_type=f32)


def _ada_body(c_ref, w_ref, b_ref, o_ref):
    s = _silu(c_ref[...]).astype(bf16)
    o_ref[...] = jnp.dot(s, w_ref[...].astype(bf16), preferred_element_type=f32) + b_ref[...]


def _ada_mod(c_all, ada_w, ada_b):
    depth, d, e = ada_w.shape
    cr = c_all.shape[0]
    tn = 1024
    return pl.pallas_call(
        _ada_body,
        grid=(depth, e // tn),
        in_specs=[
            pl.BlockSpec((cr, d), lambda l, j: (0, 0)),
            pl.BlockSpec((None, d, tn), lambda l, j: (l, 0, j)),
            pl.BlockSpec((None, 1, tn), lambda l, j: (l, 0, j)),
        ],
        out_specs=pl.BlockSpec((None, cr, tn), lambda l, j: (l, 0, j)),
        out_shape=jax.ShapeDtypeStruct((depth, cr, e), f32),
        compiler_params=_cparams(("arbitrary", "arbitrary")),
        name="ada_mod",
    )(c_all, ada_w, ada_b.reshape(depth, 1, e))


def _mod_specs(layer, k):
    p = pl.BlockSpec((None, None, None, 1, D_MODEL),
                     lambda i: (layer, k, jnp.minimum(i // TPB, BATCH - 1), 0, 0))
    s = pl.BlockSpec((None, None, NS, D_MODEL), lambda i: (layer, k, 0, 0))
    return p, s


def _norm_rows(x, g):
    return x * lax.rsqrt(jnp.mean(x * x, axis=-1, keepdims=True) + EPS) * g


def _x_specs(x, width, col):
    if len(x) == 1:
        return [pl.BlockSpec((TM, width), lambda *g: (g[-1], col(*g)))]
    return [pl.BlockSpec((TM, width), lambda *g: (jnp.minimum(g[-1], NPT - 1), col(*g))),
            pl.BlockSpec((NS, width), lambda *g: (0, col(*g)))]


def _normmod_body(*refs):
    *x_refs, g_ref, scp_ref, shp_ref, scs_ref, shs_ref, h_ref = refs
    i = pl.program_id(0)

    @pl.when(i < NPT)
    def _():
        y = _norm_rows(x_refs[0][...], g_ref[...])
        h_ref[...] = (y * (1.0 + scp_ref[...]) + shp_ref[...]).astype(h_ref.dtype)

    @pl.when(i >= NPT)
    def _():
        y = _norm_rows(x_refs[-1][:NS], g_ref[...])
        h_ref[:NS] = (y * (1.0 + scs_ref[...]) + shs_ref[...]).astype(h_ref.dtype)
        h_ref[NS:] = jnp.zeros((TM - NS, D_MODEL), h_ref.dtype)


def _normmod(x, g, modp, mods, layer, k_shift, k_scale):
    scp, scs = _mod_specs(layer, k_scale)
    shp, shs = _mod_specs(layer, k_shift)
    return pl.pallas_call(
        _normmod_body,
        grid=(N_TILES,),
        in_specs=_x_specs(x, D_MODEL, lambda i: 0) + [pl.BlockSpec((1, D_MODEL), lambda i: (0, 0)),
                                                       scp, shp, scs, shs],
        out_specs=pl.BlockSpec((TM, D_MODEL), lambda i: (i, 0)),
        out_shape=jax.ShapeDtypeStruct((M_ALL, D_MODEL), bf16),
        compiler_params=_cparams(("arbitrary",)),
        name="normmod",
    )(*x, g.reshape(1, D_MODEL), modp, modp, mods, mods)


def _proj_body(h_ref, w_ref, o_ref, wb_ref):
    @pl.when(pl.program_id(1) == 0)
    def _():
        wb_ref[...] = w_ref[...].astype(bf16)

    o_ref[...] = jnp.dot(h_ref[...], wb_ref[...], preferred_element_type=f32)


def _proj(h_all, w, tn):
    k, e = w.shape
    return pl.pallas_call(
        _proj_body,
        grid=(e // tn, N_TILES),
        in_specs=[pl.BlockSpec((TM, k), lambda j, i: (i, 0)),
                  pl.BlockSpec((k, tn), lambda j, i: (0, j))],
        out_specs=pl.BlockSpec((TM, tn), lambda j, i: (i, j)),
        out_shape=jax.ShapeDtypeStruct((M_ALL, e), f32),
        scratch_shapes=[pltpu.VMEM((k, tn), bf16)],
        compiler_params=_cparams(("arbitrary", "arbitrary")),
        name="proj",
    )(h_all, w)


def _oproj_body(*refs, n_lhs):
    ap_refs = refs[:n_lhs]
    as_refs = refs[n_lhs:2 * n_lhs]
    w_ref, *x_refs, gp_ref, gs_ref, o_ref, wb_ref = refs[2 * n_lhs:]
    i = pl.program_id(1)

    @pl.when(i == 0)
    def _():
        wb_ref[...] = w_ref[...].astype(bf16)

    def matmul(a_refs, rows):
        off, acc = 0, None
        for a in a_refs:
            kk = a.shape[1]
            part = jnp.dot(a[:rows], wb_ref[off:off + kk, :], preferred_element_type=f32)
            acc = part if acc is None else acc + part
            off += kk
        return acc

    @pl.when(i < NPT)
    def _():
        o_ref[...] = x_refs[0][...] + gp_ref[...] * matmul(ap_refs, TM)

    @pl.when(i >= NPT)
    def _():
        o_ref[:NS] = x_refs[-1][:NS] + gs_ref[...] * matmul(as_refs, NS)
        o_ref[NS:] = jnp.zeros((TM - NS, o_ref.shape[1]), f32)


def _oproj(a_prompt, a_sample, w, x, modp, mods, layer, k_gate):
    n_lhs = len(a_prompt)
    kdim, e = w.shape
    tn = 1024
    in_specs = []
    for a in a_prompt:
        in_specs.append(pl.BlockSpec((TM, a.shape[1]), lambda j, i: (jnp.minimum(i, NPT - 1), 0)))
    for a in a_sample:
        in_specs.append(pl.BlockSpec((NS, a.shape[1]), lambda j, i: (0, 0)))
    in_specs += [pl.BlockSpec((kdim, tn), lambda j, i: (0, j))] + _x_specs(x, tn, lambda j, i: j) + [
        pl.BlockSpec((None, None, None, 1, tn),
                     lambda j, i: (layer, k_gate, jnp.minimum(i // TPB, BATCH - 1), 0, j)),
        pl.BlockSpec((None, None, NS, tn), lambda j, i: (layer, k_gate, 0, j)),
    ]
    return pl.pallas_call(
        functools.partial(_oproj_body, n_lhs=n_lhs),
        grid=(e // tn, N_TILES),
        in_specs=in_specs,
        out_specs=pl.BlockSpec((TM, tn), lambda j, i: (i, j)),
        out_shape=jax.ShapeDtypeStruct((M_ALL, e), f32),
        scratch_shapes=[pltpu.VMEM((kdim, tn), bf16)],
        compiler_params=_cparams(("arbitrary", "arbitrary")),
        name="oproj",
    )(*a_prompt, *a_sample, w, *x, modp, mods)


def _head_norm(x, w_row, scale):
    lane = lax.broadcasted_iota(i32, x.shape, 1)
    lo = lane < HEAD_DIM
    ss = x * x
    m_lo = jnp.sum(jnp.where(lo, ss, 0.0), axis=-1, keepdims=True)
    m_hi = jnp.sum(jnp.where(lo, 0.0, ss), axis=-1, keepdims=True)
    r = jnp.where(lo, lax.rsqrt(m_lo * (1.0 / HEAD_DIM) + EPS), lax.rsqrt(m_hi * (1.0 / HEAD_DIM) + EPS))
    return x * r * (w_row * scale)


def _mix0_pre_body(cur_ref, halo_ref, q0_ref, q1_ref, q2_ref, k_ref, pw_ref, ps_ref, qw_ref, kw_ref,
                   pool_ref, qn_ref, kn_ref, ext_ref, *, tp, tiles_per_seq, full_count):
    i = pl.program_id(0)
    halo = halo_ref[...]
    if not full_count:
        halo = jnp.where(i % tiles_per_seq == 0, 0.0, halo)
    ext_ref[0:16, :] = halo
    ext_ref[16:16 + tp, :] = cur_ref[...]
    t_pos = (i % tiles_per_seq) * tp + lax.broadcasted_iota(i32, (tp, 1), 0)
    for g, w in enumerate(POOL_WINDOWS):
        ch = slice(g * POOL_GROUP, (g + 1) * POOL_GROUP)
        acc = ext_ref[16:16 + tp, ch]
        for dlt in range(1, w):
            acc = acc + ext_ref[16 - dlt:16 - dlt + tp, ch]
        if full_count:
            mean = acc * (1.0 / w)
        else:
            cnt = jnp.minimum(t_pos + 1, w).astype(f32)
            mean = acc / cnt
        d = mean - ext_ref[16:16 + tp, ch]
        y = jnp.dot(d.astype(bf16), pw_ref[g].astype(bf16), preferred_element_type=f32)
        pool_ref[:, ch] = (y * ps_ref[:, ch]).astype(pool_ref.dtype)
    for c3, q_ref in enumerate((q0_ref, q1_ref, q2_ref)):
        for c in range(4):
            sl = slice(c * LANES, (c + 1) * LANES)
            dst = slice((c3 * 4 + c) * LANES, (c3 * 4 + c + 1) * LANES)
            qn_ref[:, dst] = _head_norm(q_ref[:, sl], qw_ref[...], LOG2E * HEAD_DIM ** -0.5).astype(qn_ref.dtype)
    for c in range(2):
        sl = slice(c * LANES, (c + 1) * LANES)
        kn_ref[:, sl] = _head_norm(k_ref[:, sl], kw_ref[...], 1.0)


def _mix0_pre(u, halo_src, halo_map, rows, tp, tiles_per_seq, full_count, pool_w, pool_scale, q_norm, k_norm):
    qw = jnp.tile(q_norm.reshape(1, HEAD_DIM), (1, 2))
    kw = jnp.tile(k_norm.reshape(1, HEAD_DIM), (1, 2))
    body = functools.partial(_mix0_pre_body, tp=tp, tiles_per_seq=tiles_per_seq, full_count=full_count)
    return pl.pallas_call(
        body,
        grid=(rows // tp,),
        in_specs=[
            pl.BlockSpec((tp, D_POOL), lambda i: (i, 0)),
            pl.BlockSpec((16, D_POOL), halo_map),
            pl.BlockSpec((tp, 512), lambda i: (i, 1)),
            pl.BlockSpec((tp, 512), lambda i: (i, 2)),
            pl.BlockSpec((tp, 512), lambda i: (i, 3)),
            pl.BlockSpec((tp, D_KV), lambda i: (i, (D_POOL + D_ATTN) // D_KV)),
            pl.BlockSpec((4, POOL_GROUP, POOL_GROUP), lambda i: (0, 0, 0)),
            pl.BlockSpec((1, D_POOL), lambda i: (0, 0)),
            pl.BlockSpec((1, LANES), lambda i: (0, 0)),
            pl.BlockSpec((1, LANES), lambda i: (0, 0)),
        ],
        out_specs=[
            pl.BlockSpec((tp, D_POOL), lambda i: (i, 0)),
            pl.BlockSpec((tp, D_ATTN), lambda i: (i, 0)),
            pl.BlockSpec((tp, D_KV), lambda i: (i, 0)),
        ],
        out_shape=[
            jax.ShapeDtypeStruct((rows, D_POOL), bf16),
            jax.ShapeDtypeStruct((rows, D_ATTN), bf16),
            jax.ShapeDtypeStruct((rows, D_KV), f32),
        ],
        scratch_shapes=[pltpu.VMEM((16 + tp, D_POOL), f32)],
        compiler_params=_cparams(("arbitrary",)),
        name="mix0_pre",
    )(u, halo_src, u, u, u, u, pool_w, pool_scale.reshape(1, D_POOL), qw, kw)


def _alibi_slopes(n):
    def pow2_slopes(m):
        start = 2.0 ** (-8.0 / m)
        return [start ** (i + 1) for i in range(m)]
    if n & (n - 1) == 0:
        s = pow2_slopes(n)
    else:
        c = 2 ** int(math.floor(math.log2(n)))
        s = pow2_slopes(c) + pow2_slopes(2 * c)[0::2][: n - c]
    return [float(v) for v in np.asarray(s, dtype=np.float32)]


_SLOPES = _alibi_slopes(N_Q_HEADS)


def _attn_body(sink_ref, q_ref, kp_ref, kc_ref, vp_ref, vc_ref, o_ref, bias_ref, *, tq, prompt, group):
    first = pl.program_id(0) == 0
    if prompt:
        first = first & (pl.program_id(1) == 0)

    @pl.when(first)
    def _():
        qi = lax.broadcasted_iota(i32, (tq, WINDOW), 0)
        kj = lax.broadcasted_iota(i32, (tq, WINDOW), 1)
        dist_p = (qi + WINDOW - kj).astype(f32)
        dist_c = (qi - kj).astype(f32)
        neg_p = jnp.where(qi + WINDOW - kj < WINDOW, 0.0, NEG)
        neg_c = jnp.where(qi - kj >= 0, 0.0, NEG)
        for head in range(N_Q_HEADS):
            bias_ref[head, 0] = neg_p - (_SLOPES[head] * LOG2E) * dist_p
            bias_ref[head, 1] = neg_c - (_SLOPES[head] * LOG2E) * dist_c

    lane = lax.broadcasted_iota(i32, (WINDOW, LANES), 1)
    lo = lane < HEAD_DIM
    qlane = lax.broadcasted_iota(i32, (tq, LANES), 1) < HEAD_DIM

    for g in range(group):
        q_rows_g = slice(g * tq, (g + 1) * tq)

        def kv(ref, cols):
            return ref[:, cols] if group == 1 else ref[g, :, cols]

        for j in range(N_KV_HEADS):
            pair = slice((j // 2) * LANES, (j // 2 + 1) * LANES)
            own_lo = (j % 2 == 0)

            def dup(kpair):
                sw = pltpu.roll(kpair, HEAD_DIM, 1)
                return jnp.where(lo, kpair, sw) if own_lo else jnp.where(lo, sw, kpair)

            def halves(vpair):
                own = jnp.where(lo, vpair, 0.0) if own_lo else jnp.where(lo, 0.0, vpair)
                sw = pltpu.roll(own, HEAD_DIM, 1)
                return (own, sw) if own_lo else (sw, own)

            kk_p = dup(kv(kp_ref, pair)).astype(bf16)
            kk_c = dup(kv(kc_ref, pair)).astype(bf16)
            v_p = [h.astype(bf16) for h in halves(kv(vp_ref, pair))]
            v_c = [h.astype(bf16) for h in halves(kv(vc_ref, pair))]

            heads = [(hf, c) for hf in (0, 1) for c in range(3)]
            q_rows = []
            for hf, c in heads:
                ch = slice((3 * j + c) * LANES, (3 * j + c + 1) * LANES)
                qc = q_ref[q_rows_g, ch].astype(f32)
                q_rows.append(jnp.where(qlane, qc, 0.0) if hf == 0 else jnp.where(qlane, 0.0, qc))
            q_st = jnp.concatenate(q_rows, axis=0).astype(bf16)
            s_p = _nt_dot(q_st, kk_p)
            s_c = _nt_dot(q_st, kk_c)
            for c in range(3):
                out = None
                for hf in (0, 1):
                    head = 6 * j + 2 * c + hf
                    rows = slice((hf * 3 + c) * tq, (hf * 3 + c + 1) * tq)
                    bias_p = bias_ref[head, 0]
                    if prompt:
                        bias_p = jnp.where(pl.program_id(1) > 0, bias_p, NEG)
                    sp = s_p[rows] + bias_p
                    sc = s_c[rows] + bias_ref[head, 1]
                    sink = sink_ref[head] * LOG2E
                    m = jnp.maximum(jnp.max(jnp.maximum(sp, sc), axis=-1, keepdims=True), sink)
                    ep = jnp.exp2(sp - m)
                    ec = jnp.exp2(sc - m)
                    den = jnp.sum(ep + ec, axis=-1, keepdims=True) + jnp.exp2(sink - m)
                    o = (jnp.dot(ep.astype(bf16), v_p[hf], preferred_element_type=f32)
                         + jnp.dot(ec.astype(bf16), v_c[hf], preferred_element_type=f32)) * (1.0 / den)
                    out = o if out is None else out + o
                ch = slice((3 * j + c) * LANES, (3 * j + c + 1) * LANES)
                o_ref[q_rows_g, ch] = out.astype(o_ref.dtype)


def _attn_call(sinks, q, k_arr, v_arr, specs, grid, rows, tq, prompt, group):
    body = functools.partial(_attn_body, tq=tq, prompt=prompt, group=group)
    q_spec, kp_spec, kc_spec, vp_spec, vc_spec, o_spec = specs
    return pl.pallas_call(
        body,
        grid=grid,
        in_specs=[pl.BlockSpec(memory_space=pltpu.SMEM), q_spec, kp_spec, kc_spec, vp_spec, vc_spec],
        out_specs=o_spec,
        out_shape=jax.ShapeDtypeStruct((rows, D_ATTN), bf16),
        scratch_shapes=[pltpu.VMEM((N_Q_HEADS, 2, tq, WINDOW), f32)],
        compiler_params=_cparams(("arbitrary",) * len(grid)),
        name="attn",
    )(sinks, q, k_arr, k_arr, v_arr, v_arr)


def _hgrn_inputs(q_ref, f_ref, i_ref, lb_ref, sl, *, L, n_valid):
    lb = lb_ref[:, sl]
    fp = f_ref[:, sl]
    q = _silu(q_ref[:, sl])
    half_th = 0.5 * jnp.tanh(0.5 * fp)
    logf = jnp.log(lb + (1.0 - lb) * (0.5 + half_th))
    kk = (1.0 - lb) * (0.5 - half_th)
    if n_valid < L:
        valid = lax.broadcasted_iota(i32, (L, 1), 0) < n_valid
        logf = jnp.where(valid, logf, 0.0)
        kk = jnp.where(valid, kk, 0.0)
    return q, kk, i_ref[:, sl], logf


def _cum_decay(logf, L):
    row = lax.broadcasted_iota(i32, (L, L), 0)
    col = lax.broadcasted_iota(i32, (L, L), 1)
    tri = (row >= col).astype(bf16)
    h1 = logf.astype(bf16)
    r1 = logf - h1.astype(f32)
    h2 = r1.astype(bf16)
    h3 = (r1 - h2.astype(f32)).astype(bf16)
    return (jnp.dot(tri, h1, preferred_element_type=f32) + jnp.dot(tri, h2, preferred_element_type=f32)
            + jnp.dot(tri, h3, preferred_element_type=f32))


def _sub_ref(b, lo):
    return b[lo - 1:lo] if lo > 0 else jnp.zeros((1, b.shape[1]), f32)


def _factored_operands(q, kk, b, *, L, CF):
    rowid = lax.broadcasted_iota(i32, (L, 1), 0)
    qts, kts = [], []
    for lo in range(0, L, CF):
        r = _sub_ref(b, lo)
        qts.append((q[lo:lo + CF] * jnp.exp(b[lo:lo + CF] - r)).astype(bf16))
        kts.append(jnp.where(rowid < lo + CF, kk * jnp.exp(jnp.minimum(r - b, DECAY_CLAMP)), 0.0).astype(bf16))
    return qts, kts


def _decay_span(b, *, L, CF):
    span = None
    for lo in range(0, L, CF):
        d = _sub_ref(b, lo) - b[lo + CF - 1:lo + CF]
        span = d if span is None else jnp.maximum(span, d)
    return span


def _intra_pairwise(q, kk, v, b, *, L, C):
    vb = v.astype(bf16)
    rowid = lax.broadcasted_iota(i32, (L, 1), 0)
    tloc = lax.broadcasted_iota(i32, (C, 1), 0)
    outs = []
    for lo in range(0, L, C):
        bi, qi, ki, vi = b[lo:lo + C], q[lo:lo + C], kk[lo:lo + C], v[lo:lo + C]
        if lo > 0:
            r = _sub_ref(b, lo)
            qt = (qi * jnp.exp(bi - r)).astype(bf16)
            kt = jnp.where(rowid < lo, kk * jnp.exp(jnp.minimum(r - b, 0.0)), 0.0).astype(bf16)
            oi = jnp.dot(_nt_dot(qt, kt).astype(bf16), vb, preferred_element_type=f32)
        else:
            oi = jnp.zeros((C, LANES), f32)
        for s in range(C):
            e = jnp.exp(jnp.minimum(bi - bi[s:s + 1], 0.0))
            a = jnp.sum(qi * ki[s:s + 1] * e, axis=-1, keepdims=True)
            oi = oi + jnp.where(tloc >= s, a, 0.0) * vi[s:s + 1]
        outs.append(oi)
    return outs[0] if len(outs) == 1 else jnp.concatenate(outs, axis=0)


def _hgrn_body(*refs, L, C, hpb, n_valid, has_state):
    if has_state:
        q_ref, f_ref, i_ref, g_ref, lb_ref, on_ref, s0_ref, og_ref, so_ref, st_ref, inter_ref = refs
    else:
        q_ref, f_ref, i_ref, g_ref, lb_ref, on_ref, og_ref, so_ref, st_ref, inter_ref = refs
    c = pl.program_id(2)
    nc = pl.num_programs(2)
    CF = min(L, FACTORED_BLOCK)

    @pl.when(c == 0)
    def _():
        for hh in range(hpb):
            st_ref[hh] = s0_ref[hh].T if has_state else jnp.zeros((C_DK, C_DK), f32)

    def finish(o, sl):
        o = o * lax.rsqrt(jnp.mean(o * o, axis=-1, keepdims=True) + EPS) * on_ref[...]
        og_ref[:, sl] = (o * _silu(g_ref[:, sl])).astype(og_ref.dtype)

    q, kk, v, logf = _hgrn_inputs(q_ref, f_ref, i_ref, lb_ref, slice(0, hpb * LANES), L=L, n_valid=n_valid)
    b = _cum_decay(logf, L)
    vb = v.astype(bf16)
    q_hat = (q * jnp.exp(b)).astype(bf16)
    bl = b[L - 1:L]
    k_end = (kk * jnp.exp(bl - b)).astype(bf16)
    st_decay = jnp.exp(bl)
    qts, kts = _factored_operands(q, kk, b, L=L, CF=CF)
    span = _decay_span(b, L=L, CF=CF)
    causal = lax.broadcasted_iota(i32, (L, L), 0) >= lax.broadcasted_iota(i32, (L, L), 1)
    for hh in range(hpb):
        sl = slice(hh * LANES, (hh + 1) * LANES)
        st = st_ref[hh]
        o_inter = _nt_dot(q_hat[:, sl], st.astype(bf16))
        st_ref[hh] = st * st_decay[:, sl] + _tn_dot(vb[:, sl], k_end[:, sl])
        inter_ref[hh] = o_inter
        blocks = [_nt_dot(qt[:, sl], kt[:, sl]) for qt, kt in zip(qts, kts)]
        a = blocks[0] if len(blocks) == 1 else jnp.concatenate(blocks, axis=0)
        a = jnp.where(causal, a, 0.0).astype(bf16)
        finish(jnp.dot(a, vb[:, sl], preferred_element_type=f32) + o_inter, sl)

    @pl.when(jnp.max(span) >= DECAY_CLAMP)
    def _():
        for hh in range(hpb):
            sl = slice(hh * LANES, (hh + 1) * LANES)
            q, kk, v, logf = _hgrn_inputs(q_ref, f_ref, i_ref, lb_ref, sl, L=L, n_valid=n_valid)
            b = _cum_decay(logf, L)
            finish(_intra_pairwise(q, kk, v, b, L=L, C=C) + inter_ref[hh], sl)

    @pl.when(c == nc - 1)
    def _():
        for hh in range(hpb):
            so_ref[hh] = st_ref[hh].T


def _hgrn(u, lb, o_norm, s0, *, nb, nc, L, C, hpb, n_valid, rows):
    nh = C_HEADS // hpb
    w = hpb * LANES
    has_state = s0 is not None
    body = functools.partial(_hgrn_body, L=L, C=C, hpb=hpb, n_valid=n_valid, has_state=has_state)

    def uspec(part):
        return pl.BlockSpec((L, w), lambda b, h, c: (b * nc + c, part * nh + h))

    in_specs = [uspec(0), uspec(1), uspec(2), uspec(3),
                pl.BlockSpec((1, w), lambda b, h, c: (0, h)),
                pl.BlockSpec((1, LANES), lambda b, h, c: (0, 0))]
    args = [u, u, u, u, lb.reshape(1, D_C), o_norm.reshape(1, LANES)]
    if has_state:
        in_specs.append(pl.BlockSpec((None, hpb, C_DK, C_DK), lambda b, h, c: (b, h, 0, 0)))
        args.append(s0)
    return pl.pallas_call(
        body,
        grid=(nb, nh, nc),
        in_specs=in_specs,
        out_specs=[pl.BlockSpec((L, w), lambda b, h, c: (b * nc + c, h)),
                   pl.BlockSpec((None, hpb, C_DK, C_DK), lambda b, h, c: (b, h, 0, 0))],
        out_shape=[jax.ShapeDtypeStruct((rows, D_C), bf16),
                   jax.ShapeDtypeStruct((nb, C_HEADS, C_DK, C_DK), f32)],
        scratch_shapes=[pltpu.VMEM((hpb, C_DK, C_DK), f32), pltpu.VMEM((hpb, L, LANES), f32)],
        compiler_params=_cparams(("arbitrary", "arbitrary", "arbitrary")),
        name="hgrn",
    )(*args)


def _moe_pre_body(x_ref, g_ref, scp_ref, shp_ref, scs_ref, shs_ref, wrt_ref, rb_ref,
                  h_ref, idx_ref, wts_ref, rank_ref, cnt_ref, carry):
    i = pl.program_id(0)

    @pl.when(i == 0)
    def _():
        carry[...] = jnp.zeros_like(carry)

    y = _norm_rows(x_ref[...], g_ref[...])

    @pl.when(i < NPT)
    def _():
        h_ref[...] = y * (1.0 + scp_ref[...]) + shp_ref[...]

    @pl.when(i >= NPT)
    def _():
        h_ref[:NS] = y[:NS] * (1.0 + scs_ref[...]) + shs_ref[...]
        h_ref[NS:] = jnp.zeros((TM - NS, D_MODEL), f32)

    h = h_ref[...]

    w_r = wrt_ref[...]
    w_hi = w_r.astype(bf16)
    w_lo = (w_r - w_hi.astype(f32)).astype(bf16)
    h_hi = h.astype(bf16)
    h_lo = (h - h_hi.astype(f32)).astype(bf16)
    logits = _nt_dot(w_hi, h_hi) + _nt_dot(w_hi, h_lo) + _nt_dot(w_lo, h_hi)
    scores = jax.nn.sigmoid(logits)
    sel = scores + rb_ref[...]
    best = jnp.zeros((1, TM), i32)
    best_v = None
    for g in range(N_EXPERTS // EXP_PER_GROUP):
        r = [sel[EXP_PER_GROUP * g + t:EXP_PER_GROUP * g + t + 1] for t in range(EXP_PER_GROUP)]
        top2 = None
        for a in range(EXP_PER_GROUP):
            for bb in range(a + 1, EXP_PER_GROUP):
                s2 = r[a] + r[bb]
                top2 = s2 if top2 is None else jnp.maximum(top2, s2)
        if g == 0:
            best_v = top2
        else:
            upd = top2 > best_v
            best = jnp.where(upd, g, best)
            best_v = jnp.where(upd, top2, best_v)
    eidx = lax.broadcasted_iota(i32, (N_EXPERTS, TM), 0)
    masked = jnp.where(jnp.right_shift(eidx, 2) == best, sel, NEG)
    m1 = jnp.max(masked, axis=0, keepdims=True)
    i1 = jnp.min(jnp.where(masked == m1, eidx, N_EXPERTS), axis=0, keepdims=True)
    masked2 = jnp.where(eidx == i1, -jnp.inf, masked)
    m2 = jnp.max(masked2, axis=0, keepdims=True)
    i2 = jnp.min(jnp.where(masked2 == m2, eidx, N_EXPERTS), axis=0, keepdims=True)
    w1 = jnp.sum(jnp.where(eidx == i1, scores, 0.0), axis=0, keepdims=True)
    w2 = jnp.sum(jnp.where(eidx == i2, scores, 0.0), axis=0, keepdims=True)
    den = w1 + w2
    idx_ref[0:1, :] = i1
    idx_ref[1:2, :] = i2
    wts_ref[0:1, :] = w1 / den
    wts_ref[1:2, :] = w2 / den

    colg = i * TM + lax.broadcasted_iota(i32, (1, TM), 1)
    oh = jnp.where(((eidx == i1) | (eidx == i2)) & (colg < N_REAL), 1.0, 0.0)
    rr = lax.broadcasted_iota(i32, (TM, TM), 0)
    cc = lax.broadcasted_iota(i32, (TM, TM), 1)
    before = (rr < cc).astype(bf16)
    rank = jnp.dot(oh.astype(bf16), before, preferred_element_type=f32) + carry[:, 0:1]
    rank_ref[0:1, :] = jnp.sum(jnp.where(eidx == i1, rank, 0.0), axis=0, keepdims=True).astype(i32)
    rank_ref[1:2, :] = jnp.sum(jnp.where(eidx == i2, rank, 0.0), axis=0, keepdims=True).astype(i32)
    carry[...] = carry[...] + jnp.sum(oh, axis=1, keepdims=True)
    cnt_ref[...] = carry[...]


def _moe_pre(x_all, g, modp, mods, layer, router_wt, router_b):
    scp, scs = _mod_specs(layer, 4)
    shp, shs = _mod_specs(layer, 3)
    return pl.pallas_call(
        _moe_pre_body,
        grid=(N_TILES,),
        in_specs=[pl.BlockSpec((TM, D_MODEL), lambda i: (i, 0)),
                  pl.BlockSpec((1, D_MODEL), lambda i: (0, 0)),
                  scp, shp, scs, shs,
                  pl.BlockSpec((N_EXPERTS, D_MODEL), lambda i: (0, 0)),
                  pl.BlockSpec((N_EXPERTS, 1), lambda i: (0, 0))],
        out_specs=[pl.BlockSpec((TM, D_MODEL), lambda i: (i, 0)),
                   pl.BlockSpec((2, TM), lambda i: (0, i)),
                   pl.BlockSpec((2, TM), lambda i: (0, i)),
                   pl.BlockSpec((2, TM), lambda i: (0, i)),
                   pl.BlockSpec((N_EXPERTS, LANES), lambda i: (0, 0))],
        out_shape=[jax.ShapeDtypeStruct((M_ALL, D_MODEL), f32),
                   jax.ShapeDtypeStruct((2, M_ALL), i32),
                   jax.ShapeDtypeStruct((2, M_ALL), f32),
                   jax.ShapeDtypeStruct((2, M_ALL), i32),
                   jax.ShapeDtypeStruct((N_EXPERTS, LANES), f32)],
        scratch_shapes=[pltpu.VMEM((N_EXPERTS, LANES), f32)],
        compiler_params=_cparams(("arbitrary",)),
        name="moe_pre",
    )(x_all, g.reshape(1, D_MODEL), modp, modp, mods, mods, router_wt, router_b.reshape(N_EXPERTS, 1))


def _dispatch_body(pos_ref, fill_lo_ref, fill_hi_ref, nu_ref, h_ref, hs_hbm, zero_buf, sem, tile_sem):
    i = pl.program_id(0)
    base = i * TMD

    def row_copy(src, dst_row):
        return pltpu.make_async_copy(src, hs_hbm.at[pl.ds(dst_row, 1)], sem)

    def wait_row(n, carry):
        row_copy(h_ref.at[pl.ds(0, 1)], 0).wait()
        return carry

    @pl.when(i == 0)
    def _():
        zero_buf[...] = jnp.zeros_like(zero_buf)
        zero_row = zero_buf.at[pl.ds(0, 1)]

        def tile_copy(t):
            return pltpu.make_async_copy(zero_buf, hs_hbm.at[pl.ds(t * TME, TME)], tile_sem)

        def tail_start(t, carry):
            tile_copy(t).start()
            return carry

        def tail_wait(t, carry):
            tile_copy(t).wait()
            return carry

        def fill_expert(e, carry):
            def fill(r, c2):
                row_copy(zero_row, r).start()
                return c2
            lax.fori_loop(fill_lo_ref[e], fill_hi_ref[e], fill, 0)
            lax.fori_loop(fill_lo_ref[e], fill_hi_ref[e], wait_row, 0)
            return carry

        lax.fori_loop(nu_ref[0], NT_E, tail_start, 0)
        lax.fori_loop(0, N_EXPERTS, fill_expert, 0)
        lax.fori_loop(nu_ref[0], NT_E, tail_wait, 0)

    def issue(r, carry):
        src = h_ref.at[pl.ds(r, 1)]
        row_copy(src, pos_ref[base + r]).start()
        row_copy(src, pos_ref[M_ALL + base + r]).start()
        return carry

    lax.fori_loop(0, TMD, issue, 0, unroll=8)
    lax.fori_loop(0, 2 * TMD, wait_row, 0, unroll=8)


def _dispatch(pos_flat, fill_lo, fill_hi, n_used, h):
    return pl.pallas_call(
        _dispatch_body,
        grid_spec=pltpu.PrefetchScalarGridSpec(
            num_scalar_prefetch=4,
            grid=(N_REAL // TMD,),
            in_specs=[pl.BlockSpec((TMD, D_MODEL), lambda i, *_: (i, 0))],
            out_specs=pl.BlockSpec(memory_space=pl.ANY),
            scratch_shapes=[pltpu.VMEM((TME, D_MODEL), f32), pltpu.SemaphoreType.DMA,
                            pltpu.SemaphoreType.DMA],
        ),
        out_shape=jax.ShapeDtypeStruct((R_MAX, D_MODEL), f32),
        compiler_params=_cparams(("arbitrary",)),
        name="moe_dispatch",
    )(pos_flat, fill_lo, fill_hi, n_used, h)


def _expert_body(te_ref, nu_ref, hs_ref, w1_ref, w3_ref, w2_ref, y_ref, w1b, w3b, w2b):
    t = pl.program_id(0)

    @pl.when(t < nu_ref[0])
    def _():
        changed = (t == 0) | (te_ref[t] != te_ref[jnp.maximum(t - 1, 0)])

        @pl.when(changed)
        def _():
            w1b[...] = w1_ref[...].astype(bf16)
            w3b[...] = w3_ref[...].astype(bf16)
            w2b[...] = w2_ref[...].astype(bf16)

        hb = hs_ref[...].astype(bf16)
        a = jnp.dot(hb, w1b[...], preferred_element_type=f32)
        b = jnp.dot(hb, w3b[...], preferred_element_type=f32)
        hid = (_silu(a) * b).astype(bf16)
        y_ref[...] = jnp.dot(hid, w2b[...], preferred_element_type=f32)

    @pl.when(t >= nu_ref[0])
    def _():
        y_ref[...] = jnp.zeros_like(y_ref)


def _experts(tile_expert, n_used, hs, w1, w3, w2, layer):
    def row_map(t, te, nu):
        return (jnp.minimum(t, nu[0] - 1), 0)

    def out_map(t, te, nu):
        return (t, 0)

    def w_map(t, te, nu):
        return (layer, te[t], 0, 0)

    return pl.pallas_call(
        _expert_body,
        grid_spec=pltpu.PrefetchScalarGridSpec(
            num_scalar_prefetch=2,
            grid=(NT_E,),
            in_specs=[pl.BlockSpec((TME, D_MODEL), row_map),
                      pl.BlockSpec((None, None, D_MODEL, D_FF), w_map),
                      pl.BlockSpec((None, None, D_MODEL, D_FF), w_map),
                      pl.BlockSpec((None, None, D_FF, D_MODEL), w_map)],
            out_specs=pl.BlockSpec((TME, D_MODEL), out_map),
            scratch_shapes=[pltpu.VMEM((D_MODEL, D_FF), bf16), pltpu.VMEM((D_MODEL, D_FF), bf16),
                            pltpu.VMEM((D_FF, D_MODEL), bf16)],
        ),
        out_shape=jax.ShapeDtypeStruct((R_MAX, D_MODEL), f32),
        compiler_params=_cparams(("arbitrary",)),
        name="moe_experts",
    )(tile_expert, n_used, hs, w1, w3, w2)


def _combine_body(pos_ref, x_ref, w_ref, gp_ref, gs_ref, y_hbm, *refs, split):
    if split:
        yp_ref, ys_ref, buf0, buf1, sems = refs
    else:
        ng_ref, scp_ref, shp_ref, scs_ref, shs_ref, o_ref, h_ref, buf0, buf1, sems = refs
    i = pl.program_id(0)
    n_real_tiles = N_REAL // TMC

    def copies(tile, r):
        slot = tile % 2
        base = tile * TMC
        return (pltpu.make_async_copy(y_hbm.at[pl.ds(pos_ref[base + r], 1)],
                                      buf0.at[slot, pl.ds(r, 1)], sems.at[slot]),
                pltpu.make_async_copy(y_hbm.at[pl.ds(pos_ref[M_ALL + base + r], 1)],
                                      buf1.at[slot, pl.ds(r, 1)], sems.at[slot]))

    def start_tile(tile):
        def start(r, c):
            for cp in copies(tile, r):
                cp.start()
            return c
        lax.fori_loop(0, TMC, start, 0, unroll=8)

    def wait_tile(tile):
        def wait(r, c):
            for cp in copies(tile, r):
                cp.wait()
            return c
        lax.fori_loop(0, TMC, wait, 0, unroll=8)

    @pl.when(i == 0)
    def _():
        start_tile(0)

    @pl.when(i + 1 < n_real_tiles)
    def _():
        start_tile(i + 1)

    def combined(gate):
        wait_tile(i)
        slot = i % 2
        m = w_ref[:, 0:1] * buf0[slot] + w_ref[:, 1:2] * buf1[slot]
        return x_ref[...] + gate * m

    def emit(gate, dst, sc_ref, sh_ref):
        o = combined(gate)
        dst[...] = o
        if not split:
            h_ref[...] = (_norm_rows(o, ng_ref[...]) * (1.0 + sc_ref[...]) + sh_ref[...]).astype(h_ref.dtype)

    @pl.when(i < NP // TMC)
    def _():
        emit(gp_ref[...], yp_ref if split else o_ref, None if split else scp_ref, None if split else shp_ref)

    @pl.when((i >= NP // TMC) & (i < n_real_tiles))
    def _():
        emit(gs_ref[...], ys_ref if split else o_ref, None if split else scs_ref, None if split else shs_ref)

    if not split:
        @pl.when(i >= n_real_tiles)
        def _():
            o_ref[...] = jnp.zeros_like(o_ref)
            h_ref[...] = jnp.zeros_like(h_ref)


def _combine(pos_flat, x_all, wts_t, modp, mods, layer, y, next_norm_g):
    split = next_norm_g is None
    tiles_per_seq = SEQ // TMC
    n_prompt_tiles = NP // TMC

    def mod_specs(lyr, k):
        return (pl.BlockSpec((None, None, None, 1, D_MODEL),
                             lambda i, p: (lyr, k, jnp.minimum(i // tiles_per_seq, BATCH - 1), 0, 0)),
                pl.BlockSpec((None, None, NS, D_MODEL), lambda i, p: (lyr, k, 0, 0)))

    gp, gs = mod_specs(layer, 5)
    in_specs = [pl.BlockSpec((TMC, D_MODEL), lambda i, p: (i, 0)),
                pl.BlockSpec((TMC, 2), lambda i, p: (i, 0)), gp, gs,
                pl.BlockSpec(memory_space=pl.ANY)]
    args = [x_all, wts_t, modp, mods, y]
    if split:
        grid = (N_REAL // TMC,)
        out_specs = [pl.BlockSpec((TMC, D_MODEL), lambda i, p: (jnp.minimum(i, n_prompt_tiles - 1), 0)),
                     pl.BlockSpec((NS, D_MODEL), lambda i, p: (0, 0))]
        out_shape = [jax.ShapeDtypeStruct((NP, D_MODEL), f32), jax.ShapeDtypeStruct((NS, D_MODEL), f32)]
    else:
        grid = (M_ALL // TMC,)
        scp, scs = mod_specs(layer + 1, 1)
        shp, shs = mod_specs(layer + 1, 0)
        in_specs += [pl.BlockSpec((1, D_MODEL), lambda i, p: (0, 0)), scp, shp, scs, shs]
        args += [next_norm_g.reshape(1, D_MODEL), modp, modp, mods, mods]
        out_specs = [pl.BlockSpec((TMC, D_MODEL), lambda i, p: (i, 0)),
                     pl.BlockSpec((TMC, D_MODEL), lambda i, p: (i, 0))]
        out_shape = [jax.ShapeDtypeStruct((M_ALL, D_MODEL), f32), jax.ShapeDtypeStruct((M_ALL, D_MODEL), bf16)]
    return pl.pallas_call(
        functools.partial(_combine_body, split=split),
        grid_spec=pltpu.PrefetchScalarGridSpec(
            num_scalar_prefetch=1,
            grid=grid,
            in_specs=in_specs,
            out_specs=out_specs,
            scratch_shapes=[pltpu.VMEM((2, TMC, D_MODEL), f32), pltpu.VMEM((2, TMC, D_MODEL), f32),
                            pltpu.SemaphoreType.DMA((2,))],
        ),
        out_shape=out_shape,
        compiler_params=_cparams(("arbitrary",)),
        name="moe_combine",
    )(pos_flat, *args)


def _moe(x_all, g, modp, mods, layer, router_wt, router_b, w1, w3, w2, next_norm_g):
    h, idx, wts, rank, cnt = _moe_pre(x_all, g, modp, mods, layer, router_wt, router_b)
    counts = cnt[:, 0].astype(i32)
    padded = ((counts + TME - 1) // TME) * TME
    ends = jnp.cumsum(padded)
    offs = ends - padded
    real = (jnp.arange(M_ALL) < N_REAL)[None, :]
    expert_ids = jnp.arange(N_EXPERTS, dtype=i32)[:, None, None]
    seg_start = jnp.sum(jnp.where(idx[None] == expert_ids, offs[:, None, None], 0), axis=0)
    pos = jnp.where(real, seg_start + rank, 0).astype(i32)
    pos_flat = pos.reshape(2 * M_ALL)
    n_used = (ends[-1] // TME).astype(i32).reshape(1)
    tile_start = jnp.minimum(jnp.arange(NT_E, dtype=i32), n_used[0] - 1) * TME
    tile_expert = jnp.sum((tile_start[:, None] >= ends[None, :]).astype(i32), axis=1).astype(i32)
    hs = _dispatch(pos_flat, (offs + counts).astype(i32), ends.astype(i32), n_used, h)
    y = _experts(tile_expert, n_used, hs, w1, w3, w2, layer)
    return _combine(pos_flat, x_all, wts.T, modp, mods, layer, y, next_norm_g)


def _sample_rows(a, width):
    return a.reshape(DEC_BATCH, SP, width)[:, :DEC_SEQ].reshape(NS, width)


def kernel(x_prompt, x_sample, c_prompt, c_sample, state_pool, cache_k, cache_v, state_hgrn, norm_g, ada_w, ada_b, w_in_ab, pool_w, pool_scale, q_norm, k_norm, attn_sinks, w_out_ab, w_in_c, lb_param, o_norm, w_out_c, router_w, router_b, moe_w1, moe_w3, moe_w2):
    n_c = BATCH + DEC_BATCH
    c_rows = ((n_c + 7) // 8) * 8
    c_all = jnp.concatenate([c_prompt, c_sample, jnp.zeros((c_rows - n_c, D_MODEL), f32)], axis=0)
    mod = _ada_mod(c_all, ada_w, ada_b).reshape(DEPTH, c_rows, 6, D_MODEL)
    modp = mod[:, :BATCH].transpose(0, 2, 1, 3).reshape(DEPTH, 6, BATCH, 1, D_MODEL)
    mods = jnp.repeat(mod[:, BATCH:n_c].transpose(0, 2, 1, 3), DEC_SEQ, axis=2)

    x_in = (x_prompt.reshape(NP, D_MODEL), x_sample.reshape(NS, D_MODEL))
    router_wt = router_w.T

    h = _normmod(x_in, norm_g[0, 0], modp, mods, 0, 0, 1)
    u = _proj(h, w_in_ab[0], 1280)
    tp = 256
    pool_p, qn_p, kn_p = _mix0_pre(
        u, u, lambda i: (jnp.maximum(i * (tp // 16) - 1, 0), 0), NP, tp, SEQ // tp, False,
        pool_w[0], pool_scale[0], q_norm[0], k_norm[0])
    nb = SEQ // WINDOW
    blk = lambda b, n: (b * nb + n, 0)
    prv = lambda b, n: (jnp.maximum(b * nb + n - 1, 0), 0)
    vcol = (D_POOL + D_ATTN + D_KV) // D_KV
    attn_p = _attn_call(
        attn_sinks[0], qn_p, kn_p, u,
        (pl.BlockSpec((WINDOW, D_ATTN), blk), pl.BlockSpec((WINDOW, D_KV), prv), pl.BlockSpec((WINDOW, D_KV), blk),
         pl.BlockSpec((WINDOW, D_KV), lambda b, n: (jnp.maximum(b * nb + n - 1, 0), vcol)),
         pl.BlockSpec((WINDOW, D_KV), lambda b, n: (b * nb + n, vcol)),
         pl.BlockSpec((WINDOW, D_ATTN), blk)),
        (BATCH, nb), NP, WINDOW, True, 1)
    u_s = u[NP:N_REAL].reshape(DEC_BATCH, DEC_SEQ, D_IN_AB)
    u_s16 = jnp.pad(u_s, ((0, 0), (0, SP - DEC_SEQ), (0, 0))).reshape(NSP, D_IN_AB)
    halo_s = jnp.pad(state_pool[0], ((0, 0), (1, 0), (0, 0))).reshape(DEC_BATCH * 16, D_POOL)
    pool_s, qn_s, kn_s = _mix0_pre(
        u_s16, halo_s, lambda i: (i, 0), NSP, SP, 1, True,
        pool_w[0], pool_scale[0], q_norm[0], k_norm[0])
    kn_new = kn_s.reshape(DEC_BATCH, SP, D_KV)[:, :DEC_SEQ]
    v_new = u_s[:, :, D_POOL + D_ATTN + D_KV:]
    zpad = jnp.zeros((DEC_BATCH, WINDOW - DEC_SEQ, D_KV), f32)
    k_ext = jnp.concatenate([cache_k[0].reshape(DEC_BATCH, WINDOW, D_KV), kn_new, zpad], axis=1)
    v_ext = jnp.concatenate([cache_v[0].reshape(DEC_BATCH, WINDOW, D_KV), v_new, zpad], axis=1)
    sg = 4
    ext_p = pl.BlockSpec((sg, WINDOW, D_KV), lambda b: (b, 0, 0))
    ext_c = pl.BlockSpec((sg, WINDOW, D_KV), lambda b: (b, 1, 0))
    attn_s = _attn_call(
        attn_sinks[0], qn_s, k_ext, v_ext,
        (pl.BlockSpec((sg * SP, D_ATTN), lambda b: (b, 0)), ext_p, ext_c, ext_p, ext_c,
         pl.BlockSpec((sg * SP, D_ATTN), lambda b: (b, 0))),
        (DEC_BATCH // sg,), NSP, SP, False, sg)
    x_all = _oproj([pool_p, attn_p], [_sample_rows(pool_s, D_POOL), _sample_rows(attn_s, D_ATTN)],
                   w_out_ab[0], x_in, modp, mods, 0, 2)
    x_all, h = _moe(x_all, norm_g[0, 1], modp, mods, 0, router_wt, router_b, moe_w1, moe_w3, moe_w2,
                    norm_g[1, 0])

    lb_soft = jax.nn.softmax(lb_param.astype(f32), axis=0)
    lower_bounds = jnp.cumsum(lb_soft, axis=0) - lb_soft[0:1]
    u1 = _proj(h, w_in_c[0], 2048)
    og_p, hg_p = _hgrn(u1, lower_bounds[1], o_norm[0], None,
                       nb=BATCH, nc=SEQ // 64, L=64, C=16, hpb=C_HEADS, n_valid=64, rows=NP)
    u1_s = jnp.pad(u1[NP:N_REAL].reshape(DEC_BATCH, DEC_SEQ, 4 * D_C),
                   ((0, 0), (0, SP - DEC_SEQ), (0, 0))).reshape(NSP, 4 * D_C)
    og_s, hg_s = _hgrn(u1_s, lower_bounds[1], o_norm[0], state_hgrn[0],
                       nb=DEC_BATCH, nc=1, L=SP, C=SP, hpb=C_HEADS, n_valid=DEC_SEQ, rows=NSP)
    x_all = _oproj([og_p], [_sample_rows(og_s, D_C)], w_out_c[0], (x_all,), modp, mods, 1, 2)
    y_p, y_s = _moe(x_all, norm_g[1, 1], modp, mods, 1, router_wt, router_b, moe_w1, moe_w3, moe_w2, None)

    y_prompt = y_p.reshape(BATCH, SEQ, D_MODEL)
    y_sample = y_s.reshape(DEC_BATCH, DEC_SEQ, D_MODEL)
    def seq_tail(a, n, c0, c1):
        return jnp.stack([a[(b + 1) * SEQ - n:(b + 1) * SEQ, c0:c1] for b in range(BATCH)])

    pool_prompt = seq_tail(u, POOL_STATE, 0, D_POOL)[None]
    pool_sample = jnp.concatenate([state_pool[0], u_s[:, :, :D_POOL]], axis=1)[:, -POOL_STATE:][None]
    kv_shape = (N_KV_HEADS, HEAD_DIM)
    k_prompt = seq_tail(kn_p, WINDOW, 0, D_KV).reshape(BATCH, WINDOW, *kv_shape)[None]
    v_prompt = seq_tail(u, WINDOW, D_POOL + D_ATTN + D_KV, D_IN_AB).reshape(BATCH, WINDOW, *kv_shape)[None]
    k_sample = jnp.concatenate([cache_k[0], kn_new.reshape(DEC_BATCH, DEC_SEQ, *kv_shape)], axis=1)[:, -WINDOW:][None]
    v_sample = jnp.concatenate([cache_v[0], v_new.reshape(DEC_BATCH, DEC_SEQ, *kv_shape)], axis=1)[:, -WINDOW:][None]
    return (y_prompt, y_sample, pool_prompt, pool_sample, k_prompt, k_sample, v_prompt, v_sample,
            hg_p[None], hg_s[None])
```
